```python
import math
import jax, jax.numpy as jnp
from jax import lax
import numpy as np

D_MODEL = 1024
BATCH = 4
SEQ = 4096
DEPTH = 4
DEC_BATCH = 32
DEC_SEQ = 64
PAST_LEN = 4096

CHUNK = 64
D_MIX = D_MODEL
A_WIDTH = D_MIX // 2
A_BLOCKS = 8
A_BLOCK_W = A_WIDTH // A_BLOCKS
CONV_W = 4
LRU_C = 8.0
B_HEAD_DIM = 64
B_WIDTH = D_MIX // 4
B_HEADS = B_WIDTH // B_HEAD_DIM
IDX_HEADS = 8
IDX_DIM = 32
TOPK_MAX = 256
C_HALF = 32
C_HEAD_DIM = 2 * C_HALF
C_WIDTH = D_MIX // 4
C_HEADS = C_WIDTH // C_HEAD_DIM
Q_BLOCK = 128
EPS = 1e-6
IN_SIZES = (A_WIDTH, A_WIDTH, B_WIDTH, B_WIDTH, B_WIDTH, B_WIDTH, IDX_HEADS * IDX_DIM, IDX_DIM, IDX_HEADS,
            C_WIDTH, C_WIDTH, C_WIDTH, C_WIDTH)
D_IN = sum(IN_SIZES)

kernel_name = 'hybrid_rglru_dsa_diffattn_stream_step'


def rmsnorm(x, g):
    xf = x.astype(jnp.float32)
    y = xf * lax.rsqrt(jnp.mean(xf * xf, axis=-1, keepdims=True) + EPS)
    return (y * g.astype(jnp.float32)).astype(x.dtype)


def causal_conv(xb, buf, w, b):
    L = xb.shape[1]
    xp = jnp.concatenate([buf, xb], axis=1)
    out = b
    for j in range(CONV_W):
        out = out + xp[:, j:j + L] * w[j]
    return out, xp[:, L:]


def rg_lru(x, h0, rw, rb, iw, ib, lam):
    B, L, _ = x.shape
    xb = x.reshape(B, L, A_BLOCKS, A_BLOCK_W)
    r = jax.nn.sigmoid((jnp.einsum('blnd,nde->blne', xb, rw).reshape(B, L, A_WIDTH) + rb).astype(jnp.float32))
    i = jax.nn.sigmoid((jnp.einsum('blnd,nde->blne', xb, iw).reshape(B, L, A_WIDTH) + ib).astype(jnp.float32))
    log_a = -LRU_C * r * jax.nn.softplus(-lam.astype(jnp.float32))
    a = jnp.exp(log_a)
    mult = jnp.sqrt(-jnp.expm1(2.0 * log_a))
    u = mult * (i * x.astype(jnp.float32))
    u = u.at[:, 0].add(a[:, 0] * h0.astype(jnp.float32))

    def combine(c1, c2):
        a1, b1 = c1
        a2, b2 = c2
        return a1 * a2, a2 * b1 + b2

    _, h = lax.associative_scan(combine, (a, u), axis=1)
    return h.astype(x.dtype), h[:, -1].astype(x.dtype)


def sweep_query_blocks(block_fn, q_arrays):
    Lq = q_arrays[0].shape[1]
    qb = Q_BLOCK if Lq % Q_BLOCK == 0 else Lq
    nb = Lq // qb

    def to_blocks(a):
        return jnp.moveaxis(a.reshape(a.shape[0], nb, qb, *a.shape[2:]), 1, 0)

    starts = jnp.arange(nb, dtype=jnp.int32) * qb
    out = lax.map(lambda args: block_fn(*args), (starts, *[to_blocks(a) for a in q_arrays]))
    out = jnp.moveaxis(out, 0, 1)
    return out.reshape(out.shape[0], Lq, *out.shape[3:])


def chunk_mask(start, qb, past_len, Lk):
    qpos = past_len + start + jnp.arange(qb, dtype=jnp.int32)
    kpos = jnp.arange(Lk, dtype=jnp.int32)
    return (kpos[None, :] // CHUNK) <= (qpos[:, None] // CHUNK)


def dsa_block(start, q, qi, w, k, v, kidx, past_len, topk):
    qb, Lk = q.shape[1], k.shape[1]
    mask = chunk_mask(start, qb, past_len, Lk)
    s_idx = jax.nn.relu(jnp.einsum('bqhd,bkd->bqhk', qi, kidx) * (IDX_DIM ** -0.5))
    score = jnp.einsum('bqhk,bqh->bqk', s_idx, w).astype(jnp.float32) * (IDX_HEADS ** -0.5)
    score = jnp.where(mask[None], score, -jnp.inf)
    top_val, top_idx = lax.top_k(score, topk)
    valid = jnp.isfinite(top_val)
    k_sel = jax.vmap(lambda kb, ib: kb[ib])(k, top_idx)
    v_sel = jax.vmap(lambda vb, ib: vb[ib])(v, top_idx)
    s = jnp.einsum('bqhd,bqkhd->bqhk', q, k_sel).astype(jnp.float32) * (B_HEAD_DIM ** -0.5)
    s = jnp.where(valid[:, :, None, :], s, -jnp.inf)
    p = jax.nn.softmax(s, axis=-1).astype(v.dtype)
    return jnp.einsum('bqhk,bqkhd->bqhd', p, v_sel)


def diff_block(start, q, k, v, past_len, lam, lam_init, subln):
    qb, Lk = q.shape[1], k.shape[1]
    mask = chunk_mask(start, qb, past_len, Lk)
    s = jnp.einsum('bqhid,bkhid->bhiqk', q, k).astype(jnp.float32) * (C_HALF ** -0.5)
    s = jnp.where(mask, s, -jnp.inf)
    p = jax.nn.softmax(s, axis=-1)
    attn = (p[:, :, 0] - lam * p[:, :, 1]).astype(v.dtype)
    o = jnp.einsum('bhqk,bkhe->bqhe', attn, v)
    return rmsnorm(o, subln) * (1.0 - lam_init)


def layer(x, p, conv_buf, h0, past, lam_init):
    B, L, _ = x.shape
    past_bk, past_bv, past_bki, past_ck, past_cv = past
    past_len = past_bk.shape[1]
    xn = rmsnorm(x, p['norm_pre'])
    proj = jnp.einsum('bld,de->ble', xn, p['w_in'])
    offs = np.cumsum(IN_SIZES)[:-1].tolist()
    (a_x, a_g, b_q, b_k, b_v, b_g, b_iq, b_ik, b_iw, c_q, c_k, c_v, c_g) = jnp.split(proj, offs, axis=-1)

    a_c, new_buf = causal_conv(a_x, conv_buf, p['a_conv_w'], p['a_conv_b'])
    a_h, h_last = rg_lru(a_c, h0, p['a_rg_w'], p['a_rg_b'], p['a_in_w'], p['a_in_b'], p['a_lambda'])
    y_a = a_h * jax.nn.silu(a_g)

    bq = b_q.reshape(B, L, B_HEADS, B_HEAD_DIM)
    bk = b_k.reshape(B, L, B_HEADS, B_HEAD_DIM)
    bv = b_v.reshape(B, L, B_HEADS, B_HEAD_DIM)
    biq = b_iq.reshape(B, L, IDX_HEADS, IDX_DIM)
    bk_all = jnp.concatenate([past_bk, bk], axis=1)
    bv_all = jnp.concatenate([past_bv, bv], axis=1)
    bki_all = jnp.concatenate([past_bki, b_ik], axis=1)
    topk = min(TOPK_MAX, bk_all.shape[1] // 4)
    y_b = sweep_query_blocks(
        lambda s, q, qi, w: dsa_block(s, q, qi, w, bk_all, bv_all, bki_all, past_len, topk),
        (bq, biq, b_iw))
    y_b = y_b.reshape(B, L, B_WIDTH) * jax.nn.silu(b_g)

    cq = c_q.reshape(B, L, C_HEADS, 2, C_HALF)
    ck = c_k.reshape(B, L, C_HEADS, C_HEAD_DIM)
    cv = c_v.reshape(B, L, C_HEADS, C_HEAD_DIM)
    Lk = past_len + L
    ck_all = jnp.concatenate([past_ck, ck], axis=1).reshape(B, Lk, C_HEADS, 2, C_HALF)
    cv_all = jnp.concatenate([past_cv, cv], axis=1)
    f32 = jnp.float32
    lam = (jnp.exp(jnp.sum(p['c_lam_q1'].astype(f32) * p['c_lam_k1'].astype(f32)))
           - jnp.exp(jnp.sum(p['c_lam_q2'].astype(f32) * p['c_lam_k2'].astype(f32))) + lam_init)
    y_c = sweep_query_blocks(
        lambda s, q: diff_block(s, q, ck_all, cv_all, past_len, lam, lam_init, p['c_subln']),
        (cq,))
    y_c = y_c.reshape(B, L, C_WIDTH) * jax.nn.silu(c_g)

    mix = jnp.einsum('ble,ed->bld', jnp.concatenate([y_a, y_b, y_c], axis=-1), p['w_out'])
    x = x + rmsnorm(mix, p['norm_post'])
    return x, (new_buf, h_last, bk, bv, b_ik, ck, cv)


def setup_inputs(seed: int = 0) -> dict:
    key = jax.random.key(seed)
    ks = jax.random.split(key, 32)
    nrm = jax.random.normal
    u = jax.random.uniform(ks[19], (DEPTH, A_WIDTH), minval=0.9, maxval=0.999)
    a0 = u ** (1.0 / LRU_C)
    return {
        'x_prompt': nrm(ks[0], (BATCH, SEQ, D_MODEL), jnp.float32),
        'x_sample': nrm(ks[1], (DEC_BATCH, DEC_SEQ, D_MODEL), jnp.float32),
        'cache_a_conv': nrm(ks[2], (DEPTH, DEC_BATCH, CONV_W - 1, A_WIDTH), jnp.float32),
        'state_a_h': 0.5 * nrm(ks[3], (DEPTH, DEC_BATCH, A_WIDTH), jnp.float32),
        'cache_b_k': nrm(ks[4], (DEPTH, DEC_BATCH, PAST_LEN, B_HEADS, B_HEAD_DIM), jnp.float32),
        'cache_b_v': nrm(ks[5], (DEPTH, DEC_BATCH, PAST_LEN, B_HEADS, B_HEAD_DIM), jnp.float32),
        'cache_b_kidx': nrm(ks[6], (DEPTH, DEC_BATCH, PAST_LEN, IDX_DIM), jnp.float32),
        'cache_c_k': nrm(ks[7], (DEPTH, DEC_BATCH, PAST_LEN, C_HEADS, C_HEAD_DIM), jnp.float32),
        'cache_c_v': nrm(ks[8], (DEPTH, DEC_BATCH, PAST_LEN, C_HEADS, C_HEAD_DIM), jnp.float32),
        'norm_pre': 1.0 + 0.05 * nrm(ks[9], (DEPTH, D_MODEL), jnp.float32),
        'norm_post': 1.0 + 0.05 * nrm(ks[10], (DEPTH, D_MODEL), jnp.float32),
        'w_in': nrm(ks[11], (DEPTH, D_MODEL, D_IN), jnp.float32) * D_MODEL ** -0.5,
        'w_out': nrm(ks[12], (DEPTH, D_MIX, D_MODEL), jnp.float32) * D_MIX ** -0.5,
        'a_conv_w': nrm(ks[13], (DEPTH, CONV_W, A_WIDTH), jnp.float32) * CONV_W ** -0.5,
        'a_conv_b': 0.01 * nrm(ks[14], (DEPTH, A_WIDTH), jnp.float32),
        'a_rg_w': nrm(ks[15], (DEPTH, A_BLOCKS, A_BLOCK_W, A_BLOCK_W), jnp.float32) * A_BLOCK_W ** -0.5,
        'a_rg_b': 0.01 * nrm(ks[16], (DEPTH, A_WIDTH), jnp.float32),
        'a_in_w': nrm(ks[17], (DEPTH, A_BLOCKS, A_BLOCK_W, A_BLOCK_W), jnp.float32) * A_BLOCK_W ** -0.5,
        'a_in_b': 0.01 * nrm(ks[18], (DEPTH, A_WIDTH), jnp.float32),
        'a_lambda': jnp.log(a0) - jnp.log1p(-a0),
        'c_lam_q1': 0.1 * nrm(ks[20], (DEPTH, C_HALF), jnp.float32),
        'c_lam_k1': 0.1 * nrm(ks[21], (DEPTH, C_HALF), jnp.float32),
        'c_lam_q2': 0.1 * nrm(ks[22], (DEPTH, C_HALF), jnp.float32),
        'c_lam_k2': 0.1 * nrm(ks[23], (DEPTH, C_HALF), jnp.float32),
        'c_subln': 1.0 + 0.05 * nrm(ks[24], (DEPTH, C_HEAD_DIM), jnp.float32),
    }


def reference(x_prompt, x_sample, cache_a_conv, state_a_h, cache_b_k, cache_b_v, cache_b_kidx, cache_c_k, cache_c_v,
              norm_pre, norm_post, w_in, w_out, a_conv_w, a_conv_b, a_rg_w, a_rg_b, a_in_w, a_in_b, a_lambda,
              c_lam_q1, c_lam_k1, c_lam_q2, c_lam_k2, c_subln):
    Bp = x_prompt.shape[0]
    dt = x_prompt.dtype
    empty_past = (jnp.zeros((Bp, 0, B_HEADS, B_HEAD_DIM), dt), jnp.zeros((Bp, 0, B_HEADS, B_HEAD_DIM), dt),
                  jnp.zeros((Bp, 0, IDX_DIM), dt),
                  jnp.zeros((Bp, 0, C_HEADS, C_HEAD_DIM), dt), jnp.zeros((Bp, 0, C_HEADS, C_HEAD_DIM), dt))
    zero_buf = jnp.zeros((Bp, CONV_W - 1, A_WIDTH), dt)
    zero_h = jnp.zeros((Bp, A_WIDTH), dt)
    xp, xs = x_prompt, x_sample
    p_st, s_st = [], []
    for l in range(DEPTH):
        p = {'norm_pre': norm_pre[l], 'norm_post': norm_post[l], 'w_in': w_in[l], 'w_out': w_out[l],
             'a_conv_w': a_conv_w[l], 'a_conv_b': a_conv_b[l], 'a_rg_w': a_rg_w[l], 'a_rg_b': a_rg_b[l],
             'a_in_w': a_in_w[l], 'a_in_b': a_in_b[l], 'a_lambda': a_lambda[l],
             'c_lam_q1': c_lam_q1[l], 'c_lam_k1': c_lam_k1[l], 'c_lam_q2': c_lam_q2[l], 'c_lam_k2': c_lam_k2[l],
             'c_subln': c_subln[l]}
        lam_init = 0.8 - 0.6 * math.exp(-0.3 * l)
        xp, st_p = layer(xp, p, zero_buf, zero_h, empty_past, lam_init)
        xs, st_s = layer(xs, p, cache_a_conv[l], state_a_h[l],
                         (cache_b_k[l], cache_b_v[l], cache_b_kidx[l], cache_c_k[l], cache_c_v[l]), lam_init)
        p_st.append(st_p)
        s_st.append(st_s)

    def stk(states, j):
        return jnp.stack([s[j] for s in states], axis=0)

    return (xp, xs,
            stk(p_st, 0), stk(p_st, 1), stk(p_st, 2), stk(p_st, 3), stk(p_st, 4), stk(p_st, 5), stk(p_st, 6),
            stk(s_st, 0), stk(s_st, 1), stk(s_st, 2), stk(s_st, 3), stk(s_st, 4), stk(s_st, 5), stk(s_st, 6))
```

```python
import functools
import math

import jax
import jax.numpy as jnp
from jax import lax
from jax.experimental import pallas as pl
from jax.experimental.pallas import tpu as pltpu

F32 = jnp.float32
BF16 = jnp.bfloat16
I32 = jnp.int32

CHUNK = 64
CONV_W = 4
LRU_C = 8.0
A_BLOCKS = 8
B_HEADS = 4
B_HEAD_DIM = 64
IDX_HEADS = 8
IDX_DIM = 32
TOPK_MAX = 256
C_HEADS = 4
C_HALF = 32
C_HEAD_DIM = 2 * C_HALF
EPS = 1e-6
NEG = -1e30
INT_MIN = -2 ** 31
VMEM_LIMIT_BYTES = 56 * 1024 * 1024


def _cparams(sem):
    return pltpu.CompilerParams(dimension_semantics=sem, vmem_limit_bytes=VMEM_LIMIT_BYTES)


def _silu(g):
    return g * jax.nn.sigmoid(g)


def _inproj_kernel(x_ref, g_ref, wr_ref, wf_ref,
                   ax_ref, ag_ref, bk_ref, bv_ref, bg_ref, ck_ref, cv_ref, cg_ref, bik_ref,
                   bqT_ref, biqT_ref, cqT_ref, biwT_ref, *, aw, hw):
    x = x_ref[...]
    ms = jnp.mean(x * x, axis=-1, keepdims=True)
    xn = (x * lax.rsqrt(ms + EPS) * g_ref[...]).astype(BF16)
    proj = jnp.dot(xn, wr_ref[...], preferred_element_type=F32)
    o = 0
    for ref, n in ((ax_ref, aw), (ag_ref, aw), (bk_ref, hw), (bv_ref, hw), (bg_ref, hw),
                   (ck_ref, hw), (cv_ref, hw), (cg_ref, hw), (bik_ref, IDX_DIM)):
        ref[...] = proj[:, o:o + n]
        o += n
    projT = lax.dot_general(wf_ref[...], xn, (((1,), (1,)), ((), ())),
                            preferred_element_type=F32)
    bqT_ref[...] = projT[0:hw].astype(BF16)
    biqT_ref[...] = projT[hw:2 * hw].astype(BF16)
    cqT_ref[...] = projT[2 * hw:3 * hw].astype(BF16)
    biwT_ref[...] = projT[3 * hw:3 * hw + IDX_HEADS]


def _inproj(x2d, g, wr, wf, *, tm):
    T, D = x2d.shape
    aw, hw = D // 2, D // 4
    nr, nf = wr.shape[1], wf.shape[0]
    row = lambda n: pl.BlockSpec((tm, n), lambda i: (i, 0))
    col = lambda n: pl.BlockSpec((n, tm), lambda i: (0, i))
    const = lambda s: pl.BlockSpec(s, lambda i: (0, 0))
    out_shape = ([jax.ShapeDtypeStruct((T, n), F32) for n in (aw, aw, hw, hw, hw, hw, hw, hw, IDX_DIM)]
                 + [jax.ShapeDtypeStruct((hw, T), BF16)] * 3 + [jax.ShapeDtypeStruct((IDX_HEADS, T), F32)])
    out_specs = ([row(n) for n in (aw, aw, hw, hw, hw, hw, hw, hw, IDX_DIM)]
                 + [col(hw)] * 3 + [col(IDX_HEADS)])
    return pl.pallas_call(
        functools.partial(_inproj_kernel, aw=aw, hw=hw),
        grid=(T // tm,),
        in_specs=[row(D), const((1, D)), const((D, nr)), const((nf, D))],
        out_specs=out_specs, out_shape=out_shape,
        compiler_params=_cparams(("parallel",)),
    )(x2d, g, wr, wf)


def _amix_kernel(ax_ref, ag_ref, buf_ref, h0_ref, cw_ref, cb_ref, rw_ref, rb_ref, iw_ref, ib_ref, lam_ref,
                 ya_ref, nbuf_ref, hlast_ref, xbuf_ref, hc_ref, *, tl, nl, aw):
    li = pl.program_id(1)

    @pl.when(li == 0)
    def _():
        xbuf_ref[0:8, :] = jnp.zeros((8, aw), F32)
        xbuf_ref[8 - (CONV_W - 1):8, :] = buf_ref[...]
        hc_ref[...] = h0_ref[...]

    x = ax_ref[...]
    xbuf_ref[8:8 + tl, :] = x
    cw = cw_ref[...]
    conv = cb_ref[...] + x * cw[CONV_W - 1:CONV_W, :]
    for s in range(1, CONV_W):
        conv = conv + xbuf_ref[8 - s:8 - s + tl, :] * cw[CONV_W - 1 - s:CONV_W - s, :]

    cbf = conv.astype(BF16)
    r = jax.nn.sigmoid(jnp.dot(cbf, rw_ref[...], preferred_element_type=F32) + rb_ref[...])
    ig = jax.nn.sigmoid(jnp.dot(cbf, iw_ref[...], preferred_element_type=F32) + ib_ref[...])
    nl_lam = -lam_ref[...]
    sp = jnp.maximum(nl_lam, 0.0) + jnp.log1p(jnp.exp(-jnp.abs(nl_lam)))
    log_a = (-LRU_C) * r * sp
    a = jnp.exp(log_a)
    u = jnp.sqrt(1.0 - a * a) * (ig * conv)

    row = lax.broadcasted_iota(I32, (tl, aw), 0)
    s = 1
    while s < tl:
        a_sh = pltpu.roll(a, s, 0)
        u_sh = pltpu.roll(u, s, 0)
        keep = row >= s
        u = jnp.where(keep, a * u_sh + u, u)
        a = jnp.where(keep, a * a_sh, a)
        s *= 2
    h = a * hc_ref[...] + u
    ya_ref[...] = h * _silu(ag_ref[...])
    hc_ref[...] = h[tl - 1:tl, :]
    xbuf_ref[0:8, :] = x[tl - 8:tl, :]

    @pl.when(li == nl - 1)
    def _():
        nbuf_ref[...] = xbuf_ref[8 + tl - (CONV_W - 1):8 + tl, :]
        hlast_ref[...] = h[tl - 1:tl, :]


def _amix(ax, ag, buf, h0, cw, cb, rw, rb, iw, ib, lam, *, tl):
    B, L, aw = ax.shape
    nl = L // tl
    seq = pl.BlockSpec((None, tl, aw), lambda b, l: (b, l, 0))
    perb = lambda r: pl.BlockSpec((None, r, aw), lambda b, l: (b, 0, 0))
    const = lambda s: pl.BlockSpec(s, lambda b, l: (0, 0))
    return pl.pallas_call(
        functools.partial(_amix_kernel, tl=tl, nl=nl, aw=aw),
        grid=(B, nl),
        in_specs=[seq, seq, perb(CONV_W - 1), perb(1), const((CONV_W, aw)), const((1, aw)),
                  const((aw, aw)), const((1, aw)), const((aw, aw)), const((1, aw)), const((1, aw))],
        out_specs=[seq, perb(CONV_W - 1), perb(1)],
        out_shape=[jax.ShapeDtypeStruct((B, L, aw), F32), jax.ShapeDtypeStruct((B, CONV_W - 1, aw), F32),
                   jax.ShapeDtypeStruct((B, 1, aw), F32)],
        scratch_shapes=[pltpu.VMEM((tl + 8, aw), F32), pltpu.VMEM((1, aw), F32)],
        compiler_params=_cparams(("parallel", "arbitrary")),
    )(ax, ag, buf, h0, cw, cb, rw, rb, iw, ib, lam)


def _col_reduce(x, op):
    rows, n = x.shape
    return op(op(x.reshape(rows // 8, 8, n), axis=0), axis=0, keepdims=True)


def _diag_valid(rows, tq):
    kc = lax.broadcasted_iota(I32, (rows, tq), 0) // CHUNK
    qc = lax.broadcasted_iota(I32, (rows, tq), 1) // CHUNK
    return kc <= qc


def _softmax_step(s, sel, vt_h, m_ref, l_ref, acc_ref, idx, d0, dn):
    m_old = m_ref[idx]
    sm = s if sel is None else jnp.where(sel, s, NEG)
    m_new = jnp.maximum(m_old, _col_reduce(sm, jnp.max))
    alpha = jnp.exp(m_old - m_new)
    p = jnp.exp(sm - m_new)
    if sel is not None:
        p = jnp.where(sel, p, 0.0)
    l_ref[idx] = alpha * l_ref[idx] + _col_reduce(p, jnp.sum)
    pv = jnp.dot(vt_h, p.astype(BF16), preferred_element_type=F32)
    acc_ref[idx, :, :] = alpha * acc_ref[idx, :, :] + pv
    m_ref[idx] = m_new


def _dsa_kernel(*refs, P, tq, tqr, tkp, tkn, topk, hw, idx_bits):
    it = iter(refs)
    qT_ref, qiT_ref, wT_ref, g_ref = next(it), next(it), next(it), next(it)
    if P > 0:
        pk_ref, pv_ref, pki_ref = next(it), next(it), next(it)
    nk_ref, nv_ref, nki_ref = next(it), next(it), next(it)
    y_ref = next(it)
    keys_ref, qbd_ref, m_ref, l_ref, acc_ref, cut_ref = (next(it) for _ in range(6))
    j = pl.program_id(1)
    nnew = j + 1
    masked_diag = tkn > CHUNK

    w = wT_ref[...]

    def idx_keys(ki_tile, valid):
        kb = ki_tile.astype(BF16)
        acc = None
        for h in range(IDX_HEADS):
            s = jnp.dot(kb, qiT_ref[IDX_DIM * h:IDX_DIM * (h + 1), :], preferred_element_type=F32)
            t = jnp.maximum(s, 0.0) * w[h:h + 1, :]
            acc = t if acc is None else acc + t
        acc = jnp.where(acc == 0.0, 0.0, acc)
        bits = lax.bitcast_convert_type(acc, I32)
        key = bits ^ ((bits >> 31) & 0x7FFFFFFF)
        return key if valid is None else jnp.where(valid, key, INT_MIN)

    if P > 0:
        def past_keys(t, c):
            r0 = pl.multiple_of(t * tkp, tkp)
            keys_ref[pl.ds(r0, tkp), :] = idx_keys(pki_ref[pl.ds(r0, tkp), :], None)
            return c
        lax.fori_loop(0, P // tkp, past_keys, 0)

    def new_keys(i, c):
        r0 = pl.multiple_of(i * tkn, tkn)
        keys_ref[pl.ds(P + r0, tkn), :] = idx_keys(nki_ref[pl.ds(r0, tkn), :], None)
        return c
    lax.fori_loop(0, j, new_keys, 0)
    rd = pl.multiple_of(j * tkn, tkn)
    keys_ref[pl.ds(P + rd, tkn), :] = idx_keys(nki_ref[pl.ds(rd, tkn), :],
                                               _diag_valid(tkn, tq) if masked_diag else None)

    def count(pred):
        tot = jnp.zeros((8, tq), I32)
        if P > 0:
            def pb(t, a):
                r0 = pl.multiple_of(t * tkp, tkp)
                ind = pred(keys_ref[pl.ds(r0, tkp), :], r0)
                return a + jnp.sum(ind.reshape(tkp // 8, 8, tq), axis=0)
            tot = lax.fori_loop(0, P // tkp, pb, tot)

        def nb(i, a):
            r0 = pl.multiple_of(i * tkn, tkn)
            ind = pred(keys_ref[pl.ds(P + r0, tkn), :], P + r0)
            return a + jnp.sum(ind.reshape(tkn // 8, 8, tq), axis=0)
        tot = lax.fori_loop(0, nnew, nb, tot)
        return jnp.sum(tot, axis=0, keepdims=True)

    def bisect(step, ans):
        cand = ans + jnp.left_shift(jnp.int32(1), 31 - step)
        cnt = count(lambda c, r0: jnp.where(c >= cand, 1, 0))
        return jnp.where(cnt >= topk, cand, ans)
    ans = lax.fori_loop(0, 32, bisect, jnp.full((1, tq), INT_MIN, I32))

    n_gt = count(lambda c, r0: jnp.where(c > ans, 1, 0))
    n_ge = count(lambda c, r0: jnp.where(c >= ans, 1, 0))
    lane_real = lax.broadcasted_iota(I32, (1, tq), 1) < tqr
    need = (n_ge > topk) & (ans != INT_MIN) & lane_real
    take = topk - n_gt
    cut_ref[...] = jnp.full((1, tq), 2 ** 30, I32)

    @pl.when(jnp.max(jnp.where(need, 1, 0)) > 0)
    def _():
        def rows_of(c, r0):
            return lax.broadcasted_iota(I32, c.shape, 0) + r0

        def bis_row(step, x):
            t = x + jnp.left_shift(jnp.int32(1), idx_bits - 1 - step)
            cnt = count(lambda c, r0: jnp.where((c == ans) & (rows_of(c, r0) < t), 1, 0))
            return jnp.where(cnt < take, t, x)
        cut = lax.fori_loop(0, idx_bits, bis_row, jnp.zeros((1, tq), I32))
        cut_ref[...] = jnp.where(need, cut, 2 ** 30)
        cutv = cut_ref[...]

        def demote(c, r0):
            return jnp.where((c == ans) & (rows_of(c, r0) > cutv), c - 1, c)
        if P > 0:
            def pd(t, z):
                r0 = pl.multiple_of(t * tkp, tkp)
                keys_ref[pl.ds(r0, tkp), :] = demote(keys_ref[pl.ds(r0, tkp), :], r0)
                return z
            lax.fori_loop(0, P // tkp, pd, 0)

        def nd(i, z):
            r0 = pl.multiple_of(i * tkn, tkn)
            keys_ref[pl.ds(P + r0, tkn), :] = demote(keys_ref[pl.ds(P + r0, tkn), :], P + r0)
            return z
        lax.fori_loop(0, nnew, nd, 0)

    thr = jnp.maximum(ans, INT_MIN + 1)

    qbd_ref[...] = jnp.zeros(qbd_ref.shape, BF16)
    for h in range(B_HEADS):
        qbd_ref[B_HEAD_DIM * h:B_HEAD_DIM * (h + 1), h * tq:(h + 1) * tq] = \
            qT_ref[B_HEAD_DIM * h:B_HEAD_DIM * (h + 1), :]
    m_ref[...] = jnp.full(m_ref.shape, NEG, F32)
    l_ref[...] = jnp.zeros(l_ref.shape, F32)
    acc_ref[...] = jnp.zeros(acc_ref.shape, F32)

    def attend(k_tile, v_tile, key_tile):
        sel = key_tile >= thr
        s_all = jnp.dot(k_tile.astype(BF16), qbd_ref[...], preferred_element_type=F32)
        vt = v_tile.T.astype(BF16)
        for h in range(B_HEADS):
            _softmax_step(s_all[:, h * tq:(h + 1) * tq], sel, vt[B_HEAD_DIM * h:B_HEAD_DIM * (h + 1), :],
                          m_ref, l_ref, acc_ref, h, B_HEAD_DIM * h, B_HEAD_DIM)

    if P > 0:
        def pa(t, z):
            r0 = pl.multiple_of(t * tkp, tkp)
            attend(pk_ref[pl.ds(r0, tkp), :], pv_ref[pl.ds(r0, tkp), :], keys_ref[pl.ds(r0, tkp), :])
            return z
        lax.fori_loop(0, P // tkp, pa, 0)

    def na(i, z):
        r0 = pl.multiple_of(i * tkn, tkn)
        attend(nk_ref[pl.ds(r0, tkn), :], nv_ref[pl.ds(r0, tkn), :], keys_ref[pl.ds(P + r0, tkn), :])
        return z
    lax.fori_loop(0, nnew, na, 0)

    oT = jnp.concatenate([acc_ref[h, :, :] / l_ref[h] for h in range(B_HEADS)], axis=0)
    o = oT.T
    y_ref[...] = o[0:tqr, :] * _silu(g_ref[...])


def _dsa(qT, qiT, wT, gate, past, new, *, B, L, P, tq, tqr, feat3d):
    nk, nv, nki = new
    hw = nk.shape[-1]
    nqb = L // tqr
    tkn, tkp = tqr, 256
    Lk = P + L
    topk = min(TOPK_MAX, Lk // 4)
    idx_bits = max(1, math.ceil(math.log2(Lk + 1)))
    if feat3d:
        feat = lambda r: pl.BlockSpec((None, r, tq), lambda b, j: (b, 0, j))
    else:
        feat = lambda r: pl.BlockSpec((r, tq), lambda b, j: (0, b * nqb + j))
    seqblk = pl.BlockSpec((None, tqr, hw), lambda b, j: (b, j, 0))
    whole = lambda n, c: pl.BlockSpec((None, n, c), lambda b, j: (b, 0, 0))
    in_specs = [feat(hw), feat(hw), feat(IDX_HEADS), seqblk]
    args = [qT, qiT, wT, gate]
    if P > 0:
        in_specs += [whole(P, hw), whole(P, hw), whole(P, IDX_DIM)]
        args += list(past)
    in_specs += [whole(L, hw), whole(L, hw), whole(L, IDX_DIM)]
    args += [nk, nv, nki]
    return pl.pallas_call(
        functools.partial(_dsa_kernel, P=P, tq=tq, tqr=tqr, tkp=tkp, tkn=tkn, topk=topk, hw=hw,
                          idx_bits=idx_bits),
        grid=(B, nqb),
        in_specs=in_specs,
        out_specs=seqblk,
        out_shape=jax.ShapeDtypeStruct((B, L, hw), F32),
        scratch_shapes=[pltpu.VMEM((Lk, tq), I32), pltpu.VMEM((hw, B_HEADS * tq), BF16),
                        pltpu.VMEM((B_HEADS, 1, tq), F32), pltpu.VMEM((B_HEADS, 1, tq), F32),
                        pltpu.VMEM((B_HEADS, B_HEAD_DIM, tq), F32), pltpu.VMEM((1, tq), I32)],
        compiler_params=_cparams(("parallel", "arbitrary")),
    )(*args)


def _diff_kernel(*refs, P, tq, tqr, tkp, tkn, hw, lam_init):
    it = iter(refs)
    qT_ref, g_ref, lq1_ref, lk1_ref, lq2_ref, lk2_ref, sub_ref = (next(it) for _ in range(7))
    if P > 0:
        pk_ref, pv_ref = next(it), next(it)
    nk_ref, nv_ref = next(it), next(it)
    y_ref = next(it)
    qbd_ref, m_ref, l_ref, acc_ref = (next(it) for _ in range(4))
    j = pl.program_id(1)
    nmaps = 2 * C_HEADS

    qbd_ref[...] = jnp.zeros(qbd_ref.shape, BF16)
    for mi in range(nmaps):
        qbd_ref[C_HALF * mi:C_HALF * (mi + 1), mi * tq:(mi + 1) * tq] = qT_ref[C_HALF * mi:C_HALF * (mi + 1), :]
    m_ref[...] = jnp.full(m_ref.shape, NEG, F32)
    l_ref[...] = jnp.zeros(l_ref.shape, F32)
    acc_ref[...] = jnp.zeros(acc_ref.shape, F32)

    def attend(k_tile, v_tile, valid):
        s_all = jnp.dot(k_tile.astype(BF16), qbd_ref[...], preferred_element_type=F32)
        vt = v_tile.T.astype(BF16)
        for mi in range(nmaps):
            h = mi // 2
            s = s_all[:, mi * tq:(mi + 1) * tq]
            if valid is not None:
                s = jnp.where(valid, s, NEG)
            _softmax_step(s, None, vt[C_HEAD_DIM * h:C_HEAD_DIM * (h + 1), :], m_ref, l_ref, acc_ref, mi,
                          0, C_HEAD_DIM)

    if P > 0:
        def pa(t, z):
            r0 = pl.multiple_of(t * tkp, tkp)
            attend(pk_ref[pl.ds(r0, tkp), :], pv_ref[pl.ds(r0, tkp), :], None)
            return z
        lax.fori_loop(0, P // tkp, pa, 0)

    def na(i, z):
        r0 = pl.multiple_of(i * tkn, tkn)
        attend(nk_ref[pl.ds(r0, tkn), :], nv_ref[pl.ds(r0, tkn), :], None)
        return z
    lax.fori_loop(0, j, na, 0)
    rd = pl.multiple_of(j * tkn, tkn)
    attend(nk_ref[pl.ds(rd, tkn), :], nv_ref[pl.ds(rd, tkn), :],
           _diag_valid(tkn, tq) if tkn > CHUNK else None)

    lam = (jnp.exp(jnp.sum(lq1_ref[...] * lk1_ref[...], axis=-1, keepdims=True))
           - jnp.exp(jnp.sum(lq2_ref[...] * lk2_ref[...], axis=-1, keepdims=True)) + lam_init)
    outs = []
    for h in range(C_HEADS):
        o = acc_ref[2 * h, :, :] / l_ref[2 * h] - lam * (acc_ref[2 * h + 1, :, :] / l_ref[2 * h + 1])
        ms = jnp.mean(o * o, axis=0, keepdims=True)
        outs.append(o * lax.rsqrt(ms + EPS))
    o = jnp.concatenate(outs, axis=0).T
    y_ref[...] = (o[0:tqr, :] * sub_ref[...]) * (1.0 - lam_init) * _silu(g_ref[...])


def _diff(qT, gate, lams, subrow, past, new, *, B, L, P, tq, tqr, feat3d, lam_init):
    nk, nv = new
    hw = nk.shape[-1]
    nqb = L // tqr
    tkn, tkp = tqr, 256
    nmaps = 2 * C_HEADS
    if feat3d:
        feat = pl.BlockSpec((None, hw, tq), lambda b, j: (b, 0, j))
    else:
        feat = pl.BlockSpec((hw, tq), lambda b, j: (0, b * nqb + j))
    seqblk = pl.BlockSpec((None, tqr, hw), lambda b, j: (b, j, 0))
    whole = lambda n, c: pl.BlockSpec((None, n, c), lambda b, j: (b, 0, 0))
    const = lambda s: pl.BlockSpec(s, lambda b, j: (0, 0))
    in_specs = [feat, seqblk] + [const((1, C_HALF))] * 4 + [const((1, hw))]
    args = [qT, gate, *lams, subrow]
    if P > 0:
        in_specs += [whole(P, hw), whole(P, hw)]
        args += list(past)
    in_specs += [whole(L, hw), whole(L, hw)]
    args += [nk, nv]
    return pl.pallas_call(
        functools.partial(_diff_kernel, P=P, tq=tq, tqr=tqr, tkp=tkp, tkn=tkn, hw=hw, lam_init=lam_init),
        grid=(B, nqb),
        in_specs=in_specs,
        out_specs=seqblk,
        out_shape=jax.ShapeDtypeStruct((B, L, hw), F32),
        scratch_shapes=[pltpu.VMEM((hw, nmaps * tq), BF16),
                        pltpu.VMEM((nmaps, 1, tq), F32), pltpu.VMEM((nmaps, 1, tq), F32),
                        pltpu.VMEM((nmaps, C_HEAD_DIM, tq), F32)],
        compiler_params=_cparams(("parallel", "arbitrary")),
    )(*args)


def _outproj_kernel(ya_ref, yb_ref, yc_ref, x_ref, w_ref, g_ref, o_ref, *, aw, hw):
    mix = jnp.dot(ya_ref[...].astype(BF16), w_ref[0:aw, :], preferred_element_type=F32)
    mix = mix + jnp.dot(yb_ref[...].astype(BF16), w_ref[aw:aw + hw, :], preferred_element_type=F32)
    mix = mix + jnp.dot(yc_ref[...].astype(BF16), w_ref[aw + hw:aw + 2 * hw, :], preferred_element_type=F32)
    ms = jnp.mean(mix * mix, axis=-1, keepdims=True)
    o_ref[...] = x_ref[...] + mix * lax.rsqrt(ms + EPS) * g_ref[...]


def _outproj(ya, yb, yc, x2d, w, g, *, tm):
    T, D = x2d.shape
    aw, hw = D // 2, D // 4
    row = lambda n: pl.BlockSpec((tm, n), lambda i: (i, 0))
    const = lambda s: pl.BlockSpec(s, lambda i: (0, 0))
    return pl.pallas_call(
        functools.partial(_outproj_kernel, aw=aw, hw=hw),
        grid=(T // tm,),
        in_specs=[row(aw), row(hw), row(hw), row(D), const((D, D)), const((1, D))],
        out_specs=row(D),
        out_shape=jax.ShapeDtypeStruct((T, D), F32),
        compiler_params=_cparams(("parallel",)),
    )(ya, yb, yc, x2d, w, g)


def _split_w_in(w_in, D):
    aw, hw = D // 2, D // 4
    sizes = (aw, aw, hw, hw, hw, hw, IDX_HEADS * IDX_DIM, IDX_DIM, IDX_HEADS, hw, hw, hw, hw)
    names = ("a_x", "a_g", "b_q", "b_k", "b_v", "b_g", "b_iq", "b_ik", "b_iw", "c_q", "c_k", "c_v", "c_g")
    seg, o = {}, 0
    for n, s in zip(names, sizes):
        seg[n] = w_in[..., o:o + s]
        o += s
    depth = w_in.shape[0]
    wr = jnp.concatenate([seg[n] for n in ("a_x", "a_g", "b_k", "b_v", "b_g", "c_k", "c_v", "c_g", "b_ik")]
                         + [jnp.zeros((depth, D, 128 - IDX_DIM), w_in.dtype)], axis=-1).astype(BF16)
    wf = jnp.concatenate([seg["b_q"] * (B_HEAD_DIM ** -0.5), seg["b_iq"], seg["c_q"] * (C_HALF ** -0.5),
                          seg["b_iw"] * ((IDX_DIM ** -0.5) * (IDX_HEADS ** -0.5)),
                          jnp.zeros((depth, D, 16 - IDX_HEADS), w_in.dtype)], axis=-1)
    return wr, jnp.swapaxes(wf, 1, 2).astype(BF16)


def _block_diag(w):
    depth, nb, n, _ = w.shape
    eye = jnp.eye(nb, dtype=w.dtype)
    return jnp.einsum("lnde,nm->lndme", w, eye).reshape(depth, nb * n, nb * n)


def _layer(x, B, L, P, wl, conv_buf, h0, past, lam_init):
    T, D = x.shape
    hw = D // 4
    prompt = P == 0
    tm = min(256, T)
    (ax, ag, bk, bv, bg, ck, cv, cg, bik, bqT, biqT, cqT, biwT) = _inproj(
        x, wl["norm_pre"], wl["wr"], wl["wf"], tm=tm)
    r3 = lambda a: a.reshape(B, L, a.shape[-1])

    ya, nbuf, hlast = _amix(r3(ax), r3(ag), conv_buf, h0[:, None, :], wl["a_conv_w"], wl["a_conv_b"],
                            wl["rw"], wl["a_rg_b"], wl["iw"], wl["a_in_b"], wl["a_lambda"],
                            tl=min(L, 512))

    if prompt:
        tq = tqr = 256
        feat3d = False
        fm = lambda a: a
    else:
        tqr, tq = L, 128
        feat3d = True
        fm = lambda a: jnp.pad(a.reshape(a.shape[0], B, L).transpose(1, 0, 2), ((0, 0), (0, 0), (0, tq - L)))
    past_b = None if prompt else tuple(p.reshape(B, P, -1) for p in past[:3])
    past_c = None if prompt else tuple(p.reshape(B, P, -1) for p in past[3:])
    yb = _dsa(fm(bqT), fm(biqT), fm(biwT), r3(bg), past_b, (r3(bk), r3(bv), r3(bik)),
              B=B, L=L, P=P, tq=tq, tqr=tqr, feat3d=feat3d)
    yc = _diff(fm(cqT), r3(cg), wl["lams"], wl["subrow"], past_c, (r3(ck), r3(cv)),
               B=B, L=L, P=P, tq=tq, tqr=tqr, feat3d=feat3d, lam_init=lam_init)

    x_new = _outproj(ya.reshape(T, -1), yb.reshape(T, hw), yc.reshape(T, hw), x, wl["w_out"], wl["norm_post"],
                     tm=tm)
    states = (nbuf, hlast[:, 0, :],
              bk.reshape(B, L, B_HEADS, B_HEAD_DIM), bv.reshape(B, L, B_HEADS, B_HEAD_DIM), r3(bik),
              ck.reshape(B, L, C_HEADS, C_HEAD_DIM), cv.reshape(B, L, C_HEADS, C_HEAD_DIM))
    return x_new, states


def kernel(x_prompt, x_sample, cache_a_conv, state_a_h, cache_b_k, cache_b_v, cache_b_kidx, cache_c_k, cache_c_v, norm_pre, norm_post, w_in, w_out, a_conv_w, a_conv_b, a_rg_w, a_rg_b, a_in_w, a_in_b, a_lambda, c_lam_q1, c_lam_k1, c_lam_q2, c_lam_k2, c_subln):
    Bp, Lp, D = x_prompt.shape
    Bs, Ls, _ = x_sample.shape
    depth = w_in.shape[0]
    P = cache_b_k.shape[2]
    aw = D // 2

    wr, wf = _split_w_in(w_in, D)
    rw = _block_diag(a_rg_w).astype(BF16)
    iw = _block_diag(a_in_w).astype(BF16)
    wo = w_out.astype(BF16)

    xp = x_prompt.reshape(Bp * Lp, D)
    xs = x_sample.reshape(Bs * Ls, D)
    zero_buf = jnp.zeros((Bp, CONV_W - 1, aw), F32)
    zero_h = jnp.zeros((Bp, aw), F32)
    p_st, s_st = [], []
    for l in range(depth):
        row = lambda a: a[l][None, :]
        wl = {"norm_pre": row(norm_pre), "norm_post": row(norm_post), "wr": wr[l], "wf": wf[l], "w_out": wo[l],
              "a_conv_w": a_conv_w[l], "a_conv_b": row(a_conv_b), "rw": rw[l], "a_rg_b": row(a_rg_b),
              "iw": iw[l], "a_in_b": row(a_in_b), "a_lambda": row(a_lambda),
              "lams": (row(c_lam_q1), row(c_lam_k1), row(c_lam_q2), row(c_lam_k2)),
              "subrow": jnp.tile(c_subln[l], C_HEADS)[None, :]}
        lam_init = 0.8 - 0.6 * math.exp(-0.3 * l)
        xp, st_p = _layer(xp, Bp, Lp, 0, wl, zero_buf, zero_h, None, lam_init)
        xs, st_s = _layer(xs, Bs, Ls, P, wl, cache_a_conv[l], state_a_h[l],
                          (cache_b_k[l], cache_b_v[l], cache_b_kidx[l], cache_c_k[l], cache_c_v[l]), lam_init)
        p_st.append(st_p)
        s_st.append(st_s)

    stk = lambda states, i: jnp.stack([s[i] for s in states], axis=0)
    return (xp.reshape(Bp, Lp, D), xs.reshape(Bs, Ls, D),
            *[stk(p_st, i) for i in range(7)], *[stk(s_st, i) for i in range(7)])
```

```python
import functools
import math

import jax
import jax.numpy as jnp
from jax import lax
from jax.experimental import pallas as pl
from jax.experimental.pallas import tpu as pltpu

F32 = jnp.float32
BF16 = jnp.bfloat16
I32 = jnp.int32

CHUNK = 64
CONV_W = 4
LRU_C = 8.0
B_HEADS = 4
B_HEAD_DIM = 64
IDX_HEADS = 8
IDX_DIM = 32
TOPK_MAX = 256
C_HEADS = 4
C_HALF = 32
C_HEAD_DIM = 2 * C_HALF
EPS = 1e-6
NEG = -1e30
INT_MIN = -2 ** 31
VMEM_LIMIT_BYTES = 56 * 1024 * 1024
KEY_TILE = 256


def _cparams(sem):
    return pltpu.CompilerParams(dimension_semantics=sem, vmem_limit_bytes=VMEM_LIMIT_BYTES)


def _silu(g):
    return g * jax.nn.sigmoid(g)


def _inproj_kernel(x_ref, g_ref, wr_ref, wf_ref, *out_refs, row_plan, feat_plan):
    x = x_ref[...]
    ms = jnp.mean(x * x, axis=-1, keepdims=True)
    xn = (x * lax.rsqrt(ms + EPS) * g_ref[...]).astype(BF16)
    proj = jnp.dot(xn, wr_ref[...], preferred_element_type=F32)
    projT = lax.dot_general(wf_ref[...], xn, (((1,), (1,)), ((), ())),
                            preferred_element_type=F32)
    refs = iter(out_refs)
    o = 0
    for n, dt in row_plan:
        ref = next(refs)
        ref[...] = proj[:, o:o + n].astype(dt)
        o += n
    o = 0
    for n, dt in feat_plan:
        ref = next(refs)
        ref[...] = projT[o:o + n, :].astype(dt)
        o += n


def _inproj(x2d, g, wr, wf, *, tm, row_plan, feat_plan, feat_batch):
    T, D = x2d.shape
    row = lambda n: pl.BlockSpec((tm, n), lambda i: (i, 0))
    const = lambda s: pl.BlockSpec(s, lambda i: (0, 0))
    out_shape = [jax.ShapeDtypeStruct((T, n), dt) for n, dt in row_plan]
    out_specs = [row(n) for n, _ in row_plan]
    if feat_batch is None:
        out_shape += [jax.ShapeDtypeStruct((n, T), dt) for n, dt in feat_plan]
        out_specs += [pl.BlockSpec((n, tm), lambda i: (0, i)) for n, _ in feat_plan]
    else:
        B, L = feat_batch
        npb = L // tm
        out_shape += [jax.ShapeDtypeStruct((B, n, L), dt) for n, dt in feat_plan]
        out_specs += [pl.BlockSpec((None, n, tm), lambda i: (i // npb, 0, i % npb)) for n, _ in feat_plan]
    return pl.pallas_call(
        functools.partial(_inproj_kernel, row_plan=row_plan, feat_plan=feat_plan),
        grid=(T // tm,),
        in_specs=[row(D), const((1, D)), const(wr.shape), const(wf.shape)],
        out_specs=out_specs, out_shape=out_shape,
        compiler_params=_cparams(("parallel",)),
    )(x2d, g, wr, wf)


def _amix_kernel(ax_ref, ag_ref, buf_ref, h0_ref, cw_ref, cb_ref, rw_ref, rb_ref, iw_ref, ib_ref, lam_ref,
                 ya_ref, nbuf_ref, hlast_ref, xbuf_ref, hc_ref, *, tl, nl, aw):
    li = pl.program_id(1)

    @pl.when(li == 0)
    def _():
        xbuf_ref[0:8, :] = jnp.zeros((8, aw), F32)
        xbuf_ref[8 - (CONV_W - 1):8, :] = buf_ref[...]
        hc_ref[...] = h0_ref[...]

    x = ax_ref[...]
    xbuf_ref[8:8 + tl, :] = x
    cw = cw_ref[...]
    conv = cb_ref[...] + x * cw[CONV_W - 1:CONV_W, :]
    for s in range(1, CONV_W):
        conv = conv + xbuf_ref[8 - s:8 - s + tl, :] * cw[CONV_W - 1 - s:CONV_W - s, :]

    cbf = conv.astype(BF16)
    r = jax.nn.sigmoid(jnp.dot(cbf, rw_ref[...], preferred_element_type=F32) + rb_ref[...])
    ig = jax.nn.sigmoid(jnp.dot(cbf, iw_ref[...], preferred_element_type=F32) + ib_ref[...])
    nl_lam = -lam_ref[...]
    sp = jnp.maximum(nl_lam, 0.0) + jnp.log1p(jnp.exp(-jnp.abs(nl_lam)))
    log_a = (-LRU_C) * r * sp
    a = jnp.exp(log_a)
    u = jnp.sqrt(1.0 - a * a) * (ig * conv)

    row = lax.broadcasted_iota(I32, (tl, aw), 0)
    s = 1
    while s < tl:
        a_sh = pltpu.roll(a, s, 0)
        u_sh = pltpu.roll(u, s, 0)
        keep = row >= s
        u = jnp.where(keep, a * u_sh + u, u)
        a = jnp.where(keep, a * a_sh, a)
        s *= 2
    h = a * hc_ref[...] + u
    ya_ref[...] = h * _silu(ag_ref[...])
    hc_ref[...] = h[tl - 1:tl, :]
    xbuf_ref[0:8, :] = x[tl - 8:tl, :]

    @pl.when(li == nl - 1)
    def _():
        nbuf_ref[...] = xbuf_ref[8 + tl - (CONV_W - 1):8 + tl, :]
        hlast_ref[...] = h[tl - 1:tl, :]


def _amix(ax, ag, buf, h0, cw, cb, rw, rb, iw, ib, lam, *, tl):
    B, L, aw = ax.shape
    nl = L // tl
    seq = pl.BlockSpec((None, tl, aw), lambda b, l: (b, l, 0))
    perb = lambda r: pl.BlockSpec((None, r, aw), lambda b, l: (b, 0, 0))
    const = lambda s: pl.BlockSpec(s, lambda b, l: (0, 0))
    return pl.pallas_call(
        functools.partial(_amix_kernel, tl=tl, nl=nl, aw=aw),
        grid=(B, nl),
        in_specs=[seq, seq, perb(CONV_W - 1), perb(1), const((CONV_W, aw)), const((1, aw)),
                  const((aw, aw)), const((1, aw)), const((aw, aw)), const((1, aw)), const((1, aw))],
        out_specs=[seq, perb(CONV_W - 1), perb(1)],
        out_shape=[jax.ShapeDtypeStruct((B, L, aw), F32), jax.ShapeDtypeStruct((B, CONV_W - 1, aw), F32),
                   jax.ShapeDtypeStruct((B, 1, aw), F32)],
        scratch_shapes=[pltpu.VMEM((tl + 8, aw), F32), pltpu.VMEM((1, aw), F32)],
        compiler_params=_cparams(("parallel", "arbitrary")),
    )(ax, ag, buf, h0, cw, cb, rw, rb, iw, ib, lam)


def _col_reduce(x, op):
    rows, n = x.shape
    return op(op(x.reshape(rows // 8, 8, n), axis=0), axis=0, keepdims=True)


def _diag_valid(rows, tq):
    kc = lax.broadcasted_iota(I32, (rows, tq), 0) // CHUNK
    qc = lax.broadcasted_iota(I32, (rows, tq), 1) // CHUNK
    return kc <= qc


def _softmax_step(s, sel, vt_h, m_ref, l_ref, acc_ref, idx):
    m_old = m_ref[idx]
    sm = s if sel is None else jnp.where(sel, s, NEG)
    m_new = jnp.maximum(m_old, _col_reduce(sm, jnp.max))
    alpha = jnp.exp(m_old - m_new)
    p = jnp.exp(sm - m_new)
    if sel is not None:
        p = jnp.where(sel, p, 0.0)
    l_ref[idx] = alpha * l_ref[idx] + _col_reduce(p, jnp.sum)
    pv = jnp.dot(vt_h, p.astype(BF16), preferred_element_type=F32)
    acc_ref[idx, :, :] = alpha * acc_ref[idx, :, :] + pv
    m_ref[idx] = m_new


def _fill_block_diag(dst_ref, qT_ref, nb, tqr, nblk, blk):
    tq = nb * tqr
    dst_ref[...] = jnp.zeros(dst_ref.shape, dst_ref.dtype)
    for a in range(nb):
        for m in range(nblk):
            dst_ref[a, blk * m:blk * (m + 1), m * tq + a * tqr:m * tq + (a + 1) * tqr] = \
                qT_ref[blk * m:blk * (m + 1), a * tqr:(a + 1) * tqr]


def _own_lanes(full, nb, dh, tqr):
    if nb == 1:
        return full
    lane_seq = lax.broadcasted_iota(I32, (dh, nb * tqr), 1) // tqr
    out = full[0:dh, :]
    for a in range(1, nb):
        out = jnp.where(lane_seq == a, full[a * dh:(a + 1) * dh, :], out)
    return out


class _Keys:
    def __init__(self, P, nb, tkn, new_fm, past_refs, new_refs):
        self.P, self.nb, self.tkn, self.new_fm = P, nb, tkn, new_fm
        self.past_refs, self.new_refs = past_refs, new_refs

    def past(self, which, a, r0):
        t = self.past_refs[which][a, :, pl.ds(r0, KEY_TILE)]
        return t if which == 1 else t.T

    def new(self, which, a, r0):
        ref = self.new_refs[which]
        if self.new_fm:
            return ref[:, pl.ds(r0, self.tkn)] if which == 1 else ref[pl.ds(r0, self.tkn), :]
        t = ref[a]
        return t.T if which == 1 else t


def _dsa_kernel(*refs, P, nb, tqr, tkn, topk, idx_bits, new_fm):
    tq = nb * tqr
    tkp = KEY_TILE
    it = iter(refs)
    qT_ref, qiT_ref, wT_ref, g_ref = next(it), next(it), next(it), next(it)
    past_refs = (next(it), next(it), next(it)) if P > 0 else None
    new_refs = (next(it), next(it), next(it))
    y_ref = next(it)
    keys_ref, qip_ref, qbd_ref, m_ref, l_ref, acc_ref = (next(it) for _ in range(6))
    kv = _Keys(P, nb, tkn, new_fm, past_refs, new_refs)
    j = pl.program_id(1)
    nnew = j + 1
    seqs = range(nb)

    _fill_block_diag(qip_ref, qiT_ref, nb, tqr, 1, IDX_HEADS * IDX_DIM)
    w = wT_ref[...]

    def idx_keys(ki_tiles, valid):
        kbs = [t.astype(BF16) for t in ki_tiles]
        acc = None
        for h in range(IDX_HEADS):
            s = None
            for a in seqs:
                d = jnp.dot(kbs[a], qip_ref[a, IDX_DIM * h:IDX_DIM * (h + 1), :], preferred_element_type=F32)
                s = d if s is None else s + d
            t = jnp.maximum(s, 0.0) * w[h:h + 1, :]
            acc = t if acc is None else acc + t
        acc = jnp.where(acc == 0.0, 0.0, acc)
        bits = lax.bitcast_convert_type(acc, I32)
        key = bits ^ ((bits >> 31) & 0x7FFFFFFF)
        return key if valid is None else jnp.where(valid, key, INT_MIN)

    if P > 0:
        def past_keys(t, c):
            r0 = pl.multiple_of(t * tkp, tkp)
            keys_ref[pl.ds(r0, tkp), :] = idx_keys([kv.past(2, a, r0) for a in seqs], None)
            return c
        lax.fori_loop(0, P // tkp, past_keys, 0)

    def new_keys(i, c):
        r0 = pl.multiple_of(i * tkn, tkn)
        keys_ref[pl.ds(P + r0, tkn), :] = idx_keys([kv.new(2, a, r0) for a in seqs], None)
        return c
    lax.fori_loop(0, j, new_keys, 0)
    rd = pl.multiple_of(j * tkn, tkn)
    keys_ref[pl.ds(P + rd, tkn), :] = idx_keys([kv.new(2, a, rd) for a in seqs],
                                               _diag_valid(tkn, tq) if tkn > CHUNK else None)

    def count(pred):
        tot = jnp.zeros((8, tq), I32)
        if P > 0:
            def pb(t, c):
                r0 = pl.multiple_of(t * tkp, tkp)
                ind = pred(keys_ref[pl.ds(r0, tkp), :], r0)
                return c + jnp.sum(ind.reshape(tkp // 8, 8, tq), axis=0)
            tot = lax.fori_loop(0, P // tkp, pb, tot)

        def nb_(i, c):
            r0 = pl.multiple_of(i * tkn, tkn)
            ind = pred(keys_ref[pl.ds(P + r0, tkn), :], P + r0)
            return c + jnp.sum(ind.reshape(tkn // 8, 8, tq), axis=0)
        tot = lax.fori_loop(0, nnew, nb_, tot)
        return jnp.sum(tot, axis=0, keepdims=True)

    def bisect(step, ans):
        cand = ans + jnp.left_shift(jnp.int32(1), 31 - step)
        cnt = count(lambda c, r0: jnp.where(c >= cand, 1, 0))
        return jnp.where(cnt >= topk, cand, ans)
    ans = lax.fori_loop(0, 32, bisect, jnp.full((1, tq), INT_MIN, I32))

    n_gt = count(lambda c, r0: jnp.where(c > ans, 1, 0))
    n_ge = count(lambda c, r0: jnp.where(c >= ans, 1, 0))
    need = (n_ge > topk) & (ans != INT_MIN)
    take = topk - n_gt

    @pl.when(jnp.max(jnp.where(need, 1, 0)) > 0)
    def _():
        def rows_of(c, r0):
            return lax.broadcasted_iota(I32, c.shape, 0) + r0

        def bis_row(step, x):
            t = x + jnp.left_shift(jnp.int32(1), idx_bits - 1 - step)
            cnt = count(lambda c, r0: jnp.where((c == ans) & (rows_of(c, r0) < t), 1, 0))
            return jnp.where(cnt < take, t, x)
        cut = lax.fori_loop(0, idx_bits, bis_row, jnp.zeros((1, tq), I32))
        cut = jnp.where(need, cut, 2 ** 30)

        def demote(c, r0):
            return jnp.where((c == ans) & (rows_of(c, r0) > cut), c - 1, c)
        if P > 0:
            def pd(t, z):
                r0 = pl.multiple_of(t * tkp, tkp)
                keys_ref[pl.ds(r0, tkp), :] = demote(keys_ref[pl.ds(r0, tkp), :], r0)
                return z
            lax.fori_loop(0, P // tkp, pd, 0)

        def nd(i, z):
            r0 = pl.multiple_of(i * tkn, tkn)
            keys_ref[pl.ds(P + r0, tkn), :] = demote(keys_ref[pl.ds(P + r0, tkn), :], P + r0)
            return z
        lax.fori_loop(0, nnew, nd, 0)

    thr = jnp.maximum(ans, INT_MIN + 1)

    _fill_block_diag(qbd_ref, qT_ref, nb, tqr, B_HEADS, B_HEAD_DIM)
    m_ref[...] = jnp.full(m_ref.shape, NEG, F32)
    l_ref[...] = jnp.zeros(l_ref.shape, F32)
    acc_ref[...] = jnp.zeros(acc_ref.shape, F32)

    def attend(k_tiles, vt_tiles, key_tile):
        sel = key_tile >= thr
        s_all = None
        for a in seqs:
            d = jnp.dot(k_tiles[a].astype(BF16), qbd_ref[a], preferred_element_type=F32)
            s_all = d if s_all is None else s_all + d
        vts = [t.astype(BF16) for t in vt_tiles]
        for h in range(B_HEADS):
            vt_h = jnp.concatenate([v[B_HEAD_DIM * h:B_HEAD_DIM * (h + 1), :] for v in vts], axis=0)
            _softmax_step(s_all[:, h * tq:(h + 1) * tq], sel, vt_h, m_ref, l_ref, acc_ref, h)

    if P > 0:
        def pa(t, z):
            r0 = pl.multiple_of(t * tkp, tkp)
            attend([kv.past(0, a, r0) for a in seqs], [kv.past(1, a, r0) for a in seqs],
                   keys_ref[pl.ds(r0, tkp), :])
            return z
        lax.fori_loop(0, P // tkp, pa, 0)

    def na(i, z):
        r0 = pl.multiple_of(i * tkn, tkn)
        attend([kv.new(0, a, r0) for a in seqs], [kv.new(1, a, r0) for a in seqs],
               keys_ref[pl.ds(P + r0, tkn), :])
        return z
    lax.fori_loop(0, nnew, na, 0)

    oT = jnp.concatenate([_own_lanes(acc_ref[h, :, :] / l_ref[h], nb, B_HEAD_DIM, tqr)
                          for h in range(B_HEADS)], axis=0)
    y = oT.T * _silu(g_ref[...].reshape(tq, -1))
    y_ref[...] = y.reshape(y_ref.shape)


def _attn_specs(B, L, P, nb, tqr, hw, layer, new_fm, feat_rows):
    tq = nb * tqr
    if new_fm:
        feat = lambda n: pl.BlockSpec((None, n, tq), lambda b, j: (b, 0, j))
        seqblk = pl.BlockSpec((None, tqr, hw), lambda b, j: (b, j, 0))
        new_row = lambda c: pl.BlockSpec((None, L, c), lambda b, j: (b, 0, 0))
        new_col = lambda r: pl.BlockSpec((None, r, L), lambda b, j: (b, 0, 0))
    else:
        feat = lambda n: pl.BlockSpec((n, tq), lambda b, j: (0, b))
        seqblk = pl.BlockSpec((nb, tqr, hw), lambda b, j: (b, 0, 0))
        new_row = lambda c: pl.BlockSpec((nb, L, c), lambda b, j: (b, 0, 0))
        new_col = None
    past = lambda r: pl.BlockSpec((None, nb, r, P), lambda b, j: (layer, b, 0, 0))
    return feat, seqblk, new_row, new_col, past


def _dsa(qT, qiT, wT, gate, past, new, *, B, L, P, nb, tqr, layer, new_fm):
    hw = gate.shape[-1]
    tq = nb * tqr
    Lk = P + L
    topk = min(TOPK_MAX, Lk // 4)
    idx_bits = max(1, math.ceil(math.log2(Lk + 1)))
    feat, seqblk, new_row, new_col, pastspec = _attn_specs(B, L, P, nb, tqr, hw, layer, new_fm, None)
    in_specs = [feat(hw), feat(hw), feat(IDX_HEADS), seqblk]
    args = [qT, qiT, wT, gate]
    if P > 0:
        in_specs += [pastspec(hw), pastspec(hw), pastspec(IDX_DIM)]
        args += list(past)
    in_specs += [new_row(hw), new_col(hw) if new_fm else new_row(hw), new_row(IDX_DIM)]
    args += list(new)
    return pl.pallas_call(
        functools.partial(_dsa_kernel, P=P, nb=nb, tqr=tqr, tkn=tqr, topk=topk, idx_bits=idx_bits,
                          new_fm=new_fm),
        grid=(B // nb, L // tqr),
        in_specs=in_specs,
        out_specs=seqblk,
        out_shape=jax.ShapeDtypeStruct((B, L, hw), F32),
        scratch_shapes=[pltpu.VMEM((Lk, tq), I32),
                        pltpu.VMEM((nb, IDX_HEADS * IDX_DIM, tq), BF16),
                        pltpu.VMEM((nb, hw, B_HEADS * tq), BF16),
                        pltpu.VMEM((B_HEADS, 1, tq), F32), pltpu.VMEM((B_HEADS, 1, tq), F32),
                        pltpu.VMEM((B_HEADS, nb * B_HEAD_DIM, tq), F32)],
        compiler_params=_cparams(("parallel", "arbitrary")),
    )(*args)


def _diff_kernel(*refs, P, nb, tqr, tkn, lam_init, new_fm):
    tq = nb * tqr
    tkp = KEY_TILE
    it = iter(refs)
    qT_ref, g_ref, lq1_ref, lk1_ref, lq2_ref, lk2_ref, sub_ref = (next(it) for _ in range(7))
    past_refs = (next(it), next(it)) if P > 0 else None
    new_refs = (next(it), next(it))
    y_ref = next(it)
    qbd_ref, m_ref, l_ref, acc_ref = (next(it) for _ in range(4))
    kv = _Keys(P, nb, tkn, new_fm, past_refs, new_refs)
    j = pl.program_id(1)
    nmaps = 2 * C_HEADS
    seqs = range(nb)

    _fill_block_diag(qbd_ref, qT_ref, nb, tqr, nmaps, C_HALF)
    m_ref[...] = jnp.full(m_ref.shape, NEG, F32)
    l_ref[...] = jnp.zeros(l_ref.shape, F32)
    acc_ref[...] = jnp.zeros(acc_ref.shape, F32)

    def attend(k_tiles, vt_tiles, valid):
        s_all = None
        for a in seqs:
            d = jnp.dot(k_tiles[a].astype(BF16), qbd_ref[a], preferred_element_type=F32)
            s_all = d if s_all is None else s_all + d
        vts = [t.astype(BF16) for t in vt_tiles]
        for mi in range(nmaps):
            h = mi // 2
            s = s_all[:, mi * tq:(mi + 1) * tq]
            if valid is not None:
                s = jnp.where(valid, s, NEG)
            vt_h = jnp.concatenate([v[C_HEAD_DIM * h:C_HEAD_DIM * (h + 1), :] for v in vts], axis=0)
            _softmax_step(s, None, vt_h, m_ref, l_ref, acc_ref, mi)

    if P > 0:
        def pa(t, z):
            r0 = pl.multiple_of(t * tkp, tkp)
            attend([kv.past(0, a, r0) for a in seqs], [kv.past(1, a, r0) for a in seqs], None)
            return z
        lax.fori_loop(0, P // tkp, pa, 0)

    def na(i, z):
        r0 = pl.multiple_of(i * tkn, tkn)
        attend([kv.new(0, a, r0) for a in seqs], [kv.new(1, a, r0) for a in seqs], None)
        return z
    lax.fori_loop(0, j, na, 0)
    rd = pl.multiple_of(j * tkn, tkn)
    attend([kv.new(0, a, rd) for a in seqs], [kv.new(1, a, rd) for a in seqs],
           _diag_valid(tkn, tq) if tkn > CHUNK else None)

    lam = (jnp.exp(jnp.sum(lq1_ref[...] * lk1_ref[...], axis=-1, keepdims=True))
           - jnp.exp(jnp.sum(lq2_ref[...] * lk2_ref[...], axis=-1, keepdims=True)) + lam_init)
    outs = []
    for h in range(C_HEADS):
        o = acc_ref[2 * h, :, :] / l_ref[2 * h] - lam * (acc_ref[2 * h + 1, :, :] / l_ref[2 * h + 1])
        o = _own_lanes(o, nb, C_HEAD_DIM, tqr)
        ms = jnp.mean(o * o, axis=0, keepdims=True)
        outs.append(o * lax.rsqrt(ms + EPS))
    o = jnp.concatenate(outs, axis=0).T
    y = (o * sub_ref[...]) * (1.0 - lam_init) * _silu(g_ref[...].reshape(tq, -1))
    y_ref[...] = y.reshape(y_ref.shape)


def _diff(qT, gate, lams, subrow, past, new, *, B, L, P, nb, tqr, layer, new_fm, lam_init):
    hw = gate.shape[-1]
    tq = nb * tqr
    nmaps = 2 * C_HEADS
    feat, seqblk, new_row, new_col, pastspec = _attn_specs(B, L, P, nb, tqr, hw, layer, new_fm, None)
    const = lambda s: pl.BlockSpec(s, lambda b, j: (0, 0))
    in_specs = [feat(hw), seqblk] + [const((1, C_HALF))] * 4 + [const((1, hw))]
    args = [qT, gate, *lams, subrow]
    if P > 0:
        in_specs += [pastspec(hw), pastspec(hw)]
        args += list(past)
    in_specs += [new_row(hw), new_col(hw) if new_fm else new_row(hw)]
    args += list(new)
    return pl.pallas_call(
        functools.partial(_diff_kernel, P=P, nb=nb, tqr=tqr, tkn=tqr, lam_init=lam_init, new_fm=new_fm),
        grid=(B // nb, L // tqr),
        in_specs=in_specs,
        out_specs=seqblk,
        out_shape=jax.ShapeDtypeStruct((B, L, hw), F32),
        scratch_shapes=[pltpu.VMEM((nb, hw, nmaps * tq), BF16),
                        pltpu.VMEM((nmaps, 1, tq), F32), pltpu.VMEM((nmaps, 1, tq), F32),
                        pltpu.VMEM((nmaps, nb * C_HEAD_DIM, tq), F32)],
        compiler_params=_cparams(("parallel", "arbitrary")),
    )(*args)


def _outproj_kernel(ya_ref, yb_ref, yc_ref, x_ref, w_ref, g_ref, o_ref, *, aw, hw):
    mix = jnp.dot(ya_ref[...].astype(BF16), w_ref[0:aw, :], preferred_element_type=F32)
    mix = mix + jnp.dot(yb_ref[...].astype(BF16), w_ref[aw:aw + hw, :], preferred_element_type=F32)
    mix = mix + jnp.dot(yc_ref[...].astype(BF16), w_ref[aw + hw:aw + 2 * hw, :], preferred_element_type=F32)
    ms = jnp.mean(mix * mix, axis=-1, keepdims=True)
    o_ref[...] = x_ref[...] + mix * lax.rsqrt(ms + EPS) * g_ref[...]


def _outproj(ya, yb, yc, x2d, w, g, *, tm):
    T, D = x2d.shape
    aw, hw = D // 2, D // 4
    row = lambda n: pl.BlockSpec((tm, n), lambda i: (i, 0))
    const = lambda s: pl.BlockSpec(s, lambda i: (0, 0))
    return pl.pallas_call(
        functools.partial(_outproj_kernel, aw=aw, hw=hw),
        grid=(T // tm,),
        in_specs=[row(aw), row(hw), row(hw), row(D), const((D, D)), const((1, D))],
        out_specs=row(D),
        out_shape=jax.ShapeDtypeStruct((T, D), F32),
        compiler_params=_cparams(("parallel",)),
    )(ya, yb, yc, x2d, w, g)


_ROW_PROMPT = ("a_x", "a_g", "b_g", "c_g", "b_k", "c_k", "b_ik")
_FEAT_PROMPT = ("b_q", "b_iq", "c_q", "b_k", "b_v", "c_k", "c_v", "b_ik", "b_iw")
_ROW_SAMPLE = ("a_x", "a_g", "b_g", "c_g", "b_k", "b_v", "c_k", "c_v", "b_ik")
_FEAT_SAMPLE = ("b_q", "b_iq", "c_q", "b_ik", "b_iw")
_BF16_OUT = {"b_q", "b_iq", "c_q"}


def _pack_w_in(w_in, D, rows, feats, row_bf16):
    aw, hw = D // 2, D // 4
    sizes = (aw, aw, hw, hw, hw, hw, IDX_HEADS * IDX_DIM, IDX_DIM, IDX_HEADS, hw, hw, hw, hw)
    names = ("a_x", "a_g", "b_q", "b_k", "b_v", "b_g", "b_iq", "b_ik", "b_iw", "c_q", "c_k", "c_v", "c_g")
    scale = {"b_q": B_HEAD_DIM ** -0.5, "c_q": C_HALF ** -0.5,
             "b_iw": (IDX_DIM ** -0.5) * (IDX_HEADS ** -0.5)}
    seg, o = {}, 0
    for n, s in zip(names, sizes):
        seg[n] = w_in[..., o:o + s] * scale[n] if n in scale else w_in[..., o:o + s]
        o += s
    depth = w_in.shape[0]

    def cat(parts, mult):
        w = jnp.concatenate([seg[n] for n in parts], axis=-1)
        pad = (-w.shape[-1]) % mult
        return jnp.concatenate([w, jnp.zeros((depth, D, pad), w.dtype)], axis=-1) if pad else w
    wr = cat(rows, 128).astype(BF16)
    wf = jnp.swapaxes(cat(feats, 16), 1, 2).astype(BF16)
    row_plan = tuple((seg[n].shape[-1], BF16 if n in row_bf16 else F32) for n in rows)
    feat_plan = tuple((seg[n].shape[-1], BF16 if n in _BF16_OUT else F32) for n in feats)
    return wr, wf, row_plan, feat_plan


def _block_diag(w):
    depth, nb, n, _ = w.shape
    eye = jnp.eye(nb, dtype=w.dtype)
    return jnp.einsum("lnde,nm->lndme", w, eye).reshape(depth, nb * n, nb * n)


def _prompt_layer(x, B, L, wl, zero_buf, zero_h, lam_init):
    T, D = x.shape
    hw = D // 4
    tm = min(256, L)
    out = _inproj(x, wl["norm_pre"], wl["wr_p"], wl["wf_p"], tm=tm, row_plan=wl["plan_p"][0],
                  feat_plan=wl["plan_p"][1], feat_batch=(B, L))
    ax, ag, bg, cg, bk_bf, ck_bf, bik = out[:7]
    bqT, biqT, cqT, bkT, bvT, ckT, cvT, bikT, biwT = out[7:]
    r3 = lambda a: a.reshape(B, L, a.shape[-1])
    ya, nbuf, hlast = _amix(r3(ax), r3(ag), zero_buf, zero_h[:, None, :], wl["a_conv_w"], wl["a_conv_b"],
                            wl["rw"], wl["a_rg_b"], wl["iw"], wl["a_in_b"], wl["a_lambda"], tl=min(L, 512))
    tqr = min(256, L)
    kw = dict(B=B, L=L, P=0, nb=1, tqr=tqr, layer=0, new_fm=True)
    yb = _dsa(bqT, biqT, biwT, r3(bg), None, (r3(bk_bf), bvT, r3(bik)), **kw)
    yc = _diff(cqT, r3(cg), wl["lams"], wl["subrow"], None, (r3(ck_bf), cvT), lam_init=lam_init, **kw)
    x_new = _outproj(ya.reshape(T, -1), yb.reshape(T, hw), yc.reshape(T, hw), x, wl["w_out"], wl["norm_post"],
                     tm=tm)
    heads = lambda t, nh: t.reshape(B, nh, t.shape[1] // nh, L).transpose(0, 3, 1, 2)
    states = (nbuf, hlast[:, 0, :], heads(bkT, B_HEADS), heads(bvT, B_HEADS), bikT.transpose(0, 2, 1),
              heads(ckT, C_HEADS), heads(cvT, C_HEADS))
    return x_new, states


def _sample_layer(x, B, L, P, layer, wl, conv_buf, h0, past, lam_init):
    T, D = x.shape
    hw = D // 4
    tm = min(256, T)
    out = _inproj(x, wl["norm_pre"], wl["wr_s"], wl["wf_s"], tm=tm, row_plan=wl["plan_s"][0],
                  feat_plan=wl["plan_s"][1], feat_batch=None)
    ax, ag, bg, cg, bk, bv, ck, cv, bik = out[:9]
    bqT, biqT, cqT, bikT, biwT = out[9:]
    r3 = lambda a: a.reshape(B, L, a.shape[-1])
    ya, nbuf, hlast = _amix(r3(ax), r3(ag), conv_buf, h0[:, None, :], wl["a_conv_w"], wl["a_conv_b"],
                            wl["rw"], wl["a_rg_b"], wl["iw"], wl["a_in_b"], wl["a_lambda"], tl=L)
    nb = 2 if B % 2 == 0 else 1
    kw = dict(B=B, L=L, P=P, nb=nb, tqr=L, layer=layer, new_fm=False)
    pkT, pvT, pkiT, pckT, pcvT = past
    yb = _dsa(bqT, biqT, biwT, r3(bg), (pkT, pvT, pkiT), (r3(bk), r3(bv), r3(bik)), **kw)
    yc = _diff(cqT, r3(cg), wl["lams"], wl["subrow"], (pckT, pcvT), (r3(ck), r3(cv)), lam_init=lam_init, **kw)
    x_new = _outproj(ya.reshape(T, -1), yb.reshape(T, hw), yc.reshape(T, hw), x, wl["w_out"], wl["norm_post"],
                     tm=tm)
    states = (nbuf, hlast[:, 0, :],
              bk.reshape(B, L, B_HEADS, B_HEAD_DIM), bv.reshape(B, L, B_HEADS, B_HEAD_DIM),
              bikT.reshape(IDX_DIM, B, L).transpose(1, 2, 0),
              ck.reshape(B, L, C_HEADS, C_HEAD_DIM), cv.reshape(B, L, C_HEADS, C_HEAD_DIM))
    return x_new, states


def kernel(x_prompt, x_sample, cache_a_conv, state_a_h, cache_b_k, cache_b_v, cache_b_kidx, cache_c_k, cache_c_v, norm_pre, norm_post, w_in, w_out, a_conv_w, a_conv_b, a_rg_w, a_rg_b, a_in_w, a_in_b, a_lambda, c_lam_q1, c_lam_k1, c_lam_q2, c_lam_k2, c_subln):
    Bp, Lp, D = x_prompt.shape
    Bs, Ls, _ = x_sample.shape
    depth = w_in.shape[0]
    P = cache_b_k.shape[2]
    aw = D // 2

    wr_p, wf_p, *plan_p = _pack_w_in(w_in, D, _ROW_PROMPT, _FEAT_PROMPT, {"b_k", "c_k"})
    wr_s, wf_s, *plan_s = _pack_w_in(w_in, D, _ROW_SAMPLE, _FEAT_SAMPLE, set())
    rw = _block_diag(a_rg_w).astype(BF16)
    iw = _block_diag(a_in_w).astype(BF16)
    wo = w_out.astype(BF16)

    fm = lambda c: jnp.transpose(c, (0, 1, 3, 4, 2)).reshape(depth, Bs, -1, P)
    past = (fm(cache_b_k), fm(cache_b_v), jnp.transpose(cache_b_kidx, (0, 1, 3, 2)), fm(cache_c_k), fm(cache_c_v))

    xp = x_prompt.reshape(Bp * Lp, D)
    xs = x_sample.reshape(Bs * Ls, D)
    zero_buf = jnp.zeros((Bp, CONV_W - 1, aw), F32)
    zero_h = jnp.zeros((Bp, aw), F32)
    p_st, s_st = [], []
    for l in range(depth):
        row = lambda a: a[l][None, :]
        wl = {"norm_pre": row(norm_pre), "norm_post": row(norm_post), "w_out": wo[l],
              "wr_p": wr_p[l], "wf_p": wf_p[l], "plan_p": plan_p, "wr_s": wr_s[l], "wf_s": wf_s[l], "plan_s": plan_s,
              "a_conv_w": a_conv_w[l], "a_conv_b": row(a_conv_b), "rw": rw[l], "a_rg_b": row(a_rg_b),
              "iw": iw[l], "a_in_b": row(a_in_b), "a_lambda": row(a_lambda),
              "lams": (row(c_lam_q1), row(c_lam_k1), row(c_lam_q2), row(c_lam_k2)),
              "subrow": jnp.tile(c_subln[l], C_HEADS)[None, :]}
        lam_init = 0.8 - 0.6 * math.exp(-0.3 * l)
        xp, st_p = _prompt_layer(xp, Bp, Lp, wl, zero_buf, zero_h, lam_init)
        xs, st_s = _sample_layer(xs, Bs, Ls, P, l, wl, cache_a_conv[l], state_a_h[l], past, lam_init)
        p_st.append(st_p)
        s_st.append(st_s)

    stk = lambda states, i: jnp.stack([s[i] for s in states], axis=0)
    return (xp.reshape(Bp, Lp, D), xs.reshape(Bs, Ls, D),
            *[stk(p_st, i) for i in range(7)], *[stk(s_st, i) for i in range(7)])
```

```python
import functools
import math

import jax
import jax.numpy as jnp
from jax import lax
from jax.experimental import pallas as pl
from jax.experimental.pallas import tpu as pltpu

F32 = jnp.float32
BF16 = jnp.bfloat16
I32 = jnp.int32

CHUNK = 64
CONV_W = 4
LRU_C = 8.0
B_HEADS = 4
B_HEAD_DIM = 64
IDX_HEADS = 8
IDX_DIM = 32
TOPK_MAX = 256
C_HEADS = 4
C_HALF = 32
C_HEAD_DIM = 2 * C_HALF
EPS = 1e-6
NEG = -1e30
INT_MIN = -2 ** 31
VMEM_LIMIT_BYTES = 56 * 1024 * 1024
KEY_TILE = 256
ONES_ROWS = 16
I16 = jnp.int16
MIN16 = -2 ** 15
LOG2E = math.log2(math.e)


def _cparams(sem):
    return pltpu.CompilerParams(dimension_semantics=sem, vmem_limit_bytes=VMEM_LIMIT_BYTES)


def _silu(g):
    return g * jax.nn.sigmoid(g)


def _inproj_kernel(x_ref, g_ref, wr_ref, wf_ref, *out_refs, row_plan, feat_plan):
    x = x_ref[...]
    ms = jnp.mean(x * x, axis=-1, keepdims=True)
    xn = (x * lax.rsqrt(ms + EPS) * g_ref[...]).astype(BF16)
    proj = jnp.dot(xn, wr_ref[...], preferred_element_type=F32)
    projT = lax.dot_general(wf_ref[...], xn, (((1,), (1,)), ((), ())),
                            preferred_element_type=F32)
    refs = iter(out_refs)
    o = 0
    for n, dt in row_plan:
        ref = next(refs)
        ref[...] = proj[:, o:o + n].astype(dt)
        o += n
    o = 0
    for n, dt in feat_plan:
        ref = next(refs)
        ref[...] = projT[o:o + n, :].astype(dt)
        o += n


def _inproj(x2d, g, wr, wf, *, tm, row_plan, feat_plan, feat_batch):
    T, D = x2d.shape
    row = lambda n: pl.BlockSpec((tm, n), lambda i: (i, 0))
    const = lambda s: pl.BlockSpec(s, lambda i: (0, 0))
    out_shape = [jax.ShapeDtypeStruct((T, n), dt) for n, dt in row_plan]
    out_specs = [row(n) for n, _ in row_plan]
    if feat_batch is None:
        out_shape += [jax.ShapeDtypeStruct((n, T), dt) for n, dt in feat_plan]
        out_specs += [pl.BlockSpec((n, tm), lambda i: (0, i)) for n, _ in feat_plan]
    else:
        B, L = feat_batch
        npb = L // tm
        out_shape += [jax.ShapeDtypeStruct((B, n, L), dt) for n, dt in feat_plan]
        out_specs += [pl.BlockSpec((None, n, tm), lambda i: (i // npb, 0, i % npb)) for n, _ in feat_plan]
    return pl.pallas_call(
        functools.partial(_inproj_kernel, row_plan=row_plan, feat_plan=feat_plan),
        grid=(T // tm,),
        in_specs=[row(D), const((1, D)), const(wr.shape), const(wf.shape)],
        out_specs=out_specs, out_shape=out_shape,
        compiler_params=_cparams(("parallel",)),
    )(x2d, g, wr, wf)


def _amix_kernel(ax_ref, ag_ref, buf_ref, h0_ref, cw_ref, cb_ref, rw_ref, rb_ref, iw_ref, ib_ref, lam_ref,
                 ya_ref, nbuf_ref, hlast_ref, xbuf_ref, hc_ref, *, tl, nl, aw):
    li = pl.program_id(1)

    @pl.when(li == 0)
    def _():
        xbuf_ref[0:8, :] = jnp.zeros((8, aw), F32)
        xbuf_ref[8 - (CONV_W - 1):8, :] = buf_ref[...]
        hc_ref[...] = h0_ref[...]

    x = ax_ref[...]
    xbuf_ref[8:8 + tl, :] = x
    cw = cw_ref[...]
    conv = cb_ref[...] + x * cw[CONV_W - 1:CONV_W, :]
    for s in range(1, CONV_W):
        conv = conv + xbuf_ref[8 - s:8 - s + tl, :] * cw[CONV_W - 1 - s:CONV_W - s, :]

    cbf = conv.astype(BF16)
    r = jax.nn.sigmoid(jnp.dot(cbf, rw_ref[...], preferred_element_type=F32) + rb_ref[...])
    ig = jax.nn.sigmoid(jnp.dot(cbf, iw_ref[...], preferred_element_type=F32) + ib_ref[...])
    nl_lam = -lam_ref[...]
    sp = jnp.maximum(nl_lam, 0.0) + jnp.log1p(jnp.exp(-jnp.abs(nl_lam)))
    log_a = (-LRU_C) * r * sp
    a = jnp.exp(log_a)
    u = jnp.sqrt(1.0 - a * a) * (ig * conv)

    row = lax.broadcasted_iota(I32, (tl, aw), 0)
    s = 1
    while s < tl:
        a_sh = pltpu.roll(a, s, 0)
        u_sh = pltpu.roll(u, s, 0)
        keep = row >= s
        u = jnp.where(keep, a * u_sh + u, u)
        a = jnp.where(keep, a * a_sh, a)
        s *= 2
    h = a * hc_ref[...] + u
    ya_ref[...] = h * _silu(ag_ref[...])
    hc_ref[...] = h[tl - 1:tl, :]
    xbuf_ref[0:8, :] = x[tl - 8:tl, :]

    @pl.when(li == nl - 1)
    def _():
        nbuf_ref[...] = xbuf_ref[8 + tl - (CONV_W - 1):8 + tl, :]
        hlast_ref[...] = h[tl - 1:tl, :]


def _amix(ax, ag, buf, h0, cw, cb, rw, rb, iw, ib, lam, *, tl):
    B, L, aw = ax.shape
    nl = L // tl
    seq = pl.BlockSpec((None, tl, aw), lambda b, l: (b, l, 0))
    perb = lambda r: pl.BlockSpec((None, r, aw), lambda b, l: (b, 0, 0))
    const = lambda s: pl.BlockSpec(s, lambda b, l: (0, 0))
    return pl.pallas_call(
        functools.partial(_amix_kernel, tl=tl, nl=nl, aw=aw),
        grid=(B, nl),
        in_specs=[seq, seq, perb(CONV_W - 1), perb(1), const((CONV_W, aw)), const((1, aw)),
                  const((aw, aw)), const((1, aw)), const((aw, aw)), const((1, aw)), const((1, aw))],
        out_specs=[seq, perb(CONV_W - 1), perb(1)],
        out_shape=[jax.ShapeDtypeStruct((B, L, aw), F32), jax.ShapeDtypeStruct((B, CONV_W - 1, aw), F32),
                   jax.ShapeDtypeStruct((B, 1, aw), F32)],
        scratch_shapes=[pltpu.VMEM((tl + 8, aw), F32), pltpu.VMEM((1, aw), F32)],
        compiler_params=_cparams(("parallel", "arbitrary")),
    )(ax, ag, buf, h0, cw, cb, rw, rb, iw, ib, lam)


def _col_reduce(x, op):
    rows, n = x.shape
    return op(op(x.reshape(rows // 8, 8, n), axis=0), axis=0, keepdims=True)


def _diag_valid(rows, tq):
    kc = lax.broadcasted_iota(I32, (rows, tq), 0) // CHUNK
    qc = lax.broadcasted_iota(I32, (rows, tq), 1) // CHUNK
    return kc <= qc


def _ones_rows(rows):
    return jnp.where(lax.broadcasted_iota(I32, (ONES_ROWS, rows), 0) == 0, 1.0, 0.0).astype(BF16)


def _softmax_step(s, sel, vt_h, m_ref, acc_ref, idx):
    m_old = m_ref[idx]
    sm = s if sel is None else jnp.where(sel, s, NEG)
    m_new = jnp.maximum(m_old, _col_reduce(sm, jnp.max))
    alpha = jnp.exp2(m_old - m_new)
    p = jnp.exp2(sm - m_new).astype(BF16)
    pv = jnp.dot(vt_h, p, preferred_element_type=F32)
    acc_ref[idx, :, :] = alpha * acc_ref[idx, :, :] + pv
    m_ref[idx] = m_new


def _fill_block_diag(dst_ref, qT_ref, nb, tqr, nblk, blk):
    tq = nb * tqr
    dst_ref[...] = jnp.zeros(dst_ref.shape, dst_ref.dtype)
    for a in range(nb):
        for m in range(nblk):
            dst_ref[a, blk * m:blk * (m + 1), m * tq + a * tqr:m * tq + (a + 1) * tqr] = \
                qT_ref[blk * m:blk * (m + 1), a * tqr:(a + 1) * tqr]


def _own_lanes(full, nb, dh, tqr):
    if nb == 1:
        return full
    lane_seq = lax.broadcasted_iota(I32, (dh, nb * tqr), 1) // tqr
    out = full[0:dh, :]
    for a in range(1, nb):
        out = jnp.where(lane_seq == a, full[a * dh:(a + 1) * dh, :], out)
    return out


class _Keys:
    def __init__(self, P, nb, tkn, new_fm, past_refs, new_refs):
        self.P, self.nb, self.tkn, self.new_fm = P, nb, tkn, new_fm
        self.past_refs, self.new_refs = past_refs, new_refs

    def past(self, which, a, r0):
        t = self.past_refs[which][a, :, pl.ds(r0, KEY_TILE)]
        return t if which == 1 else t.T

    def new(self, which, a, r0):
        ref = self.new_refs[which]
        if self.new_fm:
            return ref[:, pl.ds(r0, self.tkn)] if which == 1 else ref[pl.ds(r0, self.tkn), :]
        t = ref[a]
        return t.T if which == 1 else t


def _dsa_kernel(*refs, P, nb, tqr, tkn, topk, idx_bits, new_fm):
    tq = nb * tqr
    tkp = KEY_TILE
    it = iter(refs)
    qT_ref, qiT_ref, wT_ref, g_ref = next(it), next(it), next(it), next(it)
    past_refs = (next(it), next(it), next(it)) if P > 0 else None
    new_refs = (next(it), next(it), next(it))
    y_ref = next(it)
    hi_ref, lo_ref, qip_ref, qbd_ref, m_ref, acc_ref = (next(it) for _ in range(6))
    kv = _Keys(P, nb, tkn, new_fm, past_refs, new_refs)
    j = pl.program_id(1)
    nnew = j + 1
    seqs = range(nb)

    _fill_block_diag(qip_ref, qiT_ref, nb, tqr, 1, IDX_HEADS * IDX_DIM)
    w = wT_ref[...]

    def put_keys(r0, rows, key):
        hi_ref[pl.ds(r0, rows), :] = (key >> 16).astype(I16)
        lo_ref[pl.ds(r0, rows), :] = ((key & 0xFFFF) + MIN16).astype(I16)

    def idx_keys(ki_tiles, valid):
        kbs = [t.astype(BF16) for t in ki_tiles]
        acc = None
        for h in range(IDX_HEADS):
            s = None
            for a in seqs:
                d = jnp.dot(kbs[a], qip_ref[a, IDX_DIM * h:IDX_DIM * (h + 1), :], preferred_element_type=F32)
                s = d if s is None else s + d
            t = jnp.maximum(s, 0.0) * w[h:h + 1, :]
            acc = t if acc is None else acc + t
        acc = jnp.where(acc == 0.0, 0.0, acc)
        bits = lax.bitcast_convert_type(acc, I32)
        key = bits ^ ((bits >> 31) & 0x7FFFFFFF)
        return key if valid is None else jnp.where(valid, key, INT_MIN)

    def for_tiles(fn, carry, n_new=None):
        if P > 0:
            carry = lax.fori_loop(0, P // tkp, lambda t, c: fn(pl.multiple_of(t * tkp, tkp), tkp, c), carry)
        return lax.fori_loop(0, nnew if n_new is None else n_new,
                             lambda i, c: fn(P + pl.multiple_of(i * tkn, tkn), tkn, c), carry)

    if P > 0:
        def past_keys(t, c):
            r0 = pl.multiple_of(t * tkp, tkp)
            put_keys(r0, tkp, idx_keys([kv.past(2, a, r0) for a in seqs], None))
            return c
        lax.fori_loop(0, P // tkp, past_keys, 0)

    def new_keys(i, c):
        r0 = pl.multiple_of(i * tkn, tkn)
        put_keys(P + r0, tkn, idx_keys([kv.new(2, a, r0) for a in seqs], None))
        return c
    lax.fori_loop(0, j, new_keys, 0)
    rd = pl.multiple_of(j * tkn, tkn)
    put_keys(P + rd, tkn, idx_keys([kv.new(2, a, rd) for a in seqs],
                                   _diag_valid(tkn, tq) if tkn > CHUNK else None))

    one, zero = jnp.int16(1), jnp.int16(0)

    def fold16(ind):
        out = ind[0:16]
        for r in range(16, ind.shape[0], 16):
            out = out + ind[r:r + 16]
        return out

    def total(parts):
        return jnp.sum(parts.astype(I32), axis=0, keepdims=True)

    def count(pred):
        def fn(r0, rows, c):
            ind = pred(lambda: hi_ref[pl.ds(r0, rows), :], lambda: lo_ref[pl.ds(r0, rows), :], r0, rows)
            return c + fold16(jnp.where(ind, one, zero))
        return total(for_tiles(fn, jnp.zeros((16, tq), I16)))

    def bisect16(pick, kth):
        def step(it, ans):
            cand = ans + jnp.left_shift(jnp.int32(1), 15 - it)
            c16 = cand.astype(I16)
            cnt = count(lambda hi, lo, r0, rows: pick(hi, lo) >= c16)
            return jnp.where(cnt >= kth, cand, ans)
        return lax.fori_loop(0, 16, step, jnp.full((1, tq), MIN16, I32))

    b = bisect16(lambda hi, lo: hi(), topk)
    b16 = b.astype(I16)

    def mask_lo(r0, rows, c):
        hi = hi_ref[pl.ds(r0, rows), :]
        lo_ref[pl.ds(r0, rows), :] = jnp.where(hi == b16, lo_ref[pl.ds(r0, rows), :], jnp.int16(MIN16))
        return c + fold16(jnp.where(hi > b16, one, zero))
    n_above = total(for_tiles(mask_lo, jnp.zeros((16, tq), I16)))
    kth_lo = topk - n_above
    cst = bisect16(lambda hi, lo: lo(), kth_lo)
    c16 = cst.astype(I16)

    def is_tie(hi, lo):
        return (hi == b16) & (lo == c16)
    n_gt = n_above + count(lambda hi, lo, r0, rows: lo() > c16)
    n_tie = count(lambda hi, lo, r0, rows: is_tie(hi(), lo()))
    need = (n_gt + n_tie > topk) & (b > MIN16)
    take = topk - n_gt

    @pl.when(jnp.max(jnp.where(need, 1, 0)) > 0)
    def _():
        def rows16(r0, rows):
            return (lax.broadcasted_iota(I32, (rows, tq), 0) + r0).astype(I16)

        def bis_row(it, x):
            t = x + jnp.left_shift(jnp.int32(1), idx_bits - 1 - it)
            t16 = t.astype(I16)
            cnt = count(lambda hi, lo, r0, rows: is_tie(hi(), lo()) & (rows16(r0, rows) < t16))
            return jnp.where(cnt < take, t, x)
        cut = lax.fori_loop(0, idx_bits, bis_row, jnp.zeros((1, tq), I32))
        cut16 = jnp.where(need, cut, 2 ** 15 - 1).astype(I16)

        def demote(r0, rows, z):
            hi, lo = hi_ref[pl.ds(r0, rows), :], lo_ref[pl.ds(r0, rows), :]
            drop = is_tie(hi, lo) & (rows16(r0, rows) > cut16)
            hi_ref[pl.ds(r0, rows), :] = jnp.where(drop, jnp.int16(MIN16), hi)
            lo_ref[pl.ds(r0, rows), :] = jnp.where(drop, jnp.int16(MIN16), lo)
            return z
        for_tiles(demote, 0)

    whole = cst == MIN16
    bs16 = jnp.where(whole, jnp.maximum(b - 1, MIN16), b).astype(I16)
    cs16 = jnp.where(whole, 2 ** 15 - 1, cst - 1).astype(I16)

    _fill_block_diag(qbd_ref, qT_ref, nb, tqr, B_HEADS, B_HEAD_DIM)
    m_ref[...] = jnp.full(m_ref.shape, NEG, F32)
    acc_ref[...] = jnp.zeros(acc_ref.shape, F32)

    def attend(k_tiles, vt_tiles, r0, rows):
        hi, lo = hi_ref[pl.ds(r0, rows), :], lo_ref[pl.ds(r0, rows), :]
        flag = jnp.where(hi > bs16, one, jnp.where(lo > cs16, one, zero))
        sel = flag.astype(I32) != 0
        s_all = None
        for a in seqs:
            d = jnp.dot(k_tiles[a].astype(BF16), qbd_ref[a], preferred_element_type=F32)
            s_all = d if s_all is None else s_all + d
        vts = [t.astype(BF16) for t in vt_tiles] + [_ones_rows(rows)]
        for h in range(B_HEADS):
            vt_h = jnp.concatenate([v[B_HEAD_DIM * h:B_HEAD_DIM * (h + 1), :] for v in vts[:-1]] + vts[-1:],
                                   axis=0)
            _softmax_step(s_all[:, h * tq:(h + 1) * tq], sel, vt_h, m_ref, acc_ref, h)

    if P > 0:
        def pa(t, z):
            r0 = pl.multiple_of(t * tkp, tkp)
            attend([kv.past(0, a, r0) for a in seqs], [kv.past(1, a, r0) for a in seqs], r0, tkp)
            return z
        lax.fori_loop(0, P // tkp, pa, 0)

    def na(i, z):
        r0 = pl.multiple_of(i * tkn, tkn)
        attend([kv.new(0, a, r0) for a in seqs], [kv.new(1, a, r0) for a in seqs], P + r0, tkn)
        return z
    lax.fori_loop(0, nnew, na, 0)

    nv = nb * B_HEAD_DIM
    oT = jnp.concatenate([_own_lanes(acc_ref[h, 0:nv, :] / acc_ref[h, nv:nv + 1, :], nb, B_HEAD_DIM, tqr)
                          for h in range(B_HEADS)], axis=0)
    y = oT.T * _silu(g_ref[...].reshape(tq, -1))
    y_ref[...] = y.reshape(y_ref.shape)


def _attn_specs(B, L, P, nb, tqr, hw, layer, new_fm, feat_rows):
    tq = nb * tqr
    if new_fm:
        feat = lambda n: pl.BlockSpec((None, n, tq), lambda b, j: (b, 0, j))
        seqblk = pl.BlockSpec((None, tqr, hw), lambda b, j: (b, j, 0))
        new_row = lambda c: pl.BlockSpec((None, L, c), lambda b, j: (b, 0, 0))
        new_col = lambda r: pl.BlockSpec((None, r, L), lambda b, j: (b, 0, 0))
    else:
        feat = lambda n: pl.BlockSpec((n, tq), lambda b, j: (0, b))
        seqblk = pl.BlockSpec((nb, tqr, hw), lambda b, j: (b, 0, 0))
        new_row = lambda c: pl.BlockSpec((nb, L, c), lambda b, j: (b, 0, 0))
        new_col = None
    past = lambda r: pl.BlockSpec((None, nb, r, P), lambda b, j: (layer, b, 0, 0))
    return feat, seqblk, new_row, new_col, past


def _dsa(qT, qiT, wT, gate, past, new, *, B, L, P, nb, tqr, layer, new_fm):
    hw = gate.shape[-1]
    tq = nb * tqr
    Lk = P + L
    topk = min(TOPK_MAX, Lk // 4)
    idx_bits = max(1, math.ceil(math.log2(Lk + 1)))
    feat, seqblk, new_row, new_col, pastspec = _attn_specs(B, L, P, nb, tqr, hw, layer, new_fm, None)
    in_specs = [feat(hw), feat(hw), feat(IDX_HEADS), seqblk]
    args = [qT, qiT, wT, gate]
    if P > 0:
        in_specs += [pastspec(hw), pastspec(hw), pastspec(IDX_DIM)]
        args += list(past)
    in_specs += [new_row(hw), new_col(hw) if new_fm else new_row(hw), new_row(IDX_DIM)]
    args += list(new)
    return pl.pallas_call(
        functools.partial(_dsa_kernel, P=P, nb=nb, tqr=tqr, tkn=tqr, topk=topk, idx_bits=idx_bits,
                          new_fm=new_fm),
        grid=(B // nb, L // tqr),
        in_specs=in_specs,
        out_specs=seqblk,
        out_shape=jax.ShapeDtypeStruct((B, L, hw), F32),
        scratch_shapes=[pltpu.VMEM((Lk, tq), I16), pltpu.VMEM((Lk, tq), I16),
                        pltpu.VMEM((nb, IDX_HEADS * IDX_DIM, tq), BF16),
                        pltpu.VMEM((nb, hw, B_HEADS * tq), BF16),
                        pltpu.VMEM((B_HEADS, 1, tq), F32),
                        pltpu.VMEM((B_HEADS, nb * B_HEAD_DIM + ONES_ROWS, tq), F32)],
        compiler_params=_cparams(("parallel", "arbitrary")),
    )(*args)


def _diff_kernel(*refs, P, nb, tqr, tkn, lam_init, new_fm):
    tq = nb * tqr
    tkp = KEY_TILE
    it = iter(refs)
    qT_ref, g_ref, lq1_ref, lk1_ref, lq2_ref, lk2_ref, sub_ref = (next(it) for _ in range(7))
    past_refs = (next(it), next(it)) if P > 0 else None
    new_refs = (next(it), next(it))
    y_ref = next(it)
    qbd_ref, m_ref, acc_ref = (next(it) for _ in range(3))
    kv = _Keys(P, nb, tkn, new_fm, past_refs, new_refs)
    j = pl.program_id(1)
    nmaps = 2 * C_HEADS
    seqs = range(nb)

    _fill_block_diag(qbd_ref, qT_ref, nb, tqr, nmaps, C_HALF)
    m_ref[...] = jnp.full(m_ref.shape, NEG, F32)
    acc_ref[...] = jnp.zeros(acc_ref.shape, F32)

    def attend(k_tiles, vt_tiles, valid):
        s_all = None
        for a in seqs:
            d = jnp.dot(k_tiles[a].astype(BF16), qbd_ref[a], preferred_element_type=F32)
            s_all = d if s_all is None else s_all + d
        vts = [t.astype(BF16) for t in vt_tiles]
        ones = _ones_rows(k_tiles[0].shape[0])
        for h in range(C_HEADS):
            vt_h = jnp.concatenate([v[C_HEAD_DIM * h:C_HEAD_DIM * (h + 1), :] for v in vts] + [ones], axis=0)
            for mi in (2 * h, 2 * h + 1):
                _softmax_step(s_all[:, mi * tq:(mi + 1) * tq], valid, vt_h, m_ref, acc_ref, mi)

    if P > 0:
        def pa(t, z):
            r0 = pl.multiple_of(t * tkp, tkp)
            attend([kv.past(0, a, r0) for a in seqs], [kv.past(1, a, r0) for a in seqs], None)
            return z
        lax.fori_loop(0, P // tkp, pa, 0)

    def na(i, z):
        r0 = pl.multiple_of(i * tkn, tkn)
        attend([kv.new(0, a, r0) for a in seqs], [kv.new(1, a, r0) for a in seqs], None)
        return z
    lax.fori_loop(0, j, na, 0)
    rd = pl.multiple_of(j * tkn, tkn)
    attend([kv.new(0, a, rd) for a in seqs], [kv.new(1, a, rd) for a in seqs],
           _diag_valid(tkn, tq) if tkn > CHUNK else None)

    lam = (jnp.exp(jnp.sum(lq1_ref[...] * lk1_ref[...], axis=-1, keepdims=True))
           - jnp.exp(jnp.sum(lq2_ref[...] * lk2_ref[...], axis=-1, keepdims=True)) + lam_init)
    outs = []
    nv = nb * C_HEAD_DIM
    for h in range(C_HEADS):
        o = (acc_ref[2 * h, 0:nv, :] / acc_ref[2 * h, nv:nv + 1, :]
             - lam * (acc_ref[2 * h + 1, 0:nv, :] / acc_ref[2 * h + 1, nv:nv + 1, :]))
        o = _own_lanes(o, nb, C_HEAD_DIM, tqr)
        ms = jnp.mean(o * o, axis=0, keepdims=True)
        outs.append(o * lax.rsqrt(ms + EPS))
    o = jnp.concatenate(outs, axis=0).T
    y = (o * sub_ref[...]) * (1.0 - lam_init) * _silu(g_ref[...].reshape(tq, -1))
    y_ref[...] = y.reshape(y_ref.shape)


def _diff(qT, gate, lams, subrow, past, new, *, B, L, P, nb, tqr, layer, new_fm, lam_init):
    hw = gate.shape[-1]
    tq = nb * tqr
    nmaps = 2 * C_HEADS
    feat, seqblk, new_row, new_col, pastspec = _attn_specs(B, L, P, nb, tqr, hw, layer, new_fm, None)
    const = lambda s: pl.BlockSpec(s, lambda b, j: (0, 0))
    in_specs = [feat(hw), seqblk] + [const((1, C_HALF))] * 4 + [const((1, hw))]
    args = [qT, gate, *lams, subrow]
    if P > 0:
        in_specs += [pastspec(hw), pastspec(hw)]
        args += list(past)
    in_specs += [new_row(hw), new_col(hw) if new_fm else new_row(hw)]
    args += list(new)
    return pl.pallas_call(
        functools.partial(_diff_kernel, P=P, nb=nb, tqr=tqr, tkn=tqr, lam_init=lam_init, new_fm=new_fm),
        grid=(B // nb, L // tqr),
        in_specs=in_specs,
        out_specs=seqblk,
        out_shape=jax.ShapeDtypeStruct((B, L, hw), F32),
        scratch_shapes=[pltpu.VMEM((nb, hw, nmaps * tq), BF16),
                        pltpu.VMEM((nmaps, 1, tq), F32),
                        pltpu.VMEM((nmaps, nb * C_HEAD_DIM + ONES_ROWS, tq), F32)],
        compiler_params=_cparams(("parallel", "arbitrary")),
    )(*args)


def _outproj_kernel(ya_ref, yb_ref, yc_ref, x_ref, w_ref, g_ref, o_ref, *, aw, hw):
    mix = jnp.dot(ya_ref[...].astype(BF16), w_ref[0:aw, :], preferred_element_type=F32)
    mix = mix + jnp.dot(yb_ref[...].astype(BF16), w_ref[aw:aw + hw, :], preferred_element_type=F32)
    mix = mix + jnp.dot(yc_ref[...].astype(BF16), w_ref[aw + hw:aw + 2 * hw, :], preferred_element_type=F32)
    ms = jnp.mean(mix * mix, axis=-1, keepdims=True)
    o_ref[...] = x_ref[...] + mix * lax.rsqrt(ms + EPS) * g_ref[...]


def _outproj(ya, yb, yc, x2d, w, g, *, tm):
    T, D = x2d.shape
    aw, hw = D // 2, D // 4
    row = lambda n: pl.BlockSpec((tm, n), lambda i: (i, 0))
    const = lambda s: pl.BlockSpec(s, lambda i: (0, 0))
    return pl.pallas_call(
        functools.partial(_outproj_kernel, aw=aw, hw=hw),
        grid=(T // tm,),
        in_specs=[row(aw), row(hw), row(hw), row(D), const((D, D)), const((1, D))],
        out_specs=row(D),
        out_shape=jax.ShapeDtypeStruct((T, D), F32),
        compiler_params=_cparams(("parallel",)),
    )(ya, yb, yc, x2d, w, g)


_ROW_PROMPT = ("a_x", "a_g", "b_g", "c_g", "b_k", "c_k", "b_ik")
_FEAT_PROMPT = ("b_q", "b_iq", "c_q", "b_k", "b_v", "c_k", "c_v", "b_ik", "b_iw")
_ROW_SAMPLE = ("a_x", "a_g", "b_g", "c_g", "b_k", "b_v", "c_k", "c_v", "b_ik")
_FEAT_SAMPLE = ("b_q", "b_iq", "c_q", "b_ik", "b_iw")
_BF16_OUT = {"b_q", "b_iq", "c_q"}


def _pack_w_in(w_in, D, rows, feats, row_bf16):
    aw, hw = D // 2, D // 4
    sizes = (aw, aw, hw, hw, hw, hw, IDX_HEADS * IDX_DIM, IDX_DIM, IDX_HEADS, hw, hw, hw, hw)
    names = ("a_x", "a_g", "b_q", "b_k", "b_v", "b_g", "b_iq", "b_ik", "b_iw", "c_q", "c_k", "c_v", "c_g")
    scale = {"b_q": B_HEAD_DIM ** -0.5 * LOG2E, "c_q": C_HALF ** -0.5 * LOG2E,
             "b_iw": (IDX_DIM ** -0.5) * (IDX_HEADS ** -0.5)}
    seg, o = {}, 0
    for n, s in zip(names, sizes):
        seg[n] = w_in[..., o:o + s] * scale[n] if n in scale else w_in[..., o:o + s]
        o += s
    depth = w_in.shape[0]

    def cat(parts, mult):
        w = jnp.concatenate([seg[n] for n in parts], axis=-1)
        pad = (-w.shape[-1]) % mult
        return jnp.concatenate([w, jnp.zeros((depth, D, pad), w.dtype)], axis=-1) if pad else w
    wr = cat(rows, 128).astype(BF16)
    wf = jnp.swapaxes(cat(feats, 16), 1, 2).astype(BF16)
    row_plan = tuple((seg[n].shape[-1], BF16 if n in row_bf16 else F32) for n in rows)
    feat_plan = tuple((seg[n].shape[-1], BF16 if n in _BF16_OUT else F32) for n in feats)
    return wr, wf, row_plan, feat_plan


def _block_diag(w):
    depth, nb, n, _ = w.shape
    eye = jnp.eye(nb, dtype=w.dtype)
    return jnp.einsum("lnde,nm->lndme", w, eye).reshape(depth, nb * n, nb * n)


def _prompt_layer(x, B, L, wl, zero_buf, zero_h, lam_init):
    T, D = x.shape
    hw = D // 4
    tm = min(256, L)
    out = _inproj(x, wl["norm_pre"], wl["wr_p"], wl["wf_p"], tm=tm, row_plan=wl["plan_p"][0],
                  feat_plan=wl["plan_p"][1], feat_batch=(B, L))
    ax, ag, bg, cg, bk_bf, ck_bf, bik = out[:7]
    bqT, biqT, cqT, bkT, bvT, ckT, cvT, bikT, biwT = out[7:]
    r3 = lambda a: a.reshape(B, L, a.shape[-1])
    ya, nbuf, hlast = _amix(r3(ax), r3(ag), zero_buf, zero_h[:, None, :], wl["a_conv_w"], wl["a_conv_b"],
                            wl["rw"], wl["a_rg_b"], wl["iw"], wl["a_in_b"], wl["a_lambda"], tl=min(L, 512))
    tqr = min(256, L)
    kw = dict(B=B, L=L, P=0, nb=1, tqr=tqr, layer=0, new_fm=True)
    yb = _dsa(bqT, biqT, biwT, r3(bg), None, (r3(bk_bf), bvT, r3(bik)), **kw)
    yc = _diff(cqT, r3(cg), wl["lams"], wl["subrow"], None, (r3(ck_bf), cvT), lam_init=lam_init, **kw)
    x_new = _outproj(ya.reshape(T, -1), yb.reshape(T, hw), yc.reshape(T, hw), x, wl["w_out"], wl["norm_post"],
                     tm=tm)
    heads = lambda t, nh: t.reshape(B, nh, t.shape[1] // nh, L).transpose(0, 3, 1, 2)
    states = (nbuf, hlast[:, 0, :], heads(bkT, B_HEADS), heads(bvT, B_HEADS), bikT.transpose(0, 2, 1),
              heads(ckT, C_HEADS), heads(cvT, C_HEADS))
    return x_new, states


def _sample_layer(x, B, L, P, layer, wl, conv_buf, h0, past, lam_init):
    T, D = x.shape
    hw = D // 4
    tm = min(256, T)
    out = _inproj(x, wl["norm_pre"], wl["wr_s"], wl["wf_s"], tm=tm, row_plan=wl["plan_s"][0],
                  feat_plan=wl["plan_s"][1], feat_batch=None)
    ax, ag, bg, cg, bk, bv, ck, cv, bik = out[:9]
    bqT, biqT, cqT, bikT, biwT = out[9:]
    r3 = lambda a: a.reshape(B, L, a.shape[-1])
    ya, nbuf, hlast = _amix(r3(ax), r3(ag), conv_buf, h0[:, None, :], wl["a_conv_w"], wl["a_conv_b"],
                            wl["rw"], wl["a_rg_b"], wl["iw"], wl["a_in_b"], wl["a_lambda"], tl=L)
    nb = 2 if B % 2 == 0 else 1
    kw = dict(B=B, L=L, P=P, nb=nb, tqr=L, layer=layer, new_fm=False)
    pkT, pvT, pkiT, pckT, pcvT = past
    yb = _dsa(bqT, biqT, biwT, r3(bg), (pkT, pvT, pkiT), (r3(bk), r3(bv), r3(bik)), **kw)
    yc = _diff(cqT, r3(cg), wl["lams"], wl["subrow"], (pckT, pcvT), (r3(ck), r3(cv)), lam_init=lam_init, **kw)
    x_new = _outproj(ya.reshape(T, -1), yb.reshape(T, hw), yc.reshape(T, hw), x, wl["w_out"], wl["norm_post"],
                     tm=tm)
    states = (nbuf, hlast[:, 0, :],
              bk.reshape(B, L, B_HEADS, B_HEAD_DIM), bv.reshape(B, L, B_HEADS, B_HEAD_DIM),
              bikT.reshape(IDX_DIM, B, L).transpose(1, 2, 0),
              ck.reshape(B, L, C_HEADS, C_HEAD_DIM), cv.reshape(B, L, C_HEADS, C_HEAD_DIM))
    return x_new, states


def kernel(x_prompt, x_sample, cache_a_conv, state_a_h, cache_b_k, cache_b_v, cache_b_kidx, cache_c_k, cache_c_v, norm_pre, norm_post, w_in, w_out, a_conv_w, a_conv_b, a_rg_w, a_rg_b, a_in_w, a_in_b, a_lambda, c_lam_q1, c_lam_k1, c_lam_q2, c_lam_k2, c_subln):
    Bp, Lp, D = x_prompt.shape
    Bs, Ls, _ = x_sample.shape
    depth = w_in.shape[0]
    P = cache_b_k.shape[2]
    aw = D // 2

    wr_p, wf_p, *plan_p = _pack_w_in(w_in, D, _ROW_PROMPT, _FEAT_PROMPT, {"b_k", "c_k"})
    wr_s, wf_s, *plan_s = _pack_w_in(w_in, D, _ROW_SAMPLE, _FEAT_SAMPLE, set())
    rw = _block_diag(a_rg_w).astype(BF16)
    iw = _block_diag(a_in_w).astype(BF16)
    wo = w_out.astype(BF16)

    fm = lambda c: jnp.transpose(c, (0, 1, 3, 4, 2)).reshape(depth, Bs, -1, P)
    past = (fm(cache_b_k), fm(cache_b_v), jnp.transpose(cache_b_kidx, (0, 1, 3, 2)), fm(cache_c_k), fm(cache_c_v))

    xp = x_prompt.reshape(Bp * Lp, D)
    xs = x_sample.reshape(Bs * Ls, D)
    zero_buf = jnp.zeros((Bp, CONV_W - 1, aw), F32)
    zero_h = jnp.zeros((Bp, aw), F32)
    p_st, s_st = [], []
    for l in range(depth):
        row = lambda a: a[l][None, :]
        wl = {"norm_pre": row(norm_pre), "norm_post": row(norm_post), "w_out": wo[l],
              "wr_p": wr_p[l], "wf_p": wf_p[l], "plan_p": plan_p, "wr_s": wr_s[l], "wf_s": wf_s[l], "plan_s": plan_s,
              "a_conv_w": a_conv_w[l], "a_conv_b": row(a_conv_b), "rw": rw[l], "a_rg_b": row(a_rg_b),
              "iw": iw[l], "a_in_b": row(a_in_b), "a_lambda": row(a_lambda),
              "lams": (row(c_lam_q1), row(c_lam_k1), row(c_lam_q2), row(c_lam_k2)),
              "subrow": jnp.tile(c_subln[l], C_HEADS)[None, :]}
        lam_init = 0.8 - 0.6 * math.exp(-0.3 * l)
        xp, st_p = _prompt_layer(xp, Bp, Lp, wl, zero_buf, zero_h, lam_init)
        xs, st_s = _sample_layer(xs, Bs, Ls, P, l, wl, cache_a_conv[l], state_a_h[l], past, lam_init)
        p_st.append(st_p)
        s_st.append(st_s)

    stk = lambda states, i: jnp.stack([s[i] for s in states], axis=0)
    return (xp.reshape(Bp, Lp, D), xs.reshape(Bs, Ls, D),
            *[stk(p_st, i) for i in range(7)], *[stk(s_st, i) for i in range(7)])
```

```python
import functools
import math

import jax
import jax.numpy as jnp
from jax import lax
from jax.experimental import pallas as pl
from jax.experimental.pallas import tpu as pltpu

F32 = jnp.float32
BF16 = jnp.bfloat16
I32 = jnp.int32

CHUNK = 64
CONV_W = 4
LRU_C = 8.0
B_HEADS = 4
B_HEAD_DIM = 64
IDX_HEADS = 8
IDX_DIM = 32
TOPK_MAX = 256
C_HEADS = 4
C_HALF = 32
C_HEAD_DIM = 2 * C_HALF
EPS = 1e-6
NEG = -1e30
INT_MIN = -2 ** 31
VMEM_LIMIT_BYTES = 56 * 1024 * 1024
KEY_TILE = 256
TILE_ELEMS = 512 * 256
ONES_ROWS = 16
I16 = jnp.int16
MIN16 = -2 ** 15
LOG2E = math.log2(math.e)


def _past_tile(P, tq):
    rows = max(KEY_TILE, TILE_ELEMS // tq)
    while P % rows:
        rows //= 2
    return rows


def _cparams(sem):
    return pltpu.CompilerParams(dimension_semantics=sem, vmem_limit_bytes=VMEM_LIMIT_BYTES)


def _silu(g):
    return g * jax.nn.sigmoid(g)


def _inproj_kernel(x_ref, g_ref, wr_ref, wf_ref, *out_refs, row_plan, feat_plan):
    x = x_ref[...]
    ms = jnp.mean(x * x, axis=-1, keepdims=True)
    xn = (x * lax.rsqrt(ms + EPS) * g_ref[...]).astype(BF16)
    proj = jnp.dot(xn, wr_ref[...], preferred_element_type=F32)
    projT = lax.dot_general(wf_ref[...], xn, (((1,), (1,)), ((), ())),
                            preferred_element_type=F32)
    refs = iter(out_refs)
    o = 0
    for n, dt in row_plan:
        ref = next(refs)
        ref[...] = proj[:, o:o + n].astype(dt)
        o += n
    o = 0
    for n, dt in feat_plan:
        ref = next(refs)
        ref[...] = projT[o:o + n, :].astype(dt)
        o += n


def _inproj(x2d, g, wr, wf, *, tm, row_plan, feat_plan, feat_batch):
    T, D = x2d.shape
    row = lambda n: pl.BlockSpec((tm, n), lambda i: (i, 0))
    const = lambda s: pl.BlockSpec(s, lambda i: (0, 0))
    out_shape = [jax.ShapeDtypeStruct((T, n), dt) for n, dt in row_plan]
    out_specs = [row(n) for n, _ in row_plan]
    if feat_batch is None:
        out_shape += [jax.ShapeDtypeStruct((n, T), dt) for n, dt in feat_plan]
        out_specs += [pl.BlockSpec((n, tm), lambda i: (0, i)) for n, _ in feat_plan]
    else:
        B, L = feat_batch
        npb = L // tm
        out_shape += [jax.ShapeDtypeStruct((B, n, L), dt) for n, dt in feat_plan]
        out_specs += [pl.BlockSpec((None, n, tm), lambda i: (i // npb, 0, i % npb)) for n, _ in feat_plan]
    return pl.pallas_call(
        functools.partial(_inproj_kernel, row_plan=row_plan, feat_plan=feat_plan),
        grid=(T // tm,),
        in_specs=[row(D), const((1, D)), const(wr.shape), const(wf.shape)],
        out_specs=out_specs, out_shape=out_shape,
        compiler_params=_cparams(("parallel",)),
    )(x2d, g, wr, wf)


def _amix_kernel(ax_ref, ag_ref, buf_ref, h0_ref, cw_ref, cb_ref, rw_ref, rb_ref, iw_ref, ib_ref, lam_ref,
                 ya_ref, nbuf_ref, hlast_ref, xbuf_ref, hc_ref, *, tl, nl, aw):
    li = pl.program_id(1)

    @pl.when(li == 0)
    def _():
        xbuf_ref[0:8, :] = jnp.zeros((8, aw), F32)
        xbuf_ref[8 - (CONV_W - 1):8, :] = buf_ref[...]
        hc_ref[...] = h0_ref[...]

    x = ax_ref[...]
    xbuf_ref[8:8 + tl, :] = x
    cw = cw_ref[...]
    conv = cb_ref[...] + x * cw[CONV_W - 1:CONV_W, :]
    for s in range(1, CONV_W):
        conv = conv + xbuf_ref[8 - s:8 - s + tl, :] * cw[CONV_W - 1 - s:CONV_W - s, :]

    cbf = conv.astype(BF16)
    r = jax.nn.sigmoid(jnp.dot(cbf, rw_ref[...], preferred_element_type=F32) + rb_ref[...])
    ig = jax.nn.sigmoid(jnp.dot(cbf, iw_ref[...], preferred_element_type=F32) + ib_ref[...])
    nl_lam = -lam_ref[...]
    sp = jnp.maximum(nl_lam, 0.0) + jnp.log1p(jnp.exp(-jnp.abs(nl_lam)))
    log_a = (-LRU_C) * r * sp
    a = jnp.exp(log_a)
    u = jnp.sqrt(1.0 - a * a) * (ig * conv)

    row = lax.broadcasted_iota(I32, (tl, aw), 0)
    s = 1
    while s < tl:
        a_sh = pltpu.roll(a, s, 0)
        u_sh = pltpu.roll(u, s, 0)
        keep = row >= s
        u = jnp.where(keep, a * u_sh + u, u)
        a = jnp.where(keep, a * a_sh, a)
        s *= 2
    h = a * hc_ref[...] + u
    ya_ref[...] = h * _silu(ag_ref[...])
    hc_ref[...] = h[tl - 1:tl, :]
    xbuf_ref[0:8, :] = x[tl - 8:tl, :]

    @pl.when(li == nl - 1)
    def _():
        nbuf_ref[...] = xbuf_ref[8 + tl - (CONV_W - 1):8 + tl, :]
        hlast_ref[...] = h[tl - 1:tl, :]


def _amix(ax, ag, buf, h0, cw, cb, rw, rb, iw, ib, lam, *, tl):
    B, L, aw = ax.shape
    nl = L // tl
    seq = pl.BlockSpec((None, tl, aw), lambda b, l: (b, l, 0))
    perb = lambda r: pl.BlockSpec((None, r, aw), lambda b, l: (b, 0, 0))
    const = lambda s: pl.BlockSpec(s, lambda b, l: (0, 0))
    return pl.pallas_call(
        functools.partial(_amix_kernel, tl=tl, nl=nl, aw=aw),
        grid=(B, nl),
        in_specs=[seq, seq, perb(CONV_W - 1), perb(1), const((CONV_W, aw)), const((1, aw)),
                  const((aw, aw)), const((1, aw)), const((aw, aw)), const((1, aw)), const((1, aw))],
        out_specs=[seq, perb(CONV_W - 1), perb(1)],
        out_shape=[jax.ShapeDtypeStruct((B, L, aw), F32), jax.ShapeDtypeStruct((B, CONV_W - 1, aw), F32),
                   jax.ShapeDtypeStruct((B, 1, aw), F32)],
        scratch_shapes=[pltpu.VMEM((tl + 8, aw), F32), pltpu.VMEM((1, aw), F32)],
        compiler_params=_cparams(("parallel", "arbitrary")),
    )(ax, ag, buf, h0, cw, cb, rw, rb, iw, ib, lam)


def _col_reduce(x, op):
    rows, n = x.shape
    return op(op(x.reshape(rows // 8, 8, n), axis=0), axis=0, keepdims=True)


def _diag_valid(rows, tq):
    kc = lax.broadcasted_iota(I32, (rows, tq), 0) // CHUNK
    qc = lax.broadcasted_iota(I32, (rows, tq), 1) // CHUNK
    return kc <= qc


def _ones_rows(rows):
    return jnp.where(lax.broadcasted_iota(I32, (ONES_ROWS, rows), 0) == 0, 1.0, 0.0).astype(BF16)


def _softmax_step(s, sel, vt_h, m_ref, acc_ref, idx):
    m_old = m_ref[idx]
    sm = s if sel is None else jnp.where(sel, s, NEG)
    m_new = jnp.maximum(m_old, _col_reduce(sm, jnp.max))
    alpha = jnp.exp2(m_old - m_new)
    p = jnp.exp2(sm - m_new).astype(BF16)
    pv = jnp.dot(vt_h, p, preferred_element_type=F32)
    acc_ref[idx, :, :] = alpha * acc_ref[idx, :, :] + pv
    m_ref[idx] = m_new


def _fill_block_diag(dst_ref, qT_ref, nb, tqr, nblk, blk, shared_rows):
    tq = nb * tqr
    dst_ref[...] = jnp.zeros(dst_ref.shape, dst_ref.dtype)
    for a in range(nb):
        for m in range(nblk):
            r = a * blk if shared_rows else (a * nblk + m) * blk
            dst_ref[r:r + blk, m * tq + a * tqr:m * tq + (a + 1) * tqr] = \
                qT_ref[blk * m:blk * (m + 1), a * tqr:(a + 1) * tqr]


def _own_lanes(full, nb, dh, tqr):
    if nb == 1:
        return full
    lane_seq = lax.broadcasted_iota(I32, (dh, nb * tqr), 1) // tqr
    out = full[0:dh, :]
    for a in range(1, nb):
        out = jnp.where(lane_seq == a, full[a * dh:(a + 1) * dh, :], out)
    return out


class _Keys:
    def __init__(self, nb, tkp, tkn, new_fm, past_refs, new_refs):
        self.nb, self.tkp, self.tkn, self.new_fm = nb, tkp, tkn, new_fm
        self.past_refs, self.new_refs = past_refs, new_refs

    def past_rows(self, which, r0):
        t = self.past_refs[which][:, :, pl.ds(r0, self.tkp)]
        return t.reshape(t.shape[0] * t.shape[1], t.shape[2]).T

    def past_vt(self, r0):
        return [self.past_refs[1][a, :, pl.ds(r0, self.tkp)] for a in range(self.nb)]

    def new_rows(self, which, r0):
        ref = self.new_refs[which]
        if self.new_fm:
            return ref[pl.ds(r0, self.tkn), :]
        return jnp.concatenate([ref[a] for a in range(self.nb)], axis=1)

    def new_vt(self, r0):
        ref = self.new_refs[1]
        if self.new_fm:
            return [ref[:, pl.ds(r0, self.tkn)]]
        return [ref[a].T for a in range(self.nb)]


def _dsa_kernel(*refs, P, nb, tqr, tkp, tkn, topk, idx_bits, new_fm):
    tq = nb * tqr
    it = iter(refs)
    qT_ref, qiT_ref, wT_ref, g_ref = next(it), next(it), next(it), next(it)
    past_refs = (next(it), next(it), next(it)) if P > 0 else None
    new_refs = (next(it), next(it), next(it))
    y_ref = next(it)
    hi_ref, lo_ref, qip_ref, qbd_ref, m_ref, acc_ref = (next(it) for _ in range(6))
    kv = _Keys(nb, tkp, tkn, new_fm, past_refs, new_refs)
    j = pl.program_id(1)
    nnew = j + 1

    _fill_block_diag(qip_ref, qiT_ref, nb, tqr, IDX_HEADS, IDX_DIM, True)
    w = wT_ref[...]

    def put_keys(r0, rows, key):
        hi_ref[pl.ds(r0, rows), :] = (key >> 16).astype(I16)
        lo_ref[pl.ds(r0, rows), :] = ((key & 0xFFFF) + MIN16).astype(I16)

    def idx_keys(ki_rows, valid):
        s_all = jnp.dot(ki_rows.astype(BF16), qip_ref[...], preferred_element_type=F32)
        acc = None
        for h in range(IDX_HEADS):
            t = jnp.maximum(s_all[:, h * tq:(h + 1) * tq], 0.0) * w[h:h + 1, :]
            acc = t if acc is None else acc + t
        acc = jnp.where(acc == 0.0, 0.0, acc)
        bits = lax.bitcast_convert_type(acc, I32)
        key = bits ^ ((bits >> 31) & 0x7FFFFFFF)
        return key if valid is None else jnp.where(valid, key, INT_MIN)

    def for_tiles(fn, carry, n_new=None):
        if P > 0:
            carry = lax.fori_loop(0, P // tkp, lambda t, c: fn(pl.multiple_of(t * tkp, tkp), tkp, c), carry)
        return lax.fori_loop(0, nnew if n_new is None else n_new,
                             lambda i, c: fn(P + pl.multiple_of(i * tkn, tkn), tkn, c), carry)

    if P > 0:
        def past_keys(t, c):
            r0 = pl.multiple_of(t * tkp, tkp)
            put_keys(r0, tkp, idx_keys(kv.past_rows(2, r0), None))
            return c
        lax.fori_loop(0, P // tkp, past_keys, 0)

    def new_keys(i, c):
        r0 = pl.multiple_of(i * tkn, tkn)
        put_keys(P + r0, tkn, idx_keys(kv.new_rows(2, r0), None))
        return c
    lax.fori_loop(0, j, new_keys, 0)
    rd = pl.multiple_of(j * tkn, tkn)
    put_keys(P + rd, tkn, idx_keys(kv.new_rows(2, rd), _diag_valid(tkn, tq) if tkn > CHUNK else None))

    one, zero = jnp.int16(1), jnp.int16(0)

    def fold16(ind):
        while ind.shape[0] > 16:
            half = ind.shape[0] // 2
            ind = ind[0:half] + ind[half:]
        return ind

    def total(parts):
        return jnp.sum(parts.astype(I32), axis=0, keepdims=True)

    def count(pred):
        def fn(r0, rows, c):
            ind = pred(lambda: hi_ref[pl.ds(r0, rows), :], lambda: lo_ref[pl.ds(r0, rows), :], r0, rows)
            return c + fold16(jnp.where(ind, one, zero))
        return total(for_tiles(fn, jnp.zeros((16, tq), I16)))

    def bisect16(pick, kth):
        def step(it, ans):
            cand = ans + jnp.left_shift(jnp.int32(1), 15 - it)
            c16 = cand.astype(I16)
            cnt = count(lambda hi, lo, r0, rows: pick(hi, lo) >= c16)
            return jnp.where(cnt >= kth, cand, ans)
        return lax.fori_loop(0, 16, step, jnp.full((1, tq), MIN16, I32))

    b = bisect16(lambda hi, lo: hi(), topk)
    b16 = b.astype(I16)

    def mask_lo(r0, rows, c):
        hi = hi_ref[pl.ds(r0, rows), :]
        lo_ref[pl.ds(r0, rows), :] = jnp.where(hi == b16, lo_ref[pl.ds(r0, rows), :], jnp.int16(MIN16))
        return c + fold16(jnp.where(hi > b16, one, zero))
    n_above = total(for_tiles(mask_lo, jnp.zeros((16, tq), I16)))
    kth_lo = topk - n_above
    cst = bisect16(lambda hi, lo: lo(), kth_lo)
    c16 = cst.astype(I16)

    def is_tie(hi, lo):
        return (hi == b16) & (lo == c16)
    n_gt = n_above + count(lambda hi, lo, r0, rows: lo() > c16)
    n_tie = count(lambda hi, lo, r0, rows: is_tie(hi(), lo()))
    need = (n_gt + n_tie > topk) & (b > MIN16)
    take = topk - n_gt

    @pl.when(jnp.max(jnp.where(need, 1, 0)) > 0)
    def _():
        def rows16(r0, rows):
            return (lax.broadcasted_iota(I32, (rows, tq), 0) + r0).astype(I16)

        def bis_row(it, x):
            t = x + jnp.left_shift(jnp.int32(1), idx_bits - 1 - it)
            t16 = t.astype(I16)
            cnt = count(lambda hi, lo, r0, rows: is_tie(hi(), lo()) & (rows16(r0, rows) < t16))
            return jnp.where(cnt < take, t, x)
        cut = lax.fori_loop(0, idx_bits, bis_row, jnp.zeros((1, tq), I32))
        cut16 = jnp.where(need, cut, 2 ** 15 - 1).astype(I16)

        def demote(r0, rows, z):
            hi, lo = hi_ref[pl.ds(r0, rows), :], lo_ref[pl.ds(r0, rows), :]
            drop = is_tie(hi, lo) & (rows16(r0, rows) > cut16)
            hi_ref[pl.ds(r0, rows), :] = jnp.where(drop, jnp.int16(MIN16), hi)
            lo_ref[pl.ds(r0, rows), :] = jnp.where(drop, jnp.int16(MIN16), lo)
            return z
        for_tiles(demote, 0)

    whole = cst == MIN16
    bs16 = jnp.where(whole, jnp.maximum(b - 1, MIN16), b).astype(I16)
    cs16 = jnp.where(whole, 2 ** 15 - 1, cst - 1).astype(I16)

    _fill_block_diag(qbd_ref, qT_ref, nb, tqr, B_HEADS, B_HEAD_DIM, False)
    m_ref[...] = jnp.full(m_ref.shape, NEG, F32)
    acc_ref[...] = jnp.zeros(acc_ref.shape, F32)

    def attend(k_rows, vt_tiles, r0, rows):
        hi, lo = hi_ref[pl.ds(r0, rows), :], lo_ref[pl.ds(r0, rows), :]
        flag = jnp.where(hi > bs16, one, jnp.where(lo > cs16, one, zero))
        sel = flag.astype(I32) != 0
        s_all = jnp.dot(k_rows.astype(BF16), qbd_ref[...], preferred_element_type=F32)
        vts = [t.astype(BF16) for t in vt_tiles]
        ones = _ones_rows(rows)
        for h in range(B_HEADS):
            vt_h = jnp.concatenate([v[B_HEAD_DIM * h:B_HEAD_DIM * (h + 1), :] for v in vts] + [ones], axis=0)
            _softmax_step(s_all[:, h * tq:(h + 1) * tq], sel, vt_h, m_ref, acc_ref, h)

    if P > 0:
        def pa(t, z):
            r0 = pl.multiple_of(t * tkp, tkp)
            attend(kv.past_rows(0, r0), kv.past_vt(r0), r0, tkp)
            return z
        lax.fori_loop(0, P // tkp, pa, 0)

    def na(i, z):
        r0 = pl.multiple_of(i * tkn, tkn)
        attend(kv.new_rows(0, r0), kv.new_vt(r0), P + r0, tkn)
        return z
    lax.fori_loop(0, nnew, na, 0)

    nv = nb * B_HEAD_DIM
    oT = jnp.concatenate([_own_lanes(acc_ref[h, 0:nv, :] / acc_ref[h, nv:nv + 1, :], nb, B_HEAD_DIM, tqr)
                          for h in range(B_HEADS)], axis=0)
    y = oT.T * _silu(g_ref[...].reshape(tq, -1))
    y_ref[...] = y.reshape(y_ref.shape)


def _attn_specs(B, L, P, nb, tqr, hw, layer, new_fm, feat_rows):
    tq = nb * tqr
    if new_fm:
        feat = lambda n: pl.BlockSpec((None, n, tq), lambda b, j: (b, 0, j))
        seqblk = pl.BlockSpec((None, tqr, hw), lambda b, j: (b, j, 0))
        new_row = lambda c: pl.BlockSpec((None, L, c), lambda b, j: (b, 0, 0))
        new_col = lambda r: pl.BlockSpec((None, r, L), lambda b, j: (b, 0, 0))
    else:
        feat = lambda n: pl.BlockSpec((n, tq), lambda b, j: (0, b))
        seqblk = pl.BlockSpec((nb, tqr, hw), lambda b, j: (b, 0, 0))
        new_row = lambda c: pl.BlockSpec((nb, L, c), lambda b, j: (b, 0, 0))
        new_col = None
    past = lambda r: pl.BlockSpec((None, nb, r, P), lambda b, j: (layer, b, 0, 0))
    return feat, seqblk, new_row, new_col, past


def _dsa(qT, qiT, wT, gate, past, new, *, B, L, P, nb, tqr, layer, new_fm):
    hw = gate.shape[-1]
    tq = nb * tqr
    Lk = P + L
    topk = min(TOPK_MAX, Lk // 4)
    idx_bits = max(1, math.ceil(math.log2(Lk + 1)))
    feat, seqblk, new_row, new_col, pastspec = _attn_specs(B, L, P, nb, tqr, hw, layer, new_fm, None)
    in_specs = [feat(hw), feat(hw), feat(IDX_HEADS), seqblk]
    args = [qT, qiT, wT, gate]
    if P > 0:
        in_specs += [pastspec(hw), pastspec(hw), pastspec(IDX_DIM)]
        args += list(past)
    in_specs += [new_row(hw), new_col(hw) if new_fm else new_row(hw), new_row(IDX_DIM)]
    args += list(new)
    return pl.pallas_call(
        functools.partial(_dsa_kernel, P=P, nb=nb, tqr=tqr, tkp=_past_tile(P, tq), tkn=tqr, topk=topk,
                          idx_bits=idx_bits, new_fm=new_fm),
        grid=(B // nb, L // tqr),
        in_specs=in_specs,
        out_specs=seqblk,
        out_shape=jax.ShapeDtypeStruct((B, L, hw), F32),
        scratch_shapes=[pltpu.VMEM((Lk, tq), I16), pltpu.VMEM((Lk, tq), I16),
                        pltpu.VMEM((nb * IDX_DIM, IDX_HEADS * tq), BF16),
                        pltpu.VMEM((nb * hw, B_HEADS * tq), BF16),
                        pltpu.VMEM((B_HEADS, 1, tq), F32),
                        pltpu.VMEM((B_HEADS, nb * B_HEAD_DIM + ONES_ROWS, tq), F32)],
        compiler_params=_cparams(("parallel", "arbitrary")),
    )(*args)


def _diff_kernel(*refs, P, nb, tqr, tkp, tkn, lam_init, new_fm):
    tq = nb * tqr
    it = iter(refs)
    qT_ref, g_ref, lq1_ref, lk1_ref, lq2_ref, lk2_ref, sub_ref = (next(it) for _ in range(7))
    past_refs = (next(it), next(it)) if P > 0 else None
    new_refs = (next(it), next(it))
    y_ref = next(it)
    qbd_ref, m_ref, acc_ref = (next(it) for _ in range(3))
    kv = _Keys(nb, tkp, tkn, new_fm, past_refs, new_refs)
    j = pl.program_id(1)
    nmaps = 2 * C_HEADS

    _fill_block_diag(qbd_ref, qT_ref, nb, tqr, nmaps, C_HALF, False)
    m_ref[...] = jnp.full(m_ref.shape, NEG, F32)
    acc_ref[...] = jnp.zeros(acc_ref.shape, F32)

    def attend(k_rows, vt_tiles, valid):
        s_all = jnp.dot(k_rows.astype(BF16), qbd_ref[...], preferred_element_type=F32)
        vts = [t.astype(BF16) for t in vt_tiles]
        ones = _ones_rows(k_rows.shape[0])
        for h in range(C_HEADS):
            vt_h = jnp.concatenate([v[C_HEAD_DIM * h:C_HEAD_DIM * (h + 1), :] for v in vts] + [ones], axis=0)
            for mi in (2 * h, 2 * h + 1):
                _softmax_step(s_all[:, mi * tq:(mi + 1) * tq], valid, vt_h, m_ref, acc_ref, mi)

    if P > 0:
        def pa(t, z):
            r0 = pl.multiple_of(t * tkp, tkp)
            attend(kv.past_rows(0, r0), kv.past_vt(r0), None)
            return z
        lax.fori_loop(0, P // tkp, pa, 0)

    def na(i, z):
        r0 = pl.multiple_of(i * tkn, tkn)
        attend(kv.new_rows(0, r0), kv.new_vt(r0), None)
        return z
    lax.fori_loop(0, j, na, 0)
    rd = pl.multiple_of(j * tkn, tkn)
    attend(kv.new_rows(0, rd), kv.new_vt(rd), _diag_valid(tkn, tq) if tkn > CHUNK else None)

    lam = (jnp.exp(jnp.sum(lq1_ref[...] * lk1_ref[...], axis=-1, keepdims=True))
           - jnp.exp(jnp.sum(lq2_ref[...] * lk2_ref[...], axis=-1, keepdims=True)) + lam_init)
    outs = []
    nv = nb * C_HEAD_DIM
    for h in range(C_HEADS):
        o = (acc_ref[2 * h, 0:nv, :] / acc_ref[2 * h, nv:nv + 1, :]
             - lam * (acc_ref[2 * h + 1, 0:nv, :] / acc_ref[2 * h + 1, nv:nv + 1, :]))
        o = _own_lanes(o, nb, C_HEAD_DIM, tqr)
        ms = jnp.mean(o * o, axis=0, keepdims=True)
        outs.append(o * lax.rsqrt(ms + EPS))
    o = jnp.concatenate(outs, axis=0).T
    y = (o * sub_ref[...]) * (1.0 - lam_init) * _silu(g_ref[...].reshape(tq, -1))
    y_ref[...] = y.reshape(y_ref.shape)


def _diff(qT, gate, lams, subrow, past, new, *, B, L, P, nb, tqr, layer, new_fm, lam_init):
    hw = gate.shape[-1]
    tq = nb * tqr
    nmaps = 2 * C_HEADS
    feat, seqblk, new_row, new_col, pastspec = _attn_specs(B, L, P, nb, tqr, hw, layer, new_fm, None)
    const = lambda s: pl.BlockSpec(s, lambda b, j: (0, 0))
    in_specs = [feat(hw), seqblk] + [const((1, C_HALF))] * 4 + [const((1, hw))]
    args = [qT, gate, *lams, subrow]
    if P > 0:
        in_specs += [pastspec(hw), pastspec(hw)]
        args += list(past)
    in_specs += [new_row(hw), new_col(hw) if new_fm else new_row(hw)]
    args += list(new)
    return pl.pallas_call(
        functools.partial(_diff_kernel, P=P, nb=nb, tqr=tqr, tkp=_past_tile(P, tq), tkn=tqr,
                          lam_init=lam_init, new_fm=new_fm),
        grid=(B // nb, L // tqr),
        in_specs=in_specs,
        out_specs=seqblk,
        out_shape=jax.ShapeDtypeStruct((B, L, hw), F32),
        scratch_shapes=[pltpu.VMEM((nb * hw, nmaps * tq), BF16),
                        pltpu.VMEM((nmaps, 1, tq), F32),
                        pltpu.VMEM((nmaps, nb * C_HEAD_DIM + ONES_ROWS, tq), F32)],
        compiler_params=_cparams(("parallel", "arbitrary")),
    )(*args)


def _outproj_kernel(ya_ref, yb_ref, yc_ref, x_ref, w_ref, g_ref, o_ref, *, aw, hw):
    mix = jnp.dot(ya_ref[...].astype(BF16), w_ref[0:aw, :], preferred_element_type=F32)
    mix = mix + jnp.dot(yb_ref[...].astype(BF16), w_ref[aw:aw + hw, :], preferred_element_type=F32)
    mix = mix + jnp.dot(yc_ref[...].astype(BF16), w_ref[aw + hw:aw + 2 * hw, :], preferred_element_type=F32)
    ms = jnp.mean(mix * mix, axis=-1, keepdims=True)
    o_ref[...] = x_ref[...] + mix * lax.rsqrt(ms + EPS) * g_ref[...]


def _outproj(ya, yb, yc, x2d, w, g, *, tm):
    T, D = x2d.shape
    aw, hw = D // 2, D // 4
    row = lambda n: pl.BlockSpec((tm, n), lambda i: (i, 0))
    const = lambda s: pl.BlockSpec(s, lambda i: (0, 0))
    return pl.pallas_call(
        functools.partial(_outproj_kernel, aw=aw, hw=hw),
        grid=(T // tm,),
        in_specs=[row(aw), row(hw), row(hw), row(D), const((D, D)), const((1, D))],
        out_specs=row(D),
        out_shape=jax.ShapeDtypeStruct((T, D), F32),
        compiler_params=_cparams(("parallel",)),
    )(ya, yb, yc, x2d, w, g)


_ROW_PROMPT = ("a_x", "a_g", "b_g", "c_g", "b_k", "c_k", "b_ik")
_FEAT_PROMPT = ("b_q", "b_iq", "c_q", "b_k", "b_v", "c_k", "c_v", "b_ik", "b_iw")
_ROW_SAMPLE = ("a_x", "a_g", "b_g", "c_g", "b_k", "b_v", "c_k", "c_v", "b_ik")
_FEAT_SAMPLE = ("b_q", "b_iq", "c_q", "b_ik", "b_iw")
_BF16_OUT = {"b_q", "b_iq", "c_q"}


def _pack_w_in(w_in, D, rows, feats, row_bf16):
    aw, hw = D // 2, D // 4
    sizes = (aw, aw, hw, hw, hw, hw, IDX_HEADS * IDX_DIM, IDX_DIM, IDX_HEADS, hw, hw, hw, hw)
    names = ("a_x", "a_g", "b_q", "b_k", "b_v", "b_g", "b_iq", "b_ik", "b_iw", "c_q", "c_k", "c_v", "c_g")
    scale = {"b_q": B_HEAD_DIM ** -0.5 * LOG2E, "c_q": C_HALF ** -0.5 * LOG2E,
             "b_iw": (IDX_DIM ** -0.5) * (IDX_HEADS ** -0.5)}
    seg, o = {}, 0
    for n, s in zip(names, sizes):
        seg[n] = w_in[..., o:o + s] * scale[n] if n in scale else w_in[..., o:o + s]
        o += s
    depth = w_in.shape[0]

    def cat(parts, mult):
        w = jnp.concatenate([seg[n] for n in parts], axis=-1)
        pad = (-w.shape[-1]) % mult
        return jnp.concatenate([w, jnp.zeros((depth, D, pad), w.dtype)], axis=-1) if pad else w
    wr = cat(rows, 128).astype(BF16)
    wf = jnp.swapaxes(cat(feats, 16), 1, 2).astype(BF16)
    row_plan = tuple((seg[n].shape[-1], BF16 if n in row_bf16 else F32) for n in rows)
    feat_plan = tuple((seg[n].shape[-1], BF16 if n in _BF16_OUT else F32) for n in feats)
    return wr, wf, row_plan, feat_plan


def _block_diag(w):
    depth, nb, n, _ = w.shape
    eye = jnp.eye(nb, dtype=w.dtype)
    return jnp.einsum("lnde,nm->lndme", w, eye).reshape(depth, nb * n, nb * n)


def _prompt_layer(x, B, L, wl, zero_buf, zero_h, lam_init):
    T, D = x.shape
    hw = D // 4
    tm = min(256, L)
    out = _inproj(x, wl["norm_pre"], wl["wr_p"], wl["wf_p"], tm=tm, row_plan=wl["plan_p"][0],
                  feat_plan=wl["plan_p"][1], feat_batch=(B, L))
    ax, ag, bg, cg, bk_bf, ck_bf, bik = out[:7]
    bqT, biqT, cqT, bkT, bvT, ckT, cvT, bikT, biwT = out[7:]
    r3 = lambda a: a.reshape(B, L, a.shape[-1])
    ya, nbuf, hlast = _amix(r3(ax), r3(ag), zero_buf, zero_h[:, None, :], wl["a_conv_w"], wl["a_conv_b"],
                            wl["rw"], wl["a_rg_b"], wl["iw"], wl["a_in_b"], wl["a_lambda"], tl=min(L, 512))
    tqr = min(512, L)
    kw = dict(B=B, L=L, P=0, nb=1, tqr=tqr, layer=0, new_fm=True)
    yb = _dsa(bqT, biqT, biwT, r3(bg), None, (r3(bk_bf), bvT, r3(bik)), **kw)
    yc = _diff(cqT, r3(cg), wl["lams"], wl["subrow"], None, (r3(ck_bf), cvT), lam_init=lam_init, **kw)
    x_new = _outproj(ya.reshape(T, -1), yb.reshape(T, hw), yc.reshape(T, hw), x, wl["w_out"], wl["norm_post"],
                     tm=tm)
    heads = lambda t, nh: t.reshape(B, nh, t.shape[1] // nh, L).transpose(0, 3, 1, 2)
    states = (nbuf, hlast[:, 0, :], heads(bkT, B_HEADS), heads(bvT, B_HEADS), bikT.transpose(0, 2, 1),
              heads(ckT, C_HEADS), heads(cvT, C_HEADS))
    return x_new, states


def _sample_layer(x, B, L, P, layer, wl, conv_buf, h0, past, lam_init):
    T, D = x.shape
    hw = D // 4
    tm = min(256, T)
    out = _inproj(x, wl["norm_pre"], wl["wr_s"], wl["wf_s"], tm=tm, row_plan=wl["plan_s"][0],
                  feat_plan=wl["plan_s"][1], feat_batch=None)
    ax, ag, bg, cg, bk, bv, ck, cv, bik = out[:9]
    bqT, biqT, cqT, bikT, biwT = out[9:]
    r3 = lambda a: a.reshape(B, L, a.shape[-1])
    ya, nbuf, hlast = _amix(r3(ax), r3(ag), conv_buf, h0[:, None, :], wl["a_conv_w"], wl["a_conv_b"],
                            wl["rw"], wl["a_rg_b"], wl["iw"], wl["a_in_b"], wl["a_lambda"], tl=L)
    nb = 2 if B % 2 == 0 else 1
    kw = dict(B=B, L=L, P=P, nb=nb, tqr=L, layer=layer, new_fm=False)
    pkT, pvT, pkiT, pckT, pcvT = past
    yb = _dsa(bqT, biqT, biwT, r3(bg), (pkT, pvT, pkiT), (r3(bk), r3(bv), r3(bik)), **kw)
    yc = _diff(cqT, r3(cg), wl["lams"], wl["subrow"], (pckT, pcvT), (r3(ck), r3(cv)), lam_init=lam_init, **kw)
    x_new = _outproj(ya.reshape(T, -1), yb.reshape(T, hw), yc.reshape(T, hw), x, wl["w_out"], wl["norm_post"],
                     tm=tm)
    states = (nbuf, hlast[:, 0, :],
              bk.reshape(B, L, B_HEADS, B_HEAD_DIM), bv.reshape(B, L, B_HEADS, B_HEAD_DIM),
              bikT.reshape(IDX_DIM, B, L).transpose(1, 2, 0),
              ck.reshape(B, L, C_HEADS, C_HEAD_DIM), cv.reshape(B, L, C_HEADS, C_HEAD_DIM))
    return x_new, states


def kernel(x_prompt, x_sample, cache_a_conv, state_a_h, cache_b_k, cache_b_v, cache_b_kidx, cache_c_k, cache_c_v, norm_pre, norm_post, w_in, w_out, a_conv_w, a_conv_b, a_rg_w, a_rg_b, a_in_w, a_in_b, a_lambda, c_lam_q1, c_lam_k1, c_lam_q2, c_lam_k2, c_subln):
    Bp, Lp, D = x_prompt.shape
    Bs, Ls, _ = x_sample.shape
    depth = w_in.shape[0]
    P = cache_b_k.shape[2]
    aw = D // 2

    wr_p, wf_p, *plan_p = _pack_w_in(w_in, D, _ROW_PROMPT, _FEAT_PROMPT, {"b_k", "c_k"})
    wr_s, wf_s, *plan_s = _pack_w_in(w_in, D, _ROW_SAMPLE, _FEAT_SAMPLE, set())
    rw = _block_diag(a_rg_w).astype(BF16)
    iw = _block_diag(a_in_w).astype(BF16)
    wo = w_out.astype(BF16)

    fm = lambda c: jnp.transpose(c, (0, 1, 3, 4, 2)).reshape(depth, Bs, -1, P)
    past = (fm(cache_b_k), fm(cache_b_v), jnp.transpose(cache_b_kidx, (0, 1, 3, 2)), fm(cache_c_k), fm(cache_c_v))

    xp = x_prompt.reshape(Bp * Lp, D)
    xs = x_sample.reshape(Bs * Ls, D)
    zero_buf = jnp.zeros((Bp, CONV_W - 1, aw), F32)
    zero_h = jnp.zeros((Bp, aw), F32)
    p_st, s_st = [], []
    for l in range(depth):
        row = lambda a: a[l][None, :]
        wl = {"norm_pre": row(norm_pre), "norm_post": row(norm_post), "w_out": wo[l],
              "wr_p": wr_p[l], "wf_p": wf_p[l], "plan_p": plan_p, "wr_s": wr_s[l], "wf_s": wf_s[l], "plan_s": plan_s,
              "a_conv_w": a_conv_w[l], "a_conv_b": row(a_conv_b), "rw": rw[l], "a_rg_b": row(a_rg_b),
              "iw": iw[l], "a_in_b": row(a_in_b), "a_lambda": row(a_lambda),
              "lams": (row(c_lam_q1), row(c_lam_k1), row(c_lam_q2), row(c_lam_k2)),
              "subrow": jnp.tile(c_subln[l], C_HEADS)[None, :]}
        lam_init = 0.8 - 0.6 * math.exp(-0.3 * l)
        xp, st_p = _prompt_layer(xp, Bp, Lp, wl, zero_buf, zero_h, lam_init)
        xs, st_s = _sample_layer(xs, Bs, Ls, P, l, wl, cache_a_conv[l], state_a_h[l], past, lam_init)
        p_st.append(st_p)
        s_st.append(st_s)

    stk = lambda states, i: jnp.stack([s[i] for s in states], axis=0)
    return (xp.reshape(Bp, Lp, D), xs.reshape(Bs, Ls, D),
            *[stk(p_st, i) for i in range(7)], *[stk(s_st, i) for i in range(7)])
```

```python
import functools
import math

import jax
import jax.numpy as jnp
from jax import lax
from jax.experimental import pallas as pl
from jax.experimental.pallas import tpu as pltpu

F32 = jnp.float32
BF16 = jnp.bfloat16
I32 = jnp.int32

CHUNK = 64
CONV_W = 4
LRU_C = 8.0
B_HEADS = 4
B_HEAD_DIM = 64
IDX_HEADS = 8
IDX_DIM = 32
TOPK_MAX = 256
C_HEADS = 4
C_HALF = 32
C_HEAD_DIM = 2 * C_HALF
EPS = 1e-6
NEG = -1e30
INT_MIN = -2 ** 31
VMEM_LIMIT_BYTES = 56 * 1024 * 1024
KEY_TILE = 256
TILE_ELEMS = 512 * 256
TIE_ROWS = 256
ONES_ROWS = 16
I16 = jnp.int16
MIN16 = -2 ** 15
LOG2E = math.log2(math.e)


def _past_tile(P, tq):
    rows = max(KEY_TILE, TILE_ELEMS // tq)
    while P % rows:
        rows //= 2
    return rows


def _cparams(sem):
    return pltpu.CompilerParams(dimension_semantics=sem, vmem_limit_bytes=VMEM_LIMIT_BYTES)


def _silu(g):
    return g * jax.nn.sigmoid(g)


def _inproj_kernel(x_ref, g_ref, wr_ref, wf_ref, *out_refs, row_plan, feat_plan):
    x = x_ref[...]
    ms = jnp.mean(x * x, axis=-1, keepdims=True)
    xn = (x * lax.rsqrt(ms + EPS) * g_ref[...]).astype(BF16)
    proj = jnp.dot(xn, wr_ref[...], preferred_element_type=F32)
    projT = lax.dot_general(wf_ref[...], xn, (((1,), (1,)), ((), ())),
                            preferred_element_type=F32)
    refs = iter(out_refs)
    o = 0
    for n, dt in row_plan:
        ref = next(refs)
        ref[...] = proj[:, o:o + n].astype(dt)
        o += n
    o = 0
    for n, dt in feat_plan:
        ref = next(refs)
        ref[...] = projT[o:o + n, :].astype(dt)
        o += n


def _inproj(x2d, g, wr, wf, *, tm, row_plan, feat_plan, feat_batch):
    T, D = x2d.shape
    row = lambda n: pl.BlockSpec((tm, n), lambda i: (i, 0))
    const = lambda s: pl.BlockSpec(s, lambda i: (0, 0))
    out_shape = [jax.ShapeDtypeStruct((T, n), dt) for n, dt in row_plan]
    out_specs = [row(n) for n, _ in row_plan]
    if feat_batch is None:
        out_shape += [jax.ShapeDtypeStruct((n, T), dt) for n, dt in feat_plan]
        out_specs += [pl.BlockSpec((n, tm), lambda i: (0, i)) for n, _ in feat_plan]
    else:
        B, L = feat_batch
        npb = L // tm
        out_shape += [jax.ShapeDtypeStruct((B, n, L), dt) for n, dt in feat_plan]
        out_specs += [pl.BlockSpec((None, n, tm), lambda i: (i // npb, 0, i % npb)) for n, _ in feat_plan]
    return pl.pallas_call(
        functools.partial(_inproj_kernel, row_plan=row_plan, feat_plan=feat_plan),
        grid=(T // tm,),
        in_specs=[row(D), const((1, D)), const(wr.shape), const(wf.shape)],
        out_specs=out_specs, out_shape=out_shape,
        compiler_params=_cparams(("parallel",)),
    )(x2d, g, wr, wf)


def _amix_kernel(ax_ref, ag_ref, buf_ref, h0_ref, cw_ref, cb_ref, rw_ref, rb_ref, iw_ref, ib_ref, lam_ref,
                 ya_ref, nbuf_ref, hlast_ref, xbuf_ref, hc_ref, *, tl, nl, aw):
    li = pl.program_id(1)

    @pl.when(li == 0)
    def _():
        xbuf_ref[0:8, :] = jnp.zeros((8, aw), F32)
        xbuf_ref[8 - (CONV_W - 1):8, :] = buf_ref[...]
        hc_ref[...] = h0_ref[...]

    x = ax_ref[...]
    xbuf_ref[8:8 + tl, :] = x
    cw = cw_ref[...]
    conv = cb_ref[...] + x * cw[CONV_W - 1:CONV_W, :]
    for s in range(1, CONV_W):
        conv = conv + xbuf_ref[8 - s:8 - s + tl, :] * cw[CONV_W - 1 - s:CONV_W - s, :]

    cbf = conv.astype(BF16)
    r = jax.nn.sigmoid(jnp.dot(cbf, rw_ref[...], preferred_element_type=F32) + rb_ref[...])
    ig = jax.nn.sigmoid(jnp.dot(cbf, iw_ref[...], preferred_element_type=F32) + ib_ref[...])
    nl_lam = -lam_ref[...]
    sp = jnp.maximum(nl_lam, 0.0) + jnp.log1p(jnp.exp(-jnp.abs(nl_lam)))
    log_a = (-LRU_C) * r * sp
    a = jnp.exp(log_a)
    u = jnp.sqrt(1.0 - a * a) * (ig * conv)

    row = lax.broadcasted_iota(I32, (tl, aw), 0)
    s = 1
    while s < tl:
        a_sh = pltpu.roll(a, s, 0)
        u_sh = pltpu.roll(u, s, 0)
        keep = row >= s
        u = jnp.where(keep, a * u_sh + u, u)
        a = jnp.where(keep, a * a_sh, a)
        s *= 2
    h = a * hc_ref[...] + u
    ya_ref[...] = h * _silu(ag_ref[...])
    hc_ref[...] = h[tl - 1:tl, :]
    xbuf_ref[0:8, :] = x[tl - 8:tl, :]

    @pl.when(li == nl - 1)
    def _():
        nbuf_ref[...] = xbuf_ref[8 + tl - (CONV_W - 1):8 + tl, :]
        hlast_ref[...] = h[tl - 1:tl, :]


def _amix(ax, ag, buf, h0, cw, cb, rw, rb, iw, ib, lam, *, tl):
    B, L, aw = ax.shape
    nl = L // tl
    seq = pl.BlockSpec((None, tl, aw), lambda b, l: (b, l, 0))
    perb = lambda r: pl.BlockSpec((None, r, aw), lambda b, l: (b, 0, 0))
    const = lambda s: pl.BlockSpec(s, lambda b, l: (0, 0))
    return pl.pallas_call(
        functools.partial(_amix_kernel, tl=tl, nl=nl, aw=aw),
        grid=(B, nl),
        in_specs=[seq, seq, perb(CONV_W - 1), perb(1), const((CONV_W, aw)), const((1, aw)),
                  const((aw, aw)), const((1, aw)), const((aw, aw)), const((1, aw)), const((1, aw))],
        out_specs=[seq, perb(CONV_W - 1), perb(1)],
        out_shape=[jax.ShapeDtypeStruct((B, L, aw), F32), jax.ShapeDtypeStruct((B, CONV_W - 1, aw), F32),
                   jax.ShapeDtypeStruct((B, 1, aw), F32)],
        scratch_shapes=[pltpu.VMEM((tl + 8, aw), F32), pltpu.VMEM((1, aw), F32)],
        compiler_params=_cparams(("parallel", "arbitrary")),
    )(ax, ag, buf, h0, cw, cb, rw, rb, iw, ib, lam)


def _col_reduce(x, op):
    rows, n = x.shape
    return op(op(x.reshape(rows // 8, 8, n), axis=0), axis=0, keepdims=True)


def _diag_valid(rows, tq):
    kc = lax.broadcasted_iota(I32, (rows, tq), 0) // CHUNK
    qc = lax.broadcasted_iota(I32, (rows, tq), 1) // CHUNK
    return kc <= qc


def _ones_rows(rows):
    return jnp.where(lax.broadcasted_iota(I32, (ONES_ROWS, rows), 0) == 0, 1.0, 0.0).astype(BF16)


def _softmax_step(s, sel, vt_h, m_ref, acc_ref, idx):
    m_old = m_ref[idx]
    sm = s if sel is None else jnp.where(sel, s, NEG)
    m_new = jnp.maximum(m_old, _col_reduce(sm, jnp.max))
    alpha = jnp.exp2(m_old - m_new)
    p = jnp.exp2(sm - m_new).astype(BF16)
    pv = jnp.dot(vt_h, p, preferred_element_type=F32)
    acc_ref[idx, :, :] = alpha * acc_ref[idx, :, :] + pv
    m_ref[idx] = m_new


def _fill_block_diag(dst_ref, qT_ref, nb, tqr, nblk, blk, shared_rows):
    tq = nb * tqr
    dst_ref[...] = jnp.zeros(dst_ref.shape, dst_ref.dtype)
    for a in range(nb):
        for m in range(nblk):
            r = a * blk if shared_rows else (a * nblk + m) * blk
            dst_ref[r:r + blk, m * tq + a * tqr:m * tq + (a + 1) * tqr] = \
                qT_ref[blk * m:blk * (m + 1), a * tqr:(a + 1) * tqr]


def _own_lanes(full, nb, dh, tqr):
    if nb == 1:
        return full
    lane_seq = lax.broadcasted_iota(I32, (dh, nb * tqr), 1) // tqr
    out = full[0:dh, :]
    for a in range(1, nb):
        out = jnp.where(lane_seq == a, full[a * dh:(a + 1) * dh, :], out)
    return out


class _Keys:
    def __init__(self, nb, tkp, tkn, new_fm, past_refs, new_refs):
        self.nb, self.tkp, self.tkn, self.new_fm = nb, tkp, tkn, new_fm
        self.past_refs, self.new_refs = past_refs, new_refs

    def past_rows(self, which, r0):
        t = self.past_refs[which][:, :, pl.ds(r0, self.tkp)]
        return t.reshape(t.shape[0] * t.shape[1], t.shape[2]).T

    def past_vt(self, r0):
        return [self.past_refs[1][a, :, pl.ds(r0, self.tkp)] for a in range(self.nb)]

    def new_rows(self, which, r0):
        ref = self.new_refs[which]
        if self.new_fm:
            return ref[pl.ds(r0, self.tkn), :]
        return jnp.concatenate([ref[a] for a in range(self.nb)], axis=1)

    def new_vt(self, r0):
        ref = self.new_refs[1]
        if self.new_fm:
            return [ref[:, pl.ds(r0, self.tkn)]]
        return [ref[a].T for a in range(self.nb)]


def _dsa_kernel(*refs, P, nb, tqr, tkp, tkn, topk, new_fm):
    tq = nb * tqr
    it = iter(refs)
    qT_ref, qiT_ref, wT_ref, g_ref = next(it), next(it), next(it), next(it)
    past_refs = (next(it), next(it), next(it)) if P > 0 else None
    new_refs = (next(it), next(it), next(it))
    y_ref = next(it)
    hi_ref, lo_ref, qip_ref, qbd_ref, m_ref, acc_ref = (next(it) for _ in range(6))
    kv = _Keys(nb, tkp, tkn, new_fm, past_refs, new_refs)
    j = pl.program_id(1)
    nnew = j + 1

    _fill_block_diag(qip_ref, qiT_ref, nb, tqr, IDX_HEADS, IDX_DIM, True)
    w = wT_ref[...]

    def put_keys(r0, rows, key):
        hi_ref[pl.ds(r0, rows), :] = (key >> 16).astype(I16)
        lo_ref[pl.ds(r0, rows), :] = ((key & 0xFFFF) + MIN16).astype(I16)

    def idx_keys(ki_rows, valid):
        s_all = jnp.dot(ki_rows.astype(BF16), qip_ref[...], preferred_element_type=F32)
        acc = None
        for h in range(IDX_HEADS):
            t = jnp.maximum(s_all[:, h * tq:(h + 1) * tq], 0.0) * w[h:h + 1, :]
            acc = t if acc is None else acc + t
        acc = jnp.where(acc == 0.0, 0.0, acc)
        bits = lax.bitcast_convert_type(acc, I32)
        key = bits ^ ((bits >> 31) & 0x7FFFFFFF)
        return key if valid is None else jnp.where(valid, key, INT_MIN)

    def for_tiles(fn, carry, n_new=None):
        if P > 0:
            carry = lax.fori_loop(0, P // tkp, lambda t, c: fn(pl.multiple_of(t * tkp, tkp), tkp, c), carry)
        return lax.fori_loop(0, nnew if n_new is None else n_new,
                             lambda i, c: fn(P + pl.multiple_of(i * tkn, tkn), tkn, c), carry)

    if P > 0:
        def past_keys(t, c):
            r0 = pl.multiple_of(t * tkp, tkp)
            put_keys(r0, tkp, idx_keys(kv.past_rows(2, r0), None))
            return c
        lax.fori_loop(0, P // tkp, past_keys, 0)

    def new_keys(i, c):
        r0 = pl.multiple_of(i * tkn, tkn)
        put_keys(P + r0, tkn, idx_keys(kv.new_rows(2, r0), None))
        return c
    lax.fori_loop(0, j, new_keys, 0)
    rd = pl.multiple_of(j * tkn, tkn)
    put_keys(P + rd, tkn, idx_keys(kv.new_rows(2, rd), _diag_valid(tkn, tq) if tkn > CHUNK else None))

    one, zero = jnp.int16(1), jnp.int16(0)

    def fold16(ind):
        parts = [ind[r:r + 16] for r in range(0, ind.shape[0], 16)]
        ways = max(1, min(len(parts), 512 // tq))
        accs = parts[:ways]
        for i in range(ways, len(parts)):
            accs[i % ways] = accs[i % ways] + parts[i]
        return functools.reduce(lambda x, y: x + y, accs)

    def total(parts):
        return jnp.sum(parts.astype(I32), axis=0, keepdims=True)

    def count(pred):
        def fn(r0, rows, c):
            ind = pred(lambda: hi_ref[pl.ds(r0, rows), :], lambda: lo_ref[pl.ds(r0, rows), :], r0, rows)
            return c + fold16(jnp.where(ind, one, zero))
        return total(for_tiles(fn, jnp.zeros((16, tq), I16)))

    def bisect16(pick, kth):
        def step(it, ans):
            cand = ans + jnp.left_shift(jnp.int32(1), 15 - it)
            c16 = cand.astype(I16)
            cnt = count(lambda hi, lo, r0, rows: pick(hi, lo) >= c16)
            return jnp.where(cnt >= kth, cand, ans)
        return lax.fori_loop(0, 16, step, jnp.full((1, tq), MIN16, I32))

    b = bisect16(lambda hi, lo: hi(), topk)
    b16 = b.astype(I16)

    def mask_lo(r0, rows, c):
        hi = hi_ref[pl.ds(r0, rows), :]
        lo_ref[pl.ds(r0, rows), :] = jnp.where(hi == b16, lo_ref[pl.ds(r0, rows), :], jnp.int16(MIN16))
        return c + fold16(jnp.where(hi > b16, one, zero))
    n_above = total(for_tiles(mask_lo, jnp.zeros((16, tq), I16)))
    kth_lo = topk - n_above
    cst = bisect16(lambda hi, lo: lo(), kth_lo)
    c16 = cst.astype(I16)

    def is_tie(hi, lo):
        return (hi == b16) & (lo == c16)
    n_gt = n_above + count(lambda hi, lo, r0, rows: lo() > c16)
    n_tie = count(lambda hi, lo, r0, rows: is_tie(hi(), lo()))
    need = (n_gt + n_tie > topk) & (b > MIN16)
    take = topk - n_gt

    @pl.when(jnp.max(jnp.where(need, 1, 0)) > 0)
    def _():
        take_f = jnp.where(need, take, 2 ** 30).astype(F32)
        tri = jnp.where(lax.broadcasted_iota(I32, (TIE_ROWS, TIE_ROWS), 0)
                        >= lax.broadcasted_iota(I32, (TIE_ROWS, TIE_ROWS), 1), 1.0, 0.0).astype(BF16)

        def demote(r0, rows, seen):
            for o in range(0, rows, TIE_ROWS):
                n = min(TIE_ROWS, rows - o)
                sl = pl.ds(r0 + o, n)
                hi, lo = hi_ref[sl, :], lo_ref[sl, :]
                tie = is_tie(hi, lo)
                ind = jnp.where(tie, jnp.bfloat16(1), jnp.bfloat16(0))
                rank = jnp.dot(tri[0:n, 0:n], ind, preferred_element_type=F32) + seen
                drop = tie & (jnp.where(rank > take_f, 1, 0).astype(I16) != 0)
                hi_ref[sl, :] = jnp.where(drop, jnp.int16(MIN16), hi)
                lo_ref[sl, :] = jnp.where(drop, jnp.int16(MIN16), lo)
                seen = rank[n - 1:n, :]
            return seen
        for_tiles(demote, jnp.zeros((1, tq), F32))

    whole = cst == MIN16
    bs16 = jnp.where(whole, jnp.maximum(b - 1, MIN16), b).astype(I16)
    cs16 = jnp.where(whole, 2 ** 15 - 1, cst - 1).astype(I16)

    _fill_block_diag(qbd_ref, qT_ref, nb, tqr, B_HEADS, B_HEAD_DIM, False)
    m_ref[...] = jnp.full(m_ref.shape, NEG, F32)
    acc_ref[...] = jnp.zeros(acc_ref.shape, F32)

    def attend(k_rows, vt_tiles, r0, rows):
        hi, lo = hi_ref[pl.ds(r0, rows), :], lo_ref[pl.ds(r0, rows), :]
        flag = jnp.where(hi > bs16, one, jnp.where(lo > cs16, one, zero))
        sel = flag.astype(I32) != 0
        s_all = jnp.dot(k_rows.astype(BF16), qbd_ref[...], preferred_element_type=F32)
        vts = [t.astype(BF16) for t in vt_tiles]
        ones = _ones_rows(rows)
        for h in range(B_HEADS):
            vt_h = jnp.concatenate([v[B_HEAD_DIM * h:B_HEAD_DIM * (h + 1), :] for v in vts] + [ones], axis=0)
            _softmax_step(s_all[:, h * tq:(h + 1) * tq], sel, vt_h, m_ref, acc_ref, h)

    if P > 0:
        def pa(t, z):
            r0 = pl.multiple_of(t * tkp, tkp)
            attend(kv.past_rows(0, r0), kv.past_vt(r0), r0, tkp)
            return z
        lax.fori_loop(0, P // tkp, pa, 0)

    def na(i, z):
        r0 = pl.multiple_of(i * tkn, tkn)
        attend(kv.new_rows(0, r0), kv.new_vt(r0), P + r0, tkn)
        return z
    lax.fori_loop(0, nnew, na, 0)

    nv = nb * B_HEAD_DIM
    oT = jnp.concatenate([_own_lanes(acc_ref[h, 0:nv, :] / acc_ref[h, nv:nv + 1, :], nb, B_HEAD_DIM, tqr)
                          for h in range(B_HEADS)], axis=0)
    y = oT.T * _silu(g_ref[...].reshape(tq, -1))
    y_ref[...] = y.reshape(y_ref.shape)


def _attn_specs(B, L, P, nb, tqr, hw, layer, new_fm, feat_rows):
    tq = nb * tqr
    if new_fm:
        feat = lambda n: pl.BlockSpec((None, n, tq), lambda b, j: (b, 0, j))
        seqblk = pl.BlockSpec((None, tqr, hw), lambda b, j: (b, j, 0))
        new_row = lambda c: pl.BlockSpec((None, L, c), lambda b, j: (b, 0, 0))
        new_col = lambda r: pl.BlockSpec((None, r, L), lambda b, j: (b, 0, 0))
    else:
        feat = lambda n: pl.BlockSpec((n, tq), lambda b, j: (0, b))
        seqblk = pl.BlockSpec((nb, tqr, hw), lambda b, j: (b, 0, 0))
        new_row = lambda c: pl.BlockSpec((nb, L, c), lambda b, j: (b, 0, 0))
        new_col = None
    past = lambda r: pl.BlockSpec((None, nb, r, P), lambda b, j: (layer, b, 0, 0))
    return feat, seqblk, new_row, new_col, past


def _dsa(qT, qiT, wT, gate, past, new, *, B, L, P, nb, tqr, layer, new_fm):
    hw = gate.shape[-1]
    tq = nb * tqr
    Lk = P + L
    topk = min(TOPK_MAX, Lk // 4)
    feat, seqblk, new_row, new_col, pastspec = _attn_specs(B, L, P, nb, tqr, hw, layer, new_fm, None)
    in_specs = [feat(hw), feat(hw), feat(IDX_HEADS), seqblk]
    args = [qT, qiT, wT, gate]
    if P > 0:
        in_specs += [pastspec(hw), pastspec(hw), pastspec(IDX_DIM)]
        args += list(past)
    in_specs += [new_row(hw), new_col(hw) if new_fm else new_row(hw), new_row(IDX_DIM)]
    args += list(new)
    return pl.pallas_call(
        functools.partial(_dsa_kernel, P=P, nb=nb, tqr=tqr, tkp=_past_tile(P, tq), tkn=tqr, topk=topk,
                          new_fm=new_fm),
        grid=(B // nb, L // tqr),
        in_specs=in_specs,
        out_specs=seqblk,
        out_shape=jax.ShapeDtypeStruct((B, L, hw), F32),
        scratch_shapes=[pltpu.VMEM((Lk, tq), I16), pltpu.VMEM((Lk, tq), I16),
                        pltpu.VMEM((nb * IDX_DIM, IDX_HEADS * tq), BF16),
                        pltpu.VMEM((nb * hw, B_HEADS * tq), BF16),
                        pltpu.VMEM((B_HEADS, 1, tq), F32),
                        pltpu.VMEM((B_HEADS, nb * B_HEAD_DIM + ONES_ROWS, tq), F32)],
        compiler_params=_cparams(("parallel", "arbitrary")),
    )(*args)


def _diff_kernel(*refs, P, nb, tqr, tkp, tkn, lam_init, new_fm):
    tq = nb * tqr
    it = iter(refs)
    qT_ref, g_ref, lq1_ref, lk1_ref, lq2_ref, lk2_ref, sub_ref = (next(it) for _ in range(7))
    past_refs = (next(it), next(it)) if P > 0 else None
    new_refs = (next(it), next(it))
    y_ref = next(it)
    qbd_ref, m_ref, acc_ref = (next(it) for _ in range(3))
    kv = _Keys(nb, tkp, tkn, new_fm, past_refs, new_refs)
    j = pl.program_id(1)
    nmaps = 2 * C_HEADS

    _fill_block_diag(qbd_ref, qT_ref, nb, tqr, nmaps, C_HALF, False)
    m_ref[...] = jnp.full(m_ref.shape, NEG, F32)
    acc_ref[...] = jnp.zeros(acc_ref.shape, F32)

    def attend(k_rows, vt_tiles, valid):
        s_all = jnp.dot(k_rows.astype(BF16), qbd_ref[...], preferred_element_type=F32)
        vts = [t.astype(BF16) for t in vt_tiles]
        ones = _ones_rows(k_rows.shape[0])
        for h in range(C_HEADS):
            vt_h = jnp.concatenate([v[C_HEAD_DIM * h:C_HEAD_DIM * (h + 1), :] for v in vts] + [ones], axis=0)
            for mi in (2 * h, 2 * h + 1):
                _softmax_step(s_all[:, mi * tq:(mi + 1) * tq], valid, vt_h, m_ref, acc_ref, mi)

    if P > 0:
        def pa(t, z):
            r0 = pl.multiple_of(t * tkp, tkp)
            attend(kv.past_rows(0, r0), kv.past_vt(r0), None)
            return z
        lax.fori_loop(0, P // tkp, pa, 0)

    def na(i, z):
        r0 = pl.multiple_of(i * tkn, tkn)
        attend(kv.new_rows(0, r0), kv.new_vt(r0), None)
        return z
    lax.fori_loop(0, j, na, 0)
    rd = pl.multiple_of(j * tkn, tkn)
    attend(kv.new_rows(0, rd), kv.new_vt(rd), _diag_valid(tkn, tq) if tkn > CHUNK else None)

    lam = (jnp.exp(jnp.sum(lq1_ref[...] * lk1_ref[...], axis=-1, keepdims=True))
           - jnp.exp(jnp.sum(lq2_ref[...] * lk2_ref[...], axis=-1, keepdims=True)) + lam_init)
    outs = []
    nv = nb * C_HEAD_DIM
    for h in range(C_HEADS):
        o = (acc_ref[2 * h, 0:nv, :] / acc_ref[2 * h, nv:nv + 1, :]
             - lam * (acc_ref[2 * h + 1, 0:nv, :] / acc_ref[2 * h + 1, nv:nv + 1, :]))
        o = _own_lanes(o, nb, C_HEAD_DIM, tqr)
        ms = jnp.mean(o * o, axis=0, keepdims=True)
        outs.append(o * lax.rsqrt(ms + EPS))
    o = jnp.concatenate(outs, axis=0).T
    y = (o * sub_ref[...]) * (1.0 - lam_init) * _silu(g_ref[...].reshape(tq, -1))
    y_ref[...] = y.reshape(y_ref.shape)


def _diff(qT, gate, lams, subrow, past, new, *, B, L, P, nb, tqr, layer, new_fm, lam_init):
    hw = gate.shape[-1]
    tq = nb * tqr
    nmaps = 2 * C_HEADS
    feat, seqblk, new_row, new_col, pastspec = _attn_specs(B, L, P, nb, tqr, hw, layer, new_fm, None)
    const = lambda s: pl.BlockSpec(s, lambda b, j: (0, 0))
    in_specs = [feat(hw), seqblk] + [const((1, C_HALF))] * 4 + [const((1, hw))]
    args = [qT, gate, *lams, subrow]
    if P > 0:
        in_specs += [pastspec(hw), pastspec(hw)]
        args += list(past)
    in_specs += [new_row(hw), new_col(hw) if new_fm else new_row(hw)]
    args += list(new)
    return pl.pallas_call(
        functools.partial(_diff_kernel, P=P, nb=nb, tqr=tqr, tkp=_past_tile(P, tq), tkn=tqr,
                          lam_init=lam_init, new_fm=new_fm),
        grid=(B // nb, L // tqr),
        in_specs=in_specs,
        out_specs=seqblk,
        out_shape=jax.ShapeDtypeStruct((B, L, hw), F32),
        scratch_shapes=[pltpu.VMEM((nb * hw, nmaps * tq), BF16),
                        pltpu.VMEM((nmaps, 1, tq), F32),
                        pltpu.VMEM((nmaps, nb * C_HEAD_DIM + ONES_ROWS, tq), F32)],
        compiler_params=_cparams(("parallel", "arbitrary")),
    )(*args)


def _outproj_kernel(ya_ref, yb_ref, yc_ref, x_ref, w_ref, g_ref, o_ref, *, aw, hw):
    mix = jnp.dot(ya_ref[...].astype(BF16), w_ref[0:aw, :], preferred_element_type=F32)
    mix = mix + jnp.dot(yb_ref[...].astype(BF16), w_ref[aw:aw + hw, :], preferred_element_type=F32)
    mix = mix + jnp.dot(yc_ref[...].astype(BF16), w_ref[aw + hw:aw + 2 * hw, :], preferred_element_type=F32)
    ms = jnp.mean(mix * mix, axis=-1, keepdims=True)
    o_ref[...] = x_ref[...] + mix * lax.rsqrt(ms + EPS) * g_ref[...]


def _outproj(ya, yb, yc, x2d, w, g, *, tm):
    T, D = x2d.shape
    aw, hw = D // 2, D // 4
    row = lambda n: pl.BlockSpec((tm, n), lambda i: (i, 0))
    const = lambda s: pl.BlockSpec(s, lambda i: (0, 0))
    return pl.pallas_call(
        functools.partial(_outproj_kernel, aw=aw, hw=hw),
        grid=(T // tm,),
        in_specs=[row(aw), row(hw), row(hw), row(D), const((D, D)), const((1, D))],
        out_specs=row(D),
        out_shape=jax.ShapeDtypeStruct((T, D), F32),
        compiler_params=_cparams(("parallel",)),
    )(ya, yb, yc, x2d, w, g)


_ROW_PROMPT = ("a_x", "a_g", "b_g", "c_g", "b_k", "c_k", "b_ik")
_FEAT_PROMPT = ("b_q", "b_iq", "c_q", "b_k", "b_v", "c_k", "c_v", "b_ik", "b_iw")
_ROW_SAMPLE = ("a_x", "a_g", "b_g", "c_g", "b_k", "b_v", "c_k", "c_v", "b_ik")
_FEAT_SAMPLE = ("b_q", "b_iq", "c_q", "b_ik", "b_iw")
_BF16_OUT = {"b_q", "b_iq", "c_q"}


def _pack_w_in(w_in, D, rows, feats, row_bf16):
    aw, hw = D // 2, D // 4
    sizes = (aw, aw, hw, hw, hw, hw, IDX_HEADS * IDX_DIM, IDX_DIM, IDX_HEADS, hw, hw, hw, hw)
    names = ("a_x", "a_g", "b_q", "b_k", "b_v", "b_g", "b_iq", "b_ik", "b_iw", "c_q", "c_k", "c_v", "c_g")
    scale = {"b_q": B_HEAD_DIM ** -0.5 * LOG2E, "c_q": C_HALF ** -0.5 * LOG2E,
             "b_iw": (IDX_DIM ** -0.5) * (IDX_HEADS ** -0.5)}
    seg, o = {}, 0
    for n, s in zip(names, sizes):
        seg[n] = w_in[..., o:o + s] * scale[n] if n in scale else w_in[..., o:o + s]
        o += s
    depth = w_in.shape[0]

    def cat(parts, mult):
        w = jnp.concatenate([seg[n] for n in parts], axis=-1)
        pad = (-w.shape[-1]) % mult
        return jnp.concatenate([w, jnp.zeros((depth, D, pad), w.dtype)], axis=-1) if pad else w
    wr = cat(rows, 128).astype(BF16)
    wf = jnp.swapaxes(cat(feats, 16), 1, 2).astype(BF16)
    row_plan = tuple((seg[n].shape[-1], BF16 if n in row_bf16 else F32) for n in rows)
    feat_plan = tuple((seg[n].shape[-1], BF16 if n in _BF16_OUT else F32) for n in feats)
    return wr, wf, row_plan, feat_plan


def _block_diag(w):
    depth, nb, n, _ = w.shape
    eye = jnp.eye(nb, dtype=w.dtype)
    return jnp.einsum("lnde,nm->lndme", w, eye).reshape(depth, nb * n, nb * n)


def _prompt_layer(x, B, L, wl, zero_buf, zero_h, lam_init):
    T, D = x.shape
    hw = D // 4
    tm = min(256, L)
    out = _inproj(x, wl["norm_pre"], wl["wr_p"], wl["wf_p"], tm=tm, row_plan=wl["plan_p"][0],
                  feat_plan=wl["plan_p"][1], feat_batch=(B, L))
    ax, ag, bg, cg, bk_bf, ck_bf, bik = out[:7]
    bqT, biqT, cqT, bkT, bvT, ckT, cvT, bikT, biwT = out[7:]
    r3 = lambda a: a.reshape(B, L, a.shape[-1])
    ya, nbuf, hlast = _amix(r3(ax), r3(ag), zero_buf, zero_h[:, None, :], wl["a_conv_w"], wl["a_conv_b"],
                            wl["rw"], wl["a_rg_b"], wl["iw"], wl["a_in_b"], wl["a_lambda"], tl=min(L, 512))
    tqr = min(512, L)
    kw = dict(B=B, L=L, P=0, nb=1, tqr=tqr, layer=0, new_fm=True)
    yb = _dsa(bqT, biqT, biwT, r3(bg), None, (r3(bk_bf), bvT, r3(bik)), **kw)
    yc = _diff(cqT, r3(cg), wl["lams"], wl["subrow"], None, (r3(ck_bf), cvT), lam_init=lam_init, **kw)
    x_new = _outproj(ya.reshape(T, -1), yb.reshape(T, hw), yc.reshape(T, hw), x, wl["w_out"], wl["norm_post"],
                     tm=tm)
    heads = lambda t, nh: t.reshape(B, nh, t.shape[1] // nh, L).transpose(0, 3, 1, 2)
    states = (nbuf, hlast[:, 0, :], heads(bkT, B_HEADS), heads(bvT, B_HEADS), bikT.transpose(0, 2, 1),
              heads(ckT, C_HEADS), heads(cvT, C_HEADS))
    return x_new, states


def _sample_layer(x, B, L, P, layer, wl, conv_buf, h0, past, lam_init):
    T, D = x.shape
    hw = D // 4
    tm = min(256, T)
    out = _inproj(x, wl["norm_pre"], wl["wr_s"], wl["wf_s"], tm=tm, row_plan=wl["plan_s"][0],
                  feat_plan=wl["plan_s"][1], feat_batch=None)
    ax, ag, bg, cg, bk, bv, ck, cv, bik = out[:9]
    bqT, biqT, cqT, bikT, biwT = out[9:]
    r3 = lambda a: a.reshape(B, L, a.shape[-1])
    ya, nbuf, hlast = _amix(r3(ax), r3(ag), conv_buf, h0[:, None, :], wl["a_conv_w"], wl["a_conv_b"],
                            wl["rw"], wl["a_rg_b"], wl["iw"], wl["a_in_b"], wl["a_lambda"], tl=L)
    nb = 2 if B % 2 == 0 else 1
    kw = dict(B=B, L=L, P=P, nb=nb, tqr=L, layer=layer, new_fm=False)
    pkT, pvT, pkiT, pckT, pcvT = past
    yb = _dsa(bqT, biqT, biwT, r3(bg), (pkT, pvT, pkiT), (r3(bk), r3(bv), r3(bik)), **kw)
    yc = _diff(cqT, r3(cg), wl["lams"], wl["subrow"], (pckT, pcvT), (r3(ck), r3(cv)), lam_init=lam_init, **kw)
    x_new = _outproj(ya.reshape(T, -1), yb.reshape(T, hw), yc.reshape(T, hw), x, wl["w_out"], wl["norm_post"],
                     tm=tm)
    states = (nbuf, hlast[:, 0, :],
              bk.reshape(B, L, B_HEADS, B_HEAD_DIM), bv.reshape(B, L, B_HEADS, B_HEAD_DIM),
              bikT.reshape(IDX_DIM, B, L).transpose(1, 2, 0),
              ck.reshape(B, L, C_HEADS, C_HEAD_DIM), cv.reshape(B, L, C_HEADS, C_HEAD_DIM))
    return x_new, states


def kernel(x_prompt, x_sample, cache_a_conv, state_a_h, cache_b_k, cache_b_v, cache_b_kidx, cache_c_k, cache_c_v, norm_pre, norm_post, w_in, w_out, a_conv_w, a_conv_b, a_rg_w, a_rg_b, a_in_w, a_in_b, a_lambda, c_lam_q1, c_lam_k1, c_lam_q2, c_lam_k2, c_subln):
    Bp, Lp, D = x_prompt.shape
    Bs, Ls, _ = x_sample.shape
    depth = w_in.shape[0]
    P = cache_b_k.shape[2]
    aw = D // 2

    wr_p, wf_p, *plan_p = _pack_w_in(w_in, D, _ROW_PROMPT, _FEAT_PROMPT, {"b_k", "c_k"})
    wr_s, wf_s, *plan_s = _pack_w_in(w_in, D, _ROW_SAMPLE, _FEAT_SAMPLE, set())
    rw = _block_diag(a_rg_w).astype(BF16)
    iw = _block_diag(a_in_w).astype(BF16)
    wo = w_out.astype(BF16)

    fm = lambda c: jnp.transpose(c, (0, 1, 3, 4, 2)).reshape(depth, Bs, -1, P)
    past = (fm(cache_b_k), fm(cache_b_v), jnp.transpose(cache_b_kidx, (0, 1, 3, 2)), fm(cache_c_k), fm(cache_c_v))

    xp = x_prompt.reshape(Bp * Lp, D)
    xs = x_sample.reshape(Bs * Ls, D)
    zero_buf = jnp.zeros((Bp, CONV_W - 1, aw), F32)
    zero_h = jnp.zeros((Bp, aw), F32)
    p_st, s_st = [], []
    for l in range(depth):
        row = lambda a: a[l][None, :]
        wl = {"norm_pre": row(norm_pre), "norm_post": row(norm_post), "w_out": wo[l],
              "wr_p": wr_p[l], "wf_p": wf_p[l], "plan_p": plan_p, "wr_s": wr_s[l], "wf_s": wf_s[l], "plan_s": plan_s,
              "a_conv_w": a_conv_w[l], "a_conv_b": row(a_conv_b), "rw": rw[l], "a_rg_b": row(a_rg_b),
              "iw": iw[l], "a_in_b": row(a_in_b), "a_lambda": row(a_lambda),
              "lams": (row(c_lam_q1), row(c_lam_k1), row(c_lam_q2), row(c_lam_k2)),
              "subrow": jnp.tile(c_subln[l], C_HEADS)[None, :]}
        lam_init = 0.8 - 0.6 * math.exp(-0.3 * l)
        xp, st_p = _prompt_layer(xp, Bp, Lp, wl, zero_buf, zero_h, lam_init)
        xs, st_s = _sample_layer(xs, Bs, Ls, P, l, wl, cache_a_conv[l], state_a_h[l], past, lam_init)
        p_st.append(st_p)
        s_st.append(st_s)

    stk = lambda states, i: jnp.stack([s[i] for s in states], axis=0)
    return (xp.reshape(Bp, Lp, D), xs.reshape(Bs, Ls, D),
            *[stk(p_st, i) for i in range(7)], *[stk(s_st, i) for i in range(7)])
```

```python
import functools
import math

import jax
import jax.numpy as jnp
from jax import lax
from jax.experimental import pallas as pl
from jax.experimental.pallas import tpu as pltpu

F32 = jnp.float32
BF16 = jnp.bfloat16
I32 = jnp.int32

CHUNK = 64
CONV_W = 4
LRU_C = 8.0
B_HEADS = 4
B_HEAD_DIM = 64
IDX_HEADS = 8
IDX_DIM = 32
TOPK_MAX = 256
C_HEADS = 4
C_HALF = 32
C_HEAD_DIM = 2 * C_HALF
EPS = 1e-6
NEG = -1e30
INT_MIN = -2 ** 31
VMEM_LIMIT_BYTES = 56 * 1024 * 1024
KEY_TILE = 256
TILE_ELEMS = 512 * 256
TIE_ROWS = 256
PROJ_ROWS = 512
ONES_ROWS = 16
I16 = jnp.int16
MIN16 = -2 ** 15
LOG2E = math.log2(math.e)


def _past_tile(P, tq):
    rows = max(KEY_TILE, TILE_ELEMS // tq)
    while P % rows:
        rows //= 2
    return rows


def _cparams(sem):
    return pltpu.CompilerParams(dimension_semantics=sem, vmem_limit_bytes=VMEM_LIMIT_BYTES)


def _silu(g):
    return g * jax.nn.sigmoid(g)


def _inproj_kernel(x_ref, g_ref, wr_ref, wf_ref, *out_refs, row_plan, feat_plan):
    x = x_ref[...]
    ms = jnp.mean(x * x, axis=-1, keepdims=True)
    xn = (x * lax.rsqrt(ms + EPS) * g_ref[...]).astype(BF16)
    proj = jnp.dot(xn, wr_ref[...], preferred_element_type=F32)
    projT = lax.dot_general(wf_ref[...], xn, (((1,), (1,)), ((), ())),
                            preferred_element_type=F32)
    refs = iter(out_refs)
    o = 0
    for n, dt in row_plan:
        ref = next(refs)
        ref[...] = proj[:, o:o + n].astype(dt)
        o += n
    o = 0
    for n, dt in feat_plan:
        ref = next(refs)
        ref[...] = projT[o:o + n, :].astype(dt)
        o += n


def _inproj(x2d, g, wr, wf, *, tm, row_plan, feat_plan, feat_batch):
    T, D = x2d.shape
    row = lambda n: pl.BlockSpec((tm, n), lambda i: (i, 0))
    const = lambda s: pl.BlockSpec(s, lambda i: (0, 0))
    out_shape = [jax.ShapeDtypeStruct((T, n), dt) for n, dt in row_plan]
    out_specs = [row(n) for n, _ in row_plan]
    if feat_batch is None:
        out_shape += [jax.ShapeDtypeStruct((n, T), dt) for n, dt in feat_plan]
        out_specs += [pl.BlockSpec((n, tm), lambda i: (0, i)) for n, _ in feat_plan]
    else:
        B, L = feat_batch
        npb = L // tm
        out_shape += [jax.ShapeDtypeStruct((B, n, L), dt) for n, dt in feat_plan]
        out_specs += [pl.BlockSpec((None, n, tm), lambda i: (i // npb, 0, i % npb)) for n, _ in feat_plan]
    return pl.pallas_call(
        functools.partial(_inproj_kernel, row_plan=row_plan, feat_plan=feat_plan),
        grid=(T // tm,),
        in_specs=[row(D), const((1, D)), const(wr.shape), const(wf.shape)],
        out_specs=out_specs, out_shape=out_shape,
        compiler_params=_cparams(("parallel",)),
    )(x2d, g, wr, wf)


def _amix_kernel(ax_ref, ag_ref, buf_ref, h0_ref, cw_ref, cb_ref, rw_ref, rb_ref, iw_ref, ib_ref, lam_ref,
                 ya_ref, nbuf_ref, hlast_ref, xbuf_ref, hc_ref, *, tl, nl, aw):
    li = pl.program_id(1)

    @pl.when(li == 0)
    def _():
        xbuf_ref[0:8, :] = jnp.zeros((8, aw), F32)
        xbuf_ref[8 - (CONV_W - 1):8, :] = buf_ref[...]
        hc_ref[...] = h0_ref[...]

    x = ax_ref[...]
    xbuf_ref[8:8 + tl, :] = x
    cw = cw_ref[...]
    conv = cb_ref[...] + x * cw[CONV_W - 1:CONV_W, :]
    for s in range(1, CONV_W):
        conv = conv + xbuf_ref[8 - s:8 - s + tl, :] * cw[CONV_W - 1 - s:CONV_W - s, :]

    cbf = conv.astype(BF16)
    r = jax.nn.sigmoid(jnp.dot(cbf, rw_ref[...], preferred_element_type=F32) + rb_ref[...])
    ig = jax.nn.sigmoid(jnp.dot(cbf, iw_ref[...], preferred_element_type=F32) + ib_ref[...])
    nl_lam = -lam_ref[...]
    sp = jnp.maximum(nl_lam, 0.0) + jnp.log1p(jnp.exp(-jnp.abs(nl_lam)))
    log_a = (-LRU_C) * r * sp
    a = jnp.exp(log_a)
    u = jnp.sqrt(1.0 - a * a) * (ig * conv)

    row = lax.broadcasted_iota(I32, (tl, aw), 0)
    s = 1
    while s < tl:
        if s % 8:
            keep = row >= s
            u = jnp.where(keep, a * pltpu.roll(u, s, 0) + u, u)
            a = jnp.where(keep, a * pltpu.roll(a, s, 0), a)
        else:
            u = jnp.concatenate([u[:s], a[s:] * u[:tl - s] + u[s:]], axis=0)
            a = jnp.concatenate([a[:s], a[s:] * a[:tl - s]], axis=0)
        s *= 2
    h = a * hc_ref[...] + u
    ya_ref[...] = h * _silu(ag_ref[...])
    hc_ref[...] = h[tl - 1:tl, :]
    xbuf_ref[0:8, :] = x[tl - 8:tl, :]

    @pl.when(li == nl - 1)
    def _():
        nbuf_ref[...] = xbuf_ref[8 + tl - (CONV_W - 1):8 + tl, :]
        hlast_ref[...] = h[tl - 1:tl, :]


def _amix(ax, ag, buf, h0, cw, cb, rw, rb, iw, ib, lam, *, tl):
    B, L, aw = ax.shape
    nl = L // tl
    seq = pl.BlockSpec((None, tl, aw), lambda b, l: (b, l, 0))
    perb = lambda r: pl.BlockSpec((None, r, aw), lambda b, l: (b, 0, 0))
    const = lambda s: pl.BlockSpec(s, lambda b, l: (0, 0))
    return pl.pallas_call(
        functools.partial(_amix_kernel, tl=tl, nl=nl, aw=aw),
        grid=(B, nl),
        in_specs=[seq, seq, perb(CONV_W - 1), perb(1), const((CONV_W, aw)), const((1, aw)),
                  const((aw, aw)), const((1, aw)), const((aw, aw)), const((1, aw)), const((1, aw))],
        out_specs=[seq, perb(CONV_W - 1), perb(1)],
        out_shape=[jax.ShapeDtypeStruct((B, L, aw), F32), jax.ShapeDtypeStruct((B, CONV_W - 1, aw), F32),
                   jax.ShapeDtypeStruct((B, 1, aw), F32)],
        scratch_shapes=[pltpu.VMEM((tl + 8, aw), F32), pltpu.VMEM((1, aw), F32)],
        compiler_params=_cparams(("parallel", "arbitrary")),
    )(ax, ag, buf, h0, cw, cb, rw, rb, iw, ib, lam)


def _col_reduce(x, op):
    rows, n = x.shape
    return op(op(x.reshape(rows // 8, 8, n), axis=0), axis=0, keepdims=True)


def _diag_valid(rows, tq):
    kc = lax.broadcasted_iota(I32, (rows, tq), 0) // CHUNK
    qc = lax.broadcasted_iota(I32, (rows, tq), 1) // CHUNK
    return kc <= qc


def _ones_rows(rows):
    return jnp.where(lax.broadcasted_iota(I32, (ONES_ROWS, rows), 0) == 0, 1.0, 0.0).astype(BF16)


def _softmax_step(s, pen, vt_h, m_ref, acc_ref, idx):
    m_old = m_ref[idx]
    sm = s if pen is None else s + pen
    m_new = jnp.maximum(m_old, _col_reduce(sm, jnp.max))
    alpha = jnp.exp2(m_old - m_new)
    p = jnp.exp2(sm - m_new).astype(BF16)
    pv = jnp.dot(vt_h, p, preferred_element_type=F32)
    acc_ref[idx, :, :] = alpha * acc_ref[idx, :, :] + pv
    m_ref[idx] = m_new


def _fill_block_diag(dst_ref, qT_ref, nb, tqr, nblk, blk, shared_rows):
    tq = nb * tqr
    dst_ref[...] = jnp.zeros(dst_ref.shape, dst_ref.dtype)
    for a in range(nb):
        for m in range(nblk):
            r = a * blk if shared_rows else (a * nblk + m) * blk
            dst_ref[r:r + blk, m * tq + a * tqr:m * tq + (a + 1) * tqr] = \
                qT_ref[blk * m:blk * (m + 1), a * tqr:(a + 1) * tqr]


def _own_lanes(full, nb, dh, tqr):
    if nb == 1:
        return full
    lane_seq = lax.broadcasted_iota(I32, (dh, nb * tqr), 1) // tqr
    out = full[0:dh, :]
    for a in range(1, nb):
        out = jnp.where(lane_seq == a, full[a * dh:(a + 1) * dh, :], out)
    return out


class _Keys:
    def __init__(self, nb, tkp, tkn, new_fm, past_refs, new_refs):
        self.nb, self.tkp, self.tkn, self.new_fm = nb, tkp, tkn, new_fm
        self.past_refs, self.new_refs = past_refs, new_refs

    def past_rows(self, which, r0):
        t = self.past_refs[which][:, :, pl.ds(r0, self.tkp)]
        return t.reshape(t.shape[0] * t.shape[1], t.shape[2]).T

    def past_vt(self, r0):
        return [self.past_refs[1][a, :, pl.ds(r0, self.tkp)] for a in range(self.nb)]

    def new_rows(self, which, r0):
        ref = self.new_refs[which]
        if self.new_fm:
            return ref[pl.ds(r0, self.tkn), :]
        return jnp.concatenate([ref[a] for a in range(self.nb)], axis=1)

    def new_vt(self, r0):
        ref = self.new_refs[1]
        if self.new_fm:
            return [ref[:, pl.ds(r0, self.tkn)]]
        return [ref[a].T for a in range(self.nb)]


def _dsa_kernel(*refs, P, nb, tqr, tkp, tkn, topk, new_fm):
    tq = nb * tqr
    it = iter(refs)
    qT_ref, qiT_ref, wT_ref, g_ref = next(it), next(it), next(it), next(it)
    past_refs = (next(it), next(it), next(it)) if P > 0 else None
    new_refs = (next(it), next(it), next(it))
    y_ref = next(it)
    hi_ref, lo_ref, qip_ref, qbd_ref, m_ref, acc_ref = (next(it) for _ in range(6))
    kv = _Keys(nb, tkp, tkn, new_fm, past_refs, new_refs)
    j = pl.program_id(1)
    nnew = j + 1

    _fill_block_diag(qip_ref, qiT_ref, nb, tqr, IDX_HEADS, IDX_DIM, True)
    w = wT_ref[...]

    def put_keys(r0, rows, key):
        hi_ref[pl.ds(r0, rows), :] = (key >> 16).astype(I16)
        lo_ref[pl.ds(r0, rows), :] = ((key & 0xFFFF) + MIN16).astype(I16)

    def idx_keys(ki_rows, valid):
        s_all = jnp.dot(ki_rows.astype(BF16), qip_ref[...], preferred_element_type=F32)
        acc = None
        for h in range(IDX_HEADS):
            t = jnp.maximum(s_all[:, h * tq:(h + 1) * tq], 0.0) * w[h:h + 1, :]
            acc = t if acc is None else acc + t
        bits = lax.bitcast_convert_type(acc, I32)
        sign = bits >> 31
        key = (bits ^ (sign & 0x7FFFFFFF)) - sign
        return key if valid is None else jnp.where(valid, key, INT_MIN)

    def for_tiles(fn, carry, n_new=None):
        if P > 0:
            carry = lax.fori_loop(0, P // tkp, lambda t, c: fn(pl.multiple_of(t * tkp, tkp), tkp, c), carry)
        return lax.fori_loop(0, nnew if n_new is None else n_new,
                             lambda i, c: fn(P + pl.multiple_of(i * tkn, tkn), tkn, c), carry)

    if P > 0:
        def past_keys(t, c):
            r0 = pl.multiple_of(t * tkp, tkp)
            put_keys(r0, tkp, idx_keys(kv.past_rows(2, r0), None))
            return c
        lax.fori_loop(0, P // tkp, past_keys, 0)

    def new_keys(i, c):
        r0 = pl.multiple_of(i * tkn, tkn)
        put_keys(P + r0, tkn, idx_keys(kv.new_rows(2, r0), None))
        return c
    lax.fori_loop(0, j, new_keys, 0)
    rd = pl.multiple_of(j * tkn, tkn)
    put_keys(P + rd, tkn, idx_keys(kv.new_rows(2, rd), _diag_valid(tkn, tq) if tkn > CHUNK else None))

    one, zero = jnp.int16(1), jnp.int16(0)

    def fold16(ind):
        parts = [ind[r:r + 16] for r in range(0, ind.shape[0], 16)]
        ways = max(1, min(len(parts), 512 // tq))
        accs = parts[:ways]
        for i in range(ways, len(parts)):
            accs[i % ways] = accs[i % ways] + parts[i]
        return functools.reduce(lambda x, y: x + y, accs)

    def total(parts):
        return jnp.sum(parts.astype(I32), axis=0, keepdims=True)

    def count(pred):
        def fn(r0, rows, c):
            ind = pred(lambda: hi_ref[pl.ds(r0, rows), :], lambda: lo_ref[pl.ds(r0, rows), :], r0, rows)
            return c + fold16(jnp.where(ind, one, zero))
        return total(for_tiles(fn, jnp.zeros((16, tq), I16)))

    def bisect16(pick, kth):
        def step(it, ans):
            cand = ans + jnp.left_shift(jnp.int32(1), 15 - it)
            c16 = cand.astype(I16)
            cnt = count(lambda hi, lo, r0, rows: pick(hi, lo) >= c16)
            return jnp.where(cnt >= kth, cand, ans)
        return lax.fori_loop(0, 16, step, jnp.full((1, tq), MIN16, I32))

    b = bisect16(lambda hi, lo: hi(), topk)
    b16 = b.astype(I16)

    def mask_lo(r0, rows, c):
        hi = hi_ref[pl.ds(r0, rows), :]
        lo_ref[pl.ds(r0, rows), :] = jnp.where(hi == b16, lo_ref[pl.ds(r0, rows), :], jnp.int16(MIN16))
        return c + fold16(jnp.where(hi > b16, one, zero))
    n_above = total(for_tiles(mask_lo, jnp.zeros((16, tq), I16)))
    kth_lo = topk - n_above
    cst = bisect16(lambda hi, lo: lo(), kth_lo)
    c16 = cst.astype(I16)

    def is_tie(hi, lo):
        return (hi == b16) & (lo == c16)
    n_gt = n_above + count(lambda hi, lo, r0, rows: lo() > c16)
    n_tie = count(lambda hi, lo, r0, rows: is_tie(hi(), lo()))
    need = (n_gt + n_tie > topk) & (b > MIN16)
    take = topk - n_gt

    @pl.when(jnp.max(jnp.where(need, 1, 0)) > 0)
    def _():
        take_f = jnp.where(need, take, 2 ** 30).astype(F32)
        tri = jnp.where(lax.broadcasted_iota(I32, (TIE_ROWS, TIE_ROWS), 0)
                        >= lax.broadcasted_iota(I32, (TIE_ROWS, TIE_ROWS), 1), 1.0, 0.0).astype(BF16)

        def demote(r0, rows, seen):
            blocks = []
            for o in range(0, rows, TIE_ROWS):
                n = min(TIE_ROWS, rows - o)
                sl = pl.ds(r0 + o, n)
                hi, lo = hi_ref[sl, :], lo_ref[sl, :]
                tie = is_tie(hi, lo)
                ind = jnp.where(tie, jnp.bfloat16(1), jnp.bfloat16(0))
                blocks.append((sl, n, hi, lo, tie, jnp.dot(tri[0:n, 0:n], ind, preferred_element_type=F32)))
            for sl, n, hi, lo, tie, within in blocks:
                rank = within + seen
                drop = tie & (jnp.where(rank > take_f, 1, 0).astype(I16) != 0)
                hi_ref[sl, :] = jnp.where(drop, jnp.int16(MIN16), hi)
                lo_ref[sl, :] = jnp.where(drop, jnp.int16(MIN16), lo)
                seen = rank[n - 1:n, :]
            return seen
        for_tiles(demote, jnp.zeros((1, tq), F32))

    whole = cst == MIN16
    bs16 = jnp.where(whole, jnp.maximum(b - 1, MIN16), b).astype(I16)
    cs16 = jnp.where(whole, 2 ** 15 - 1, cst - 1).astype(I16)
    open_, shut = jnp.bfloat16(0), jnp.bfloat16(NEG)

    _fill_block_diag(qbd_ref, qT_ref, nb, tqr, B_HEADS, B_HEAD_DIM, False)
    m_ref[...] = jnp.full(m_ref.shape, NEG, F32)
    acc_ref[...] = jnp.zeros(acc_ref.shape, F32)

    def attend(k_rows, vt_tiles, r0, rows):
        hi, lo = hi_ref[pl.ds(r0, rows), :], lo_ref[pl.ds(r0, rows), :]
        pen = jnp.where(hi > bs16, open_, jnp.where(lo > cs16, open_, shut)).astype(F32)
        s_all = jnp.dot(k_rows.astype(BF16), qbd_ref[...], preferred_element_type=F32)
        vts = [t.astype(BF16) for t in vt_tiles]
        ones = _ones_rows(rows)
        for h in range(B_HEADS):
            vt_h = jnp.concatenate([v[B_HEAD_DIM * h:B_HEAD_DIM * (h + 1), :] for v in vts] + [ones], axis=0)
            _softmax_step(s_all[:, h * tq:(h + 1) * tq], pen, vt_h, m_ref, acc_ref, h)

    if P > 0:
        def pa(t, z):
            r0 = pl.multiple_of(t * tkp, tkp)
            attend(kv.past_rows(0, r0), kv.past_vt(r0), r0, tkp)
            return z
        lax.fori_loop(0, P // tkp, pa, 0)

    def na(i, z):
        r0 = pl.multiple_of(i * tkn, tkn)
        attend(kv.new_rows(0, r0), kv.new_vt(r0), P + r0, tkn)
        return z
    lax.fori_loop(0, nnew, na, 0)

    nv = nb * B_HEAD_DIM
    oT = jnp.concatenate([_own_lanes(acc_ref[h, 0:nv, :] / acc_ref[h, nv:nv + 1, :], nb, B_HEAD_DIM, tqr)
                          for h in range(B_HEADS)], axis=0)
    y = oT.T * _silu(g_ref[...].reshape(tq, -1))
    y_ref[...] = y.reshape(y_ref.shape)


def _attn_specs(B, L, P, nb, tqr, hw, layer, new_fm, feat_rows):
    tq = nb * tqr
    if new_fm:
        feat = lambda n: pl.BlockSpec((None, n, tq), lambda b, j: (b, 0, j))
        seqblk = pl.BlockSpec((None, tqr, hw), lambda b, j: (b, j, 0))
        new_row = lambda c: pl.BlockSpec((None, L, c), lambda b, j: (b, 0, 0))
        new_col = lambda r: pl.BlockSpec((None, r, L), lambda b, j: (b, 0, 0))
    else:
        feat = lambda n: pl.BlockSpec((n, tq), lambda b, j: (0, b))
        seqblk = pl.BlockSpec((nb, tqr, hw), lambda b, j: (b, 0, 0))
        new_row = lambda c: pl.BlockSpec((nb, L, c), lambda b, j: (b, 0, 0))
        new_col = None
    past = lambda r: pl.BlockSpec((None, nb, r, P), lambda b, j: (layer, b, 0, 0))
    return feat, seqblk, new_row, new_col, past


def _dsa(qT, qiT, wT, gate, past, new, *, B, L, P, nb, tqr, layer, new_fm):
    hw = gate.shape[-1]
    tq = nb * tqr
    Lk = P + L
    topk = min(TOPK_MAX, Lk // 4)
    feat, seqblk, new_row, new_col, pastspec = _attn_specs(B, L, P, nb, tqr, hw, layer, new_fm, None)
    in_specs = [feat(hw), feat(hw), feat(IDX_HEADS), seqblk]
    args = [qT, qiT, wT, gate]
    if P > 0:
        in_specs += [pastspec(hw), pastspec(hw), pastspec(IDX_DIM)]
        args += list(past)
    in_specs += [new_row(hw), new_col(hw) if new_fm else new_row(hw), new_row(IDX_DIM)]
    args += list(new)
    return pl.pallas_call(
        functools.partial(_dsa_kernel, P=P, nb=nb, tqr=tqr, tkp=_past_tile(P, tq), tkn=tqr, topk=topk,
                          new_fm=new_fm),
        grid=(B // nb, L // tqr),
        in_specs=in_specs,
        out_specs=seqblk,
        out_shape=jax.ShapeDtypeStruct((B, L, hw), F32),
        scratch_shapes=[pltpu.VMEM((Lk, tq), I16), pltpu.VMEM((Lk, tq), I16),
                        pltpu.VMEM((nb * IDX_DIM, IDX_HEADS * tq), BF16),
                        pltpu.VMEM((nb * hw, B_HEADS * tq), BF16),
                        pltpu.VMEM((B_HEADS, 1, tq), F32),
                        pltpu.VMEM((B_HEADS, nb * B_HEAD_DIM + ONES_ROWS, tq), F32)],
        compiler_params=_cparams(("parallel", "arbitrary")),
    )(*args)


def _diff_kernel(*refs, P, nb, tqr, tkp, tkn, lam_init, new_fm):
    tq = nb * tqr
    it = iter(refs)
    qT_ref, g_ref, lq1_ref, lk1_ref, lq2_ref, lk2_ref, sub_ref = (next(it) for _ in range(7))
    past_refs = (next(it), next(it)) if P > 0 else None
    new_refs = (next(it), next(it))
    y_ref = next(it)
    qbd_ref, m_ref, acc_ref = (next(it) for _ in range(3))
    kv = _Keys(nb, tkp, tkn, new_fm, past_refs, new_refs)
    j = pl.program_id(1)
    nmaps = 2 * C_HEADS

    _fill_block_diag(qbd_ref, qT_ref, nb, tqr, nmaps, C_HALF, False)
    m_ref[...] = jnp.full(m_ref.shape, NEG, F32)
    acc_ref[...] = jnp.zeros(acc_ref.shape, F32)

    def attend(k_rows, vt_tiles, pen):
        s_all = jnp.dot(k_rows.astype(BF16), qbd_ref[...], preferred_element_type=F32)
        vts = [t.astype(BF16) for t in vt_tiles]
        ones = _ones_rows(k_rows.shape[0])
        for h in range(C_HEADS):
            vt_h = jnp.concatenate([v[C_HEAD_DIM * h:C_HEAD_DIM * (h + 1), :] for v in vts] + [ones], axis=0)
            for mi in (2 * h, 2 * h + 1):
                _softmax_step(s_all[:, mi * tq:(mi + 1) * tq], pen, vt_h, m_ref, acc_ref, mi)

    if P > 0:
        def pa(t, z):
            r0 = pl.multiple_of(t * tkp, tkp)
            attend(kv.past_rows(0, r0), kv.past_vt(r0), None)
            return z
        lax.fori_loop(0, P // tkp, pa, 0)

    def na(i, z):
        r0 = pl.multiple_of(i * tkn, tkn)
        attend(kv.new_rows(0, r0), kv.new_vt(r0), None)
        return z
    lax.fori_loop(0, j, na, 0)
    rd = pl.multiple_of(j * tkn, tkn)
    attend(kv.new_rows(0, rd), kv.new_vt(rd),
           jnp.where(_diag_valid(tkn, tq), 0.0, NEG) if tkn > CHUNK else None)

    lam = (jnp.exp(jnp.sum(lq1_ref[...] * lk1_ref[...], axis=-1, keepdims=True))
           - jnp.exp(jnp.sum(lq2_ref[...] * lk2_ref[...], axis=-1, keepdims=True)) + lam_init)
    outs = []
    nv = nb * C_HEAD_DIM
    for h in range(C_HEADS):
        o = (acc_ref[2 * h, 0:nv, :] / acc_ref[2 * h, nv:nv + 1, :]
             - lam * (acc_ref[2 * h + 1, 0:nv, :] / acc_ref[2 * h + 1, nv:nv + 1, :]))
        o = _own_lanes(o, nb, C_HEAD_DIM, tqr)
        ms = jnp.mean(o * o, axis=0, keepdims=True)
        outs.append(o * lax.rsqrt(ms + EPS))
    o = jnp.concatenate(outs, axis=0).T
    y = (o * sub_ref[...]) * (1.0 - lam_init) * _silu(g_ref[...].reshape(tq, -1))
    y_ref[...] = y.reshape(y_ref.shape)


def _diff(qT, gate, lams, subrow, past, new, *, B, L, P, nb, tqr, layer, new_fm, lam_init):
    hw = gate.shape[-1]
    tq = nb * tqr
    nmaps = 2 * C_HEADS
    feat, seqblk, new_row, new_col, pastspec = _attn_specs(B, L, P, nb, tqr, hw, layer, new_fm, None)
    const = lambda s: pl.BlockSpec(s, lambda b, j: (0, 0))
    in_specs = [feat(hw), seqblk] + [const((1, C_HALF))] * 4 + [const((1, hw))]
    args = [qT, gate, *lams, subrow]
    if P > 0:
        in_specs += [pastspec(hw), pastspec(hw)]
        args += list(past)
    in_specs += [new_row(hw), new_col(hw) if new_fm else new_row(hw)]
    args += list(new)
    return pl.pallas_call(
        functools.partial(_diff_kernel, P=P, nb=nb, tqr=tqr, tkp=_past_tile(P, tq), tkn=tqr,
                          lam_init=lam_init, new_fm=new_fm),
        grid=(B // nb, L // tqr),
        in_specs=in_specs,
        out_specs=seqblk,
        out_shape=jax.ShapeDtypeStruct((B, L, hw), F32),
        scratch_shapes=[pltpu.VMEM((nb * hw, nmaps * tq), BF16),
                        pltpu.VMEM((nmaps, 1, tq), F32),
                        pltpu.VMEM((nmaps, nb * C_HEAD_DIM + ONES_ROWS, tq), F32)],
        compiler_params=_cparams(("parallel", "arbitrary")),
    )(*args)


def _outproj_kernel(ya_ref, yb_ref, yc_ref, x_ref, w_ref, g_ref, o_ref, *, aw, hw):
    mix = jnp.dot(ya_ref[...].astype(BF16), w_ref[0:aw, :], preferred_element_type=F32)
    mix = mix + jnp.dot(yb_ref[...].astype(BF16), w_ref[aw:aw + hw, :], preferred_element_type=F32)
    mix = mix + jnp.dot(yc_ref[...].astype(BF16), w_ref[aw + hw:aw + 2 * hw, :], preferred_element_type=F32)
    ms = jnp.mean(mix * mix, axis=-1, keepdims=True)
    o_ref[...] = x_ref[...] + mix * lax.rsqrt(ms + EPS) * g_ref[...]


def _outproj(ya, yb, yc, x2d, w, g, *, tm):
    T, D = x2d.shape
    aw, hw = D // 2, D // 4
    row = lambda n: pl.BlockSpec((tm, n), lambda i: (i, 0))
    const = lambda s: pl.BlockSpec(s, lambda i: (0, 0))
    return pl.pallas_call(
        functools.partial(_outproj_kernel, aw=aw, hw=hw),
        grid=(T // tm,),
        in_specs=[row(aw), row(hw), row(hw), row(D), const((D, D)), const((1, D))],
        out_specs=row(D),
        out_shape=jax.ShapeDtypeStruct((T, D), F32),
        compiler_params=_cparams(("parallel",)),
    )(ya, yb, yc, x2d, w, g)


_ROW_PROMPT = ("a_x", "a_g", "b_g", "c_g", "b_k", "c_k", "b_ik")
_FEAT_PROMPT = ("b_q", "b_iq", "c_q", "b_k", "b_v", "c_k", "c_v", "b_ik", "b_iw")
_ROW_SAMPLE = ("a_x", "a_g", "b_g", "c_g", "b_k", "b_v", "c_k", "c_v", "b_ik")
_FEAT_SAMPLE = ("b_q", "b_iq", "c_q", "b_ik", "b_iw")
_BF16_OUT = {"b_q", "b_iq", "c_q"}


def _pack_w_in(w_in, D, rows, feats, row_bf16):
    aw, hw = D // 2, D // 4
    sizes = (aw, aw, hw, hw, hw, hw, IDX_HEADS * IDX_DIM, IDX_DIM, IDX_HEADS, hw, hw, hw, hw)
    names = ("a_x", "a_g", "b_q", "b_k", "b_v", "b_g", "b_iq", "b_ik", "b_iw", "c_q", "c_k", "c_v", "c_g")
    scale = {"b_q": B_HEAD_DIM ** -0.5 * LOG2E, "c_q": C_HALF ** -0.5 * LOG2E,
             "b_iw": (IDX_DIM ** -0.5) * (IDX_HEADS ** -0.5)}
    seg, o = {}, 0
    for n, s in zip(names, sizes):
        seg[n] = w_in[..., o:o + s] * scale[n] if n in scale else w_in[..., o:o + s]
        o += s
    depth = w_in.shape[0]

    def cat(parts, mult):
        w = jnp.concatenate([seg[n] for n in parts], axis=-1)
        pad = (-w.shape[-1]) % mult
        return jnp.concatenate([w, jnp.zeros((depth, D, pad), w.dtype)], axis=-1) if pad else w
    wr = cat(rows, 128).astype(BF16)
    wf = jnp.swapaxes(cat(feats, 16), 1, 2).astype(BF16)
    row_plan = tuple((seg[n].shape[-1], BF16 if n in row_bf16 else F32) for n in rows)
    feat_plan = tuple((seg[n].shape[-1], BF16 if n in _BF16_OUT else F32) for n in feats)
    return wr, wf, row_plan, feat_plan


def _block_diag(w):
    depth, nb, n, _ = w.shape
    eye = jnp.eye(nb, dtype=w.dtype)
    return jnp.einsum("lnde,nm->lndme", w, eye).reshape(depth, nb * n, nb * n)


def _prompt_layer(x, B, L, wl, zero_buf, zero_h, lam_init):
    T, D = x.shape
    hw = D // 4
    tm = min(PROJ_ROWS, L)
    out = _inproj(x, wl["norm_pre"], wl["wr_p"], wl["wf_p"], tm=tm, row_plan=wl["plan_p"][0],
                  feat_plan=wl["plan_p"][1], feat_batch=(B, L))
    ax, ag, bg, cg, bk_bf, ck_bf, bik = out[:7]
    bqT, biqT, cqT, bkT, bvT, ckT, cvT, bikT, biwT = out[7:]
    r3 = lambda a: a.reshape(B, L, a.shape[-1])
    ya, nbuf, hlast = _amix(r3(ax), r3(ag), zero_buf, zero_h[:, None, :], wl["a_conv_w"], wl["a_conv_b"],
                            wl["rw"], wl["a_rg_b"], wl["iw"], wl["a_in_b"], wl["a_lambda"], tl=min(L, 512))
    tqr = min(512, L)
    kw = dict(B=B, L=L, P=0, nb=1, tqr=tqr, layer=0, new_fm=True)
    yb = _dsa(bqT, biqT, biwT, r3(bg), None, (r3(bk_bf), bvT, r3(bik)), **kw)
    yc = _diff(cqT, r3(cg), wl["lams"], wl["subrow"], None, (r3(ck_bf), cvT), lam_init=lam_init, **kw)
    x_new = _outproj(ya.reshape(T, -1), yb.reshape(T, hw), yc.reshape(T, hw), x, wl["w_out"], wl["norm_post"],
                     tm=tm)
    heads = lambda t, nh: t.reshape(B, nh, t.shape[1] // nh, L).transpose(0, 3, 1, 2)
    states = (nbuf, hlast[:, 0, :], heads(bkT, B_HEADS), heads(bvT, B_HEADS), bikT.transpose(0, 2, 1),
              heads(ckT, C_HEADS), heads(cvT, C_HEADS))
    return x_new, states


def _sample_layer(x, B, L, P, layer, wl, conv_buf, h0, past, lam_init):
    T, D = x.shape
    hw = D // 4
    tm = min(PROJ_ROWS, T)
    out = _inproj(x, wl["norm_pre"], wl["wr_s"], wl["wf_s"], tm=tm, row_plan=wl["plan_s"][0],
                  feat_plan=wl["plan_s"][1], feat_batch=None)
    ax, ag, bg, cg, bk, bv, ck, cv, bik = out[:9]
    bqT, biqT, cqT, bikT, biwT = out[9:]
    r3 = lambda a: a.reshape(B, L, a.shape[-1])
    ya, nbuf, hlast = _amix(r3(ax), r3(ag), conv_buf, h0[:, None, :], wl["a_conv_w"], wl["a_conv_b"],
                            wl["rw"], wl["a_rg_b"], wl["iw"], wl["a_in_b"], wl["a_lambda"], tl=L)
    nb = 2 if B % 2 == 0 else 1
    kw = dict(B=B, L=L, P=P, nb=nb, tqr=L, layer=layer, new_fm=False)
    pkT, pvT, pkiT, pckT, pcvT = past
    yb = _dsa(bqT, biqT, biwT, r3(bg), (pkT, pvT, pkiT), (r3(bk), r3(bv), r3(bik)), **kw)
    yc = _diff(cqT, r3(cg), wl["lams"], wl["subrow"], (pckT, pcvT), (r3(ck), r3(cv)), lam_init=lam_init, **kw)
    x_new = _outproj(ya.reshape(T, -1), yb.reshape(T, hw), yc.reshape(T, hw), x, wl["w_out"], wl["norm_post"],
                     tm=tm)
    states = (nbuf, hlast[:, 0, :],
              bk.reshape(B, L, B_HEADS, B_HEAD_DIM), bv.reshape(B, L, B_HEADS, B_HEAD_DIM),
              bikT.reshape(IDX_DIM, B, L).transpose(1, 2, 0),
              ck.reshape(B, L, C_HEADS, C_HEAD_DIM), cv.reshape(B, L, C_HEADS, C_HEAD_DIM))
    return x_new, states


def kernel(x_prompt, x_sample, cache_a_conv, state_a_h, cache_b_k, cache_b_v, cache_b_kidx, cache_c_k, cache_c_v, norm_pre, norm_post, w_in, w_out, a_conv_w, a_conv_b, a_rg_w, a_rg_b, a_in_w, a_in_b, a_lambda, c_lam_q1, c_lam_k1, c_lam_q2, c_lam_k2, c_subln):
    Bp, Lp, D = x_prompt.shape
    Bs, Ls, _ = x_sample.shape
    depth = w_in.shape[0]
    P = cache_b_k.shape[2]
    aw = D // 2

    wr_p, wf_p, *plan_p = _pack_w_in(w_in, D, _ROW_PROMPT, _FEAT_PROMPT, {"b_k", "c_k"})
    wr_s, wf_s, *plan_s = _pack_w_in(w_in, D, _ROW_SAMPLE, _FEAT_SAMPLE, set())
    rw = _block_diag(a_rg_w).astype(BF16)
    iw = _block_diag(a_in_w).astype(BF16)
    wo = w_out.astype(BF16)

    fm = lambda c: jnp.transpose(c, (0, 1, 3, 4, 2)).reshape(depth, Bs, -1, P)
    past = (fm(cache_b_k), fm(cache_b_v), jnp.transpose(cache_b_kidx, (0, 1, 3, 2)), fm(cache_c_k), fm(cache_c_v))

    xp = x_prompt.reshape(Bp * Lp, D)
    xs = x_sample.reshape(Bs * Ls, D)
    zero_buf = jnp.zeros((Bp, CONV_W - 1, aw), F32)
    zero_h = jnp.zeros((Bp, aw), F32)
    p_st, s_st = [], []
    for l in range(depth):
        row = lambda a: a[l][None, :]
        wl = {"norm_pre": row(norm_pre), "norm_post": row(norm_post), "w_out": wo[l],
              "wr_p": wr_p[l], "wf_p": wf_p[l], "plan_p": plan_p, "wr_s": wr_s[l], "wf_s": wf_s[l], "plan_s": plan_s,
              "a_conv_w": a_conv_w[l], "a_conv_b": row(a_conv_b), "rw": rw[l], "a_rg_b": row(a_rg_b),
              "iw": iw[l], "a_in_b": row(a_in_b), "a_lambda": row(a_lambda),
              "lams": (row(c_lam_q1), row(c_lam_k1), row(c_lam_q2), row(c_lam_k2)),
              "subrow": jnp.tile(c_subln[l], C_HEADS)[None, :]}
        lam_init = 0.8 - 0.6 * math.exp(-0.3 * l)
        xp, st_p = _prompt_layer(xp, Bp, Lp, wl, zero_buf, zero_h, lam_init)
        xs, st_s = _sample_layer(xs, Bs, Ls, P, l, wl, cache_a_conv[l], state_a_h[l], past, lam_init)
        p_st.append(st_p)
        s_st.append(st_s)

    stk = lambda states, i: jnp.stack([s[i] for s in states], axis=0)
    return (xp.reshape(Bp, Lp, D), xs.reshape(Bs, Ls, D),
            *[stk(p_st, i) for i in range(7)], *[stk(s_st, i) for i in range(7)])
```

```python
import functools
import math

import jax
import jax.numpy as jnp
from jax import lax
from jax.experimental import pallas as pl
from jax.experimental.pallas import tpu as pltpu

F32 = jnp.float32
BF16 = jnp.bfloat16
I32 = jnp.int32

CHUNK = 64
CONV_W = 4
LRU_C = 8.0
B_HEADS = 4
B_HEAD_DIM = 64
IDX_HEADS = 8
IDX_DIM = 32
TOPK_MAX = 256
C_HEADS = 4
C_HALF = 32
C_HEAD_DIM = 2 * C_HALF
EPS = 1e-6
NEG = -1e30
INT_MIN = -2 ** 31
VMEM_LIMIT_BYTES = 56 * 1024 * 1024
KEY_TILE = 256
TILE_ELEMS = 512 * 256
TIE_ROWS = 256
PROJ_ROWS = 512
ONES_ROWS = 16
I16 = jnp.int16
MIN16 = -2 ** 15
LOG2E = math.log2(math.e)


def _past_tile(P, tq):
    rows = max(KEY_TILE, TILE_ELEMS // tq)
    while P % rows:
        rows //= 2
    return rows


def _cparams(sem):
    return pltpu.CompilerParams(dimension_semantics=sem, vmem_limit_bytes=VMEM_LIMIT_BYTES)


def _silu(g):
    return g * jax.nn.sigmoid(g)


def _inproj_kernel(x_ref, g_ref, wr_ref, wf_ref, *out_refs, row_plan, feat_plan):
    x = x_ref[...]
    ms = jnp.mean(x * x, axis=-1, keepdims=True)
    xn = (x * lax.rsqrt(ms + EPS) * g_ref[...]).astype(BF16)
    proj = jnp.dot(xn, wr_ref[...], preferred_element_type=F32)
    projT = lax.dot_general(wf_ref[...], xn, (((1,), (1,)), ((), ())),
                            preferred_element_type=F32)
    refs = iter(out_refs)
    o = 0
    for n, dt in row_plan:
        ref = next(refs)
        ref[...] = proj[:, o:o + n].astype(dt)
        o += n
    o = 0
    for n, dt in feat_plan:
        ref = next(refs)
        ref[...] = projT[o:o + n, :].astype(dt)
        o += n


def _inproj(x2d, g, wr, wf, *, tm, row_plan, feat_plan, feat_batch):
    T, D = x2d.shape
    row = lambda n: pl.BlockSpec((tm, n), lambda i: (i, 0))
    const = lambda s: pl.BlockSpec(s, lambda i: (0, 0))
    out_shape = [jax.ShapeDtypeStruct((T, n), dt) for n, dt in row_plan]
    out_specs = [row(n) for n, _ in row_plan]
    if feat_batch is None:
        out_shape += [jax.ShapeDtypeStruct((n, T), dt) for n, dt in feat_plan]
        out_specs += [pl.BlockSpec((n, tm), lambda i: (0, i)) for n, _ in feat_plan]
    else:
        B, L = feat_batch
        npb = L // tm
        out_shape += [jax.ShapeDtypeStruct((B, n, L), dt) for n, dt in feat_plan]
        out_specs += [pl.BlockSpec((None, n, tm), lambda i: (i // npb, 0, i % npb)) for n, _ in feat_plan]
    return pl.pallas_call(
        functools.partial(_inproj_kernel, row_plan=row_plan, feat_plan=feat_plan),
        grid=(T // tm,),
        in_specs=[row(D), const((1, D)), const(wr.shape), const(wf.shape)],
        out_specs=out_specs, out_shape=out_shape,
        compiler_params=_cparams(("parallel",)),
    )(x2d, g, wr, wf)


def _amix_kernel(ax_ref, ag_ref, buf_ref, h0_ref, cw_ref, cb_ref, rw_ref, rb_ref, iw_ref, ib_ref, lam_ref,
                 ya_ref, nbuf_ref, hlast_ref, xbuf_ref, hc_ref, *, tl, nl, aw):
    li = pl.program_id(1)

    @pl.when(li == 0)
    def _():
        xbuf_ref[0:8, :] = jnp.zeros((8, aw), F32)
        xbuf_ref[8 - (CONV_W - 1):8, :] = buf_ref[...]
        hc_ref[...] = h0_ref[...]

    x = ax_ref[...]
    xbuf_ref[8:8 + tl, :] = x
    cw = cw_ref[...]
    conv = cb_ref[...] + x * cw[CONV_W - 1:CONV_W, :]
    for s in range(1, CONV_W):
        conv = conv + xbuf_ref[8 - s:8 - s + tl, :] * cw[CONV_W - 1 - s:CONV_W - s, :]

    cbf = conv.astype(BF16)
    r = jax.nn.sigmoid(jnp.dot(cbf, rw_ref[...], preferred_element_type=F32) + rb_ref[...])
    ig = jax.nn.sigmoid(jnp.dot(cbf, iw_ref[...], preferred_element_type=F32) + ib_ref[...])
    nl_lam = -lam_ref[...]
    sp = jnp.maximum(nl_lam, 0.0) + jnp.log1p(jnp.exp(-jnp.abs(nl_lam)))
    log_a = (-LRU_C) * r * sp
    a = jnp.exp(log_a)
    u = jnp.sqrt(1.0 - a * a) * (ig * conv)

    row = lax.broadcasted_iota(I32, (tl, aw), 0)
    s = 1
    while s < tl:
        if s % 8:
            keep = row >= s
            u = jnp.where(keep, a * pltpu.roll(u, s, 0) + u, u)
            a = jnp.where(keep, a * pltpu.roll(a, s, 0), a)
        else:
            u = jnp.concatenate([u[:s], a[s:] * u[:tl - s] + u[s:]], axis=0)
            a = jnp.concatenate([a[:s], a[s:] * a[:tl - s]], axis=0)
        s *= 2
    h = a * hc_ref[...] + u
    ya_ref[...] = h * _silu(ag_ref[...])
    hc_ref[...] = h[tl - 1:tl, :]
    xbuf_ref[0:8, :] = x[tl - 8:tl, :]

    @pl.when(li == nl - 1)
    def _():
        nbuf_ref[...] = xbuf_ref[8 + tl - (CONV_W - 1):8 + tl, :]
        hlast_ref[...] = h[tl - 1:tl, :]


def _amix(ax, ag, buf, h0, cw, cb, rw, rb, iw, ib, lam, *, tl):
    B, L, aw = ax.shape
    nl = L // tl
    seq = pl.BlockSpec((None, tl, aw), lambda b, l: (b, l, 0))
    perb = lambda r: pl.BlockSpec((None, r, aw), lambda b, l: (b, 0, 0))
    const = lambda s: pl.BlockSpec(s, lambda b, l: (0, 0))
    return pl.pallas_call(
        functools.partial(_amix_kernel, tl=tl, nl=nl, aw=aw),
        grid=(B, nl),
        in_specs=[seq, seq, perb(CONV_W - 1), perb(1), const((CONV_W, aw)), const((1, aw)),
                  const((aw, aw)), const((1, aw)), const((aw, aw)), const((1, aw)), const((1, aw))],
        out_specs=[seq, perb(CONV_W - 1), perb(1)],
        out_shape=[jax.ShapeDtypeStruct((B, L, aw), F32), jax.ShapeDtypeStruct((B, CONV_W - 1, aw), F32),
                   jax.ShapeDtypeStruct((B, 1, aw), F32)],
        scratch_shapes=[pltpu.VMEM((tl + 8, aw), F32), pltpu.VMEM((1, aw), F32)],
        compiler_params=_cparams(("parallel", "arbitrary")),
    )(ax, ag, buf, h0, cw, cb, rw, rb, iw, ib, lam)


def _col_reduce(x, op):
    rows, n = x.shape
    return op(op(x.reshape(rows // 8, 8, n), axis=0), axis=0, keepdims=True)


def _diag_valid(rows, tq):
    kc = lax.broadcasted_iota(I32, (rows, tq), 0) // CHUNK
    qc = lax.broadcasted_iota(I32, (rows, tq), 1) // CHUNK
    return kc <= qc


def _ones_rows(rows):
    return jnp.where(lax.broadcasted_iota(I32, (ONES_ROWS, rows), 0) == 0, 1.0, 0.0).astype(BF16)


def _softmax_step(s, pen, vt_h, m_ref, acc_ref, idx):
    m_old = m_ref[idx]
    sm = s if pen is None else s + pen
    m_new = jnp.maximum(m_old, _col_reduce(sm, jnp.max))
    alpha = jnp.exp2(m_old - m_new)
    p = jnp.exp2(sm - m_new).astype(BF16)
    pv = jnp.dot(vt_h, p, preferred_element_type=F32)
    acc_ref[idx, :, :] = alpha * acc_ref[idx, :, :] + pv
    m_ref[idx] = m_new


def _key_dot(keys, rhs, keys_fm):
    keys = keys.astype(BF16)
    if keys_fm:
        return lax.dot_general(keys, rhs, (((0,), (0,)), ((), ())), preferred_element_type=F32)
    return jnp.dot(keys, rhs, preferred_element_type=F32)


def _fill_block_diag(dst_ref, qT_ref, nb, tqr, nblk, blk, shared_rows):
    tq = nb * tqr
    dst_ref[...] = jnp.zeros(dst_ref.shape, dst_ref.dtype)
    for a in range(nb):
        for m in range(nblk):
            r = a * blk if shared_rows else (a * nblk + m) * blk
            dst_ref[r:r + blk, m * tq + a * tqr:m * tq + (a + 1) * tqr] = \
                qT_ref[blk * m:blk * (m + 1), a * tqr:(a + 1) * tqr]


def _own_lanes(full, nb, dh, tqr):
    if nb == 1:
        return full
    lane_seq = lax.broadcasted_iota(I32, (dh, nb * tqr), 1) // tqr
    out = full[0:dh, :]
    for a in range(1, nb):
        out = jnp.where(lane_seq == a, full[a * dh:(a + 1) * dh, :], out)
    return out


class _Keys:
    def __init__(self, nb, tkp, tkn, new_fm, past_refs, new_refs):
        self.nb, self.tkp, self.tkn, self.new_fm = nb, tkp, tkn, new_fm
        self.past_refs, self.new_refs = past_refs, new_refs

    def past_cols(self, which, r0):
        t = self.past_refs[which][:, :, pl.ds(r0, self.tkp)]
        return t.reshape(t.shape[0] * t.shape[1], t.shape[2])

    def past_vt(self, r0):
        return [self.past_refs[1][a, :, pl.ds(r0, self.tkp)] for a in range(self.nb)]

    def new_rows(self, which, r0):
        ref = self.new_refs[which]
        if self.new_fm:
            return ref[pl.ds(r0, self.tkn), :]
        return jnp.concatenate([ref[a] for a in range(self.nb)], axis=1)

    def new_vt(self, r0):
        ref = self.new_refs[1]
        if self.new_fm:
            return [ref[:, pl.ds(r0, self.tkn)]]
        return [ref[a].T for a in range(self.nb)]


def _dsa_kernel(*refs, P, nb, tqr, tkp, tkn, topk, new_fm):
    tq = nb * tqr
    it = iter(refs)
    qT_ref, qiT_ref, wT_ref, g_ref = next(it), next(it), next(it), next(it)
    past_refs = (next(it), next(it), next(it)) if P > 0 else None
    new_refs = (next(it), next(it), next(it))
    y_ref = next(it)
    hi_ref, lo_ref, qip_ref, qbd_ref, m_ref, acc_ref = (next(it) for _ in range(6))
    kv = _Keys(nb, tkp, tkn, new_fm, past_refs, new_refs)
    j = pl.program_id(1)
    nnew = j + 1

    _fill_block_diag(qip_ref, qiT_ref, nb, tqr, IDX_HEADS, IDX_DIM, True)
    w = wT_ref[...]

    def put_keys(r0, rows, key):
        hi_ref[pl.ds(r0, rows), :] = (key >> 16).astype(I16)
        lo_ref[pl.ds(r0, rows), :] = ((key & 0xFFFF) + MIN16).astype(I16)

    def idx_keys(ki, valid, fm=False):
        s_all = _key_dot(ki, qip_ref[...], fm)
        acc = None
        for h in range(IDX_HEADS):
            t = jnp.maximum(s_all[:, h * tq:(h + 1) * tq], 0.0) * w[h:h + 1, :]
            acc = t if acc is None else acc + t
        bits = lax.bitcast_convert_type(acc, I32)
        sign = bits >> 31
        key = (bits ^ (sign & 0x7FFFFFFF)) - sign
        return key if valid is None else jnp.where(valid, key, INT_MIN)

    def for_tiles(fn, carry, n_new=None):
        if P > 0:
            carry = lax.fori_loop(0, P // tkp, lambda t, c: fn(pl.multiple_of(t * tkp, tkp), tkp, c), carry,
                                  unroll=True)
        return lax.fori_loop(0, nnew if n_new is None else n_new,
                             lambda i, c: fn(P + pl.multiple_of(i * tkn, tkn), tkn, c), carry)

    if P > 0:
        def past_keys(t, c):
            r0 = pl.multiple_of(t * tkp, tkp)
            put_keys(r0, tkp, idx_keys(kv.past_cols(2, r0), None, True))
            return c
        lax.fori_loop(0, P // tkp, past_keys, 0)

    def new_keys(i, c):
        r0 = pl.multiple_of(i * tkn, tkn)
        put_keys(P + r0, tkn, idx_keys(kv.new_rows(2, r0), None))
        return c
    lax.fori_loop(0, j, new_keys, 0)
    rd = pl.multiple_of(j * tkn, tkn)
    put_keys(P + rd, tkn, idx_keys(kv.new_rows(2, rd), _diag_valid(tkn, tq) if tkn > CHUNK else None))

    one, zero = jnp.int16(1), jnp.int16(0)

    def fold16(ind):
        parts = [ind[r:r + 16] for r in range(0, ind.shape[0], 16)]
        ways = max(1, min(len(parts), 512 // tq))
        accs = parts[:ways]
        for i in range(ways, len(parts)):
            accs[i % ways] = accs[i % ways] + parts[i]
        return functools.reduce(lambda x, y: x + y, accs)

    def total(parts):
        return jnp.sum(parts.astype(I32), axis=0, keepdims=True)

    def count(pred):
        def fn(r0, rows, c):
            ind = pred(lambda: hi_ref[pl.ds(r0, rows), :], lambda: lo_ref[pl.ds(r0, rows), :], r0, rows)
            return c + fold16(jnp.where(ind, one, zero))
        return total(for_tiles(fn, jnp.zeros((16, tq), I16)))

    def bisect16(pick, kth):
        def step(it, ans):
            cand = ans + jnp.left_shift(jnp.int32(1), 15 - it)
            c16 = cand.astype(I16)
            cnt = count(lambda hi, lo, r0, rows: pick(hi, lo) >= c16)
            return jnp.where(cnt >= kth, cand, ans)
        return lax.fori_loop(0, 16, step, jnp.full((1, tq), MIN16, I32))

    b = bisect16(lambda hi, lo: hi(), topk)
    b16 = b.astype(I16)

    def mask_lo(r0, rows, c):
        hi = hi_ref[pl.ds(r0, rows), :]
        lo_ref[pl.ds(r0, rows), :] = jnp.where(hi == b16, lo_ref[pl.ds(r0, rows), :], jnp.int16(MIN16))
        return c + fold16(jnp.where(hi > b16, one, zero))
    n_above = total(for_tiles(mask_lo, jnp.zeros((16, tq), I16)))
    kth_lo = topk - n_above
    cst = bisect16(lambda hi, lo: lo(), kth_lo)
    c16 = cst.astype(I16)

    def is_tie(hi, lo):
        return (hi == b16) & (lo == c16)
    n_gt = n_above + count(lambda hi, lo, r0, rows: lo() > c16)
    n_tie = count(lambda hi, lo, r0, rows: is_tie(hi(), lo()))
    need = (n_gt + n_tie > topk) & (b > MIN16)
    take = topk - n_gt

    @pl.when(jnp.max(jnp.where(need, 1, 0)) > 0)
    def _():
        take_f = jnp.where(need, take, 2 ** 30).astype(F32)
        tri = jnp.where(lax.broadcasted_iota(I32, (TIE_ROWS, TIE_ROWS), 0)
                        >= lax.broadcasted_iota(I32, (TIE_ROWS, TIE_ROWS), 1), 1.0, 0.0).astype(BF16)

        def demote(r0, rows, seen):
            blocks = []
            for o in range(0, rows, TIE_ROWS):
                n = min(TIE_ROWS, rows - o)
                sl = pl.ds(r0 + o, n)
                hi, lo = hi_ref[sl, :], lo_ref[sl, :]
                tie = is_tie(hi, lo)
                ind = jnp.where(tie, jnp.bfloat16(1), jnp.bfloat16(0))
                blocks.append((sl, n, hi, lo, tie, jnp.dot(tri[0:n, 0:n], ind, preferred_element_type=F32)))
            for sl, n, hi, lo, tie, within in blocks:
                rank = within + seen
                drop = tie & (jnp.where(rank > take_f, 1, 0).astype(I16) != 0)
                hi_ref[sl, :] = jnp.where(drop, jnp.int16(MIN16), hi)
                lo_ref[sl, :] = jnp.where(drop, jnp.int16(MIN16), lo)
                seen = rank[n - 1:n, :]
            return seen
        for_tiles(demote, jnp.zeros((1, tq), F32))

    whole = cst == MIN16
    bs16 = jnp.where(whole, jnp.maximum(b - 1, MIN16), b).astype(I16)
    cs16 = jnp.where(whole, 2 ** 15 - 1, cst - 1).astype(I16)
    open_, shut = jnp.bfloat16(0), jnp.bfloat16(NEG)

    _fill_block_diag(qbd_ref, qT_ref, nb, tqr, B_HEADS, B_HEAD_DIM, False)
    m_ref[...] = jnp.full(m_ref.shape, NEG, F32)
    acc_ref[...] = jnp.zeros(acc_ref.shape, F32)

    def attend(k, vt_tiles, r0, rows, fm=False):
        hi, lo = hi_ref[pl.ds(r0, rows), :], lo_ref[pl.ds(r0, rows), :]
        pen = jnp.where(hi > bs16, open_, jnp.where(lo > cs16, open_, shut)).astype(F32)
        s_all = _key_dot(k, qbd_ref[...], fm)
        vts = [t.astype(BF16) for t in vt_tiles]
        ones = _ones_rows(rows)
        for h in range(B_HEADS):
            vt_h = jnp.concatenate([v[B_HEAD_DIM * h:B_HEAD_DIM * (h + 1), :] for v in vts] + [ones], axis=0)
            _softmax_step(s_all[:, h * tq:(h + 1) * tq], pen, vt_h, m_ref, acc_ref, h)

    if P > 0:
        def pa(t, z):
            r0 = pl.multiple_of(t * tkp, tkp)
            attend(kv.past_cols(0, r0), kv.past_vt(r0), r0, tkp, True)
            return z
        lax.fori_loop(0, P // tkp, pa, 0)

    def na(i, z):
        r0 = pl.multiple_of(i * tkn, tkn)
        attend(kv.new_rows(0, r0), kv.new_vt(r0), P + r0, tkn)
        return z
    lax.fori_loop(0, nnew, na, 0)

    nv = nb * B_HEAD_DIM
    oT = jnp.concatenate([_own_lanes(acc_ref[h, 0:nv, :] / acc_ref[h, nv:nv + 1, :], nb, B_HEAD_DIM, tqr)
                          for h in range(B_HEADS)], axis=0)
    y = oT.T * _silu(g_ref[...].reshape(tq, -1))
    y_ref[...] = y.reshape(y_ref.shape)


def _attn_specs(B, L, P, nb, tqr, hw, layer, new_fm, feat_rows):
    tq = nb * tqr
    if new_fm:
        feat = lambda n: pl.BlockSpec((None, n, tq), lambda b, j: (b, 0, j))
        seqblk = pl.BlockSpec((None, tqr, hw), lambda b, j: (b, j, 0))
        new_row = lambda c: pl.BlockSpec((None, L, c), lambda b, j: (b, 0, 0))
        new_col = lambda r: pl.BlockSpec((None, r, L), lambda b, j: (b, 0, 0))
    else:
        feat = lambda n: pl.BlockSpec((n, tq), lambda b, j: (0, b))
        seqblk = pl.BlockSpec((nb, tqr, hw), lambda b, j: (b, 0, 0))
        new_row = lambda c: pl.BlockSpec((nb, L, c), lambda b, j: (b, 0, 0))
        new_col = None
    past = lambda r: pl.BlockSpec((None, nb, r, P), lambda b, j: (layer, b, 0, 0))
    return feat, seqblk, new_row, new_col, past


def _dsa(qT, qiT, wT, gate, past, new, *, B, L, P, nb, tqr, layer, new_fm):
    hw = gate.shape[-1]
    tq = nb * tqr
    Lk = P + L
    topk = min(TOPK_MAX, Lk // 4)
    feat, seqblk, new_row, new_col, pastspec = _attn_specs(B, L, P, nb, tqr, hw, layer, new_fm, None)
    in_specs = [feat(hw), feat(hw), feat(IDX_HEADS), seqblk]
    args = [qT, qiT, wT, gate]
    if P > 0:
        in_specs += [pastspec(hw), pastspec(hw), pastspec(IDX_DIM)]
        args += list(past)
    in_specs += [new_row(hw), new_col(hw) if new_fm else new_row(hw), new_row(IDX_DIM)]
    args += list(new)
    return pl.pallas_call(
        functools.partial(_dsa_kernel, P=P, nb=nb, tqr=tqr, tkp=_past_tile(P, tq), tkn=tqr, topk=topk,
                          new_fm=new_fm),
        grid=(B // nb, L // tqr),
        in_specs=in_specs,
        out_specs=seqblk,
        out_shape=jax.ShapeDtypeStruct((B, L, hw), F32),
        scratch_shapes=[pltpu.VMEM((Lk, tq), I16), pltpu.VMEM((Lk, tq), I16),
                        pltpu.VMEM((nb * IDX_DIM, IDX_HEADS * tq), BF16),
                        pltpu.VMEM((nb * hw, B_HEADS * tq), BF16),
                        pltpu.VMEM((B_HEADS, 1, tq), F32),
                        pltpu.VMEM((B_HEADS, nb * B_HEAD_DIM + ONES_ROWS, tq), F32)],
        compiler_params=_cparams(("parallel", "arbitrary")),
    )(*args)


def _diff_kernel(*refs, P, nb, tqr, tkp, tkn, lam_init, new_fm):
    tq = nb * tqr
    it = iter(refs)
    qT_ref, g_ref, lq1_ref, lk1_ref, lq2_ref, lk2_ref, sub_ref = (next(it) for _ in range(7))
    past_refs = (next(it), next(it)) if P > 0 else None
    new_refs = (next(it), next(it))
    y_ref = next(it)
    qbd_ref, m_ref, acc_ref = (next(it) for _ in range(3))
    kv = _Keys(nb, tkp, tkn, new_fm, past_refs, new_refs)
    j = pl.program_id(1)
    nmaps = 2 * C_HEADS

    _fill_block_diag(qbd_ref, qT_ref, nb, tqr, nmaps, C_HALF, False)
    m_ref[...] = jnp.full(m_ref.shape, NEG, F32)
    acc_ref[...] = jnp.zeros(acc_ref.shape, F32)

    def attend(k, vt_tiles, pen, fm=False):
        s_all = _key_dot(k, qbd_ref[...], fm)
        vts = [t.astype(BF16) for t in vt_tiles]
        ones = _ones_rows(s_all.shape[0])
        for h in range(C_HEADS):
            vt_h = jnp.concatenate([v[C_HEAD_DIM * h:C_HEAD_DIM * (h + 1), :] for v in vts] + [ones], axis=0)
            for mi in (2 * h, 2 * h + 1):
                _softmax_step(s_all[:, mi * tq:(mi + 1) * tq], pen, vt_h, m_ref, acc_ref, mi)

    if P > 0:
        def pa(t, z):
            r0 = pl.multiple_of(t * tkp, tkp)
            attend(kv.past_cols(0, r0), kv.past_vt(r0), None, True)
            return z
        lax.fori_loop(0, P // tkp, pa, 0)

    def na(i, z):
        r0 = pl.multiple_of(i * tkn, tkn)
        attend(kv.new_rows(0, r0), kv.new_vt(r0), None)
        return z
    lax.fori_loop(0, j, na, 0)
    rd = pl.multiple_of(j * tkn, tkn)
    attend(kv.new_rows(0, rd), kv.new_vt(rd),
           jnp.where(_diag_valid(tkn, tq), 0.0, NEG) if tkn > CHUNK else None)

    lam = (jnp.exp(jnp.sum(lq1_ref[...] * lk1_ref[...], axis=-1, keepdims=True))
           - jnp.exp(jnp.sum(lq2_ref[...] * lk2_ref[...], axis=-1, keepdims=True)) + lam_init)
    outs = []
    nv = nb * C_HEAD_DIM
    for h in range(C_HEADS):
        o = (acc_ref[2 * h, 0:nv, :] / acc_ref[2 * h, nv:nv + 1, :]
             - lam * (acc_ref[2 * h + 1, 0:nv, :] / acc_ref[2 * h + 1, nv:nv + 1, :]))
        o = _own_lanes(o, nb, C_HEAD_DIM, tqr)
        ms = jnp.mean(o * o, axis=0, keepdims=True)
        outs.append(o * lax.rsqrt(ms + EPS))
    o = jnp.concatenate(outs, axis=0).T
    y = (o * sub_ref[...]) * (1.0 - lam_init) * _silu(g_ref[...].reshape(tq, -1))
    y_ref[...] = y.reshape(y_ref.shape)


def _diff(qT, gate, lams, subrow, past, new, *, B, L, P, nb, tqr, layer, new_fm, lam_init):
    hw = gate.shape[-1]
    tq = nb * tqr
    nmaps = 2 * C_HEADS
    feat, seqblk, new_row, new_col, pastspec = _attn_specs(B, L, P, nb, tqr, hw, layer, new_fm, None)
    const = lambda s: pl.BlockSpec(s, lambda b, j: (0, 0))
    in_specs = [feat(hw), seqblk] + [const((1, C_HALF))] * 4 + [const((1, hw))]
    args = [qT, gate, *lams, subrow]
    if P > 0:
        in_specs += [pastspec(hw), pastspec(hw)]
        args += list(past)
    in_specs += [new_row(hw), new_col(hw) if new_fm else new_row(hw)]
    args += list(new)
    return pl.pallas_call(
        functools.partial(_diff_kernel, P=P, nb=nb, tqr=tqr, tkp=_past_tile(P, tq), tkn=tqr,
                          lam_init=lam_init, new_fm=new_fm),
        grid=(B // nb, L // tqr),
        in_specs=in_specs,
        out_specs=seqblk,
        out_shape=jax.ShapeDtypeStruct((B, L, hw), F32),
        scratch_shapes=[pltpu.VMEM((nb * hw, nmaps * tq), BF16),
                        pltpu.VMEM((nmaps, 1, tq), F32),
                        pltpu.VMEM((nmaps, nb * C_HEAD_DIM + ONES_ROWS, tq), F32)],
        compiler_params=_cparams(("parallel", "arbitrary")),
    )(*args)


def _outproj_kernel(ya_ref, yb_ref, yc_ref, x_ref, w_ref, g_ref, o_ref, *, aw, hw):
    mix = jnp.dot(ya_ref[...].astype(BF16), w_ref[0:aw, :], preferred_element_type=F32)
    mix = mix + jnp.dot(yb_ref[...].astype(BF16), w_ref[aw:aw + hw, :], preferred_element_type=F32)
    mix = mix + jnp.dot(yc_ref[...].astype(BF16), w_ref[aw + hw:aw + 2 * hw, :], preferred_element_type=F32)
    ms = jnp.mean(mix * mix, axis=-1, keepdims=True)
    o_ref[...] = x_ref[...] + mix * lax.rsqrt(ms + EPS) * g_ref[...]


def _outproj(ya, yb, yc, x2d, w, g, *, tm):
    T, D = x2d.shape
    aw, hw = D // 2, D // 4
    row = lambda n: pl.BlockSpec((tm, n), lambda i: (i, 0))
    const = lambda s: pl.BlockSpec(s, lambda i: (0, 0))
    return pl.pallas_call(
        functools.partial(_outproj_kernel, aw=aw, hw=hw),
        grid=(T // tm,),
        in_specs=[row(aw), row(hw), row(hw), row(D), const((D, D)), const((1, D))],
        out_specs=row(D),
        out_shape=jax.ShapeDtypeStruct((T, D), F32),
        compiler_params=_cparams(("parallel",)),
    )(ya, yb, yc, x2d, w, g)


_ROW_PROMPT = ("a_x", "a_g", "b_g", "c_g", "b_k", "c_k", "b_ik")
_FEAT_PROMPT = ("b_q", "b_iq", "c_q", "b_k", "b_v", "c_k", "c_v", "b_ik", "b_iw")
_ROW_SAMPLE = ("a_x", "a_g", "b_g", "c_g", "b_k", "b_v", "c_k", "c_v", "b_ik")
_FEAT_SAMPLE = ("b_q", "b_iq", "c_q", "b_ik", "b_iw")
_BF16_OUT = {"b_q", "b_iq", "c_q"}


def _pack_w_in(w_in, D, rows, feats, row_bf16):
    aw, hw = D // 2, D // 4
    sizes = (aw, aw, hw, hw, hw, hw, IDX_HEADS * IDX_DIM, IDX_DIM, IDX_HEADS, hw, hw, hw, hw)
    names = ("a_x", "a_g", "b_q", "b_k", "b_v", "b_g", "b_iq", "b_ik", "b_iw", "c_q", "c_k", "c_v", "c_g")
    scale = {"b_q": B_HEAD_DIM ** -0.5 * LOG2E, "c_q": C_HALF ** -0.5 * LOG2E,
             "b_iw": (IDX_DIM ** -0.5) * (IDX_HEADS ** -0.5)}
    seg, o = {}, 0
    for n, s in zip(names, sizes):
        seg[n] = w_in[..., o:o + s] * scale[n] if n in scale else w_in[..., o:o + s]
        o += s
    depth = w_in.shape[0]

    def cat(parts, mult):
        w = jnp.concatenate([seg[n] for n in parts], axis=-1)
        pad = (-w.shape[-1]) % mult
        return jnp.concatenate([w, jnp.zeros((depth, D, pad), w.dtype)], axis=-1) if pad else w
    wr = cat(rows, 128).astype(BF16)
    wf = jnp.swapaxes(cat(feats, 16), 1, 2).astype(BF16)
    row_plan = tuple((seg[n].shape[-1], BF16 if n in row_bf16 else F32) for n in rows)
    feat_plan = tuple((seg[n].shape[-1], BF16 if n in _BF16_OUT else F32) for n in feats)
    return wr, wf, row_plan, feat_plan


def _block_diag(w):
    depth, nb, n, _ = w.shape
    eye = jnp.eye(nb, dtype=w.dtype)
    return jnp.einsum("lnde,nm->lndme", w, eye).reshape(depth, nb * n, nb * n)


def _prompt_layer(x, B, L, wl, zero_buf, zero_h, lam_init):
    T, D = x.shape
    hw = D // 4
    tm = min(PROJ_ROWS, L)
    out = _inproj(x, wl["norm_pre"], wl["wr_p"], wl["wf_p"], tm=tm, row_plan=wl["plan_p"][0],
                  feat_plan=wl["plan_p"][1], feat_batch=(B, L))
    ax, ag, bg, cg, bk_bf, ck_bf, bik = out[:7]
    bqT, biqT, cqT, bkT, bvT, ckT, cvT, bikT, biwT = out[7:]
    r3 = lambda a: a.reshape(B, L, a.shape[-1])
    ya, nbuf, hlast = _amix(r3(ax), r3(ag), zero_buf, zero_h[:, None, :], wl["a_conv_w"], wl["a_conv_b"],
                            wl["rw"], wl["a_rg_b"], wl["iw"], wl["a_in_b"], wl["a_lambda"], tl=min(L, 512))
    tqr = min(512, L)
    kw = dict(B=B, L=L, P=0, nb=1, tqr=tqr, layer=0, new_fm=True)
    yb = _dsa(bqT, biqT, biwT, r3(bg), None, (r3(bk_bf), bvT, r3(bik)), **kw)
    yc = _diff(cqT, r3(cg), wl["lams"], wl["subrow"], None, (r3(ck_bf), cvT), lam_init=lam_init, **kw)
    x_new = _outproj(ya.reshape(T, -1), yb.reshape(T, hw), yc.reshape(T, hw), x, wl["w_out"], wl["norm_post"],
                     tm=tm)
    heads = lambda t, nh: t.reshape(B, nh, t.shape[1] // nh, L).transpose(0, 3, 1, 2)
    states = (nbuf, hlast[:, 0, :], heads(bkT, B_HEADS), heads(bvT, B_HEADS), bikT.transpose(0, 2, 1),
              heads(ckT, C_HEADS), heads(cvT, C_HEADS))
    return x_new, states


def _sample_layer(x, B, L, P, layer, wl, conv_buf, h0, past, lam_init):
    T, D = x.shape
    hw = D // 4
    tm = min(PROJ_ROWS, T)
    out = _inproj(x, wl["norm_pre"], wl["wr_s"], wl["wf_s"], tm=tm, row_plan=wl["plan_s"][0],
                  feat_plan=wl["plan_s"][1], feat_batch=None)
    ax, ag, bg, cg, bk, bv, ck, cv, bik = out[:9]
    bqT, biqT, cqT, bikT, biwT = out[9:]
    r3 = lambda a: a.reshape(B, L, a.shape[-1])
    ya, nbuf, hlast = _amix(r3(ax), r3(ag), conv_buf, h0[:, None, :], wl["a_conv_w"], wl["a_conv_b"],
                            wl["rw"], wl["a_rg_b"], wl["iw"], wl["a_in_b"], wl["a_lambda"], tl=L)
    nb = 2 if B % 2 == 0 else 1
    kw = dict(B=B, L=L, P=P, nb=nb, tqr=L, layer=layer, new_fm=False)
    pkT, pvT, pkiT, pckT, pcvT = past
    yb = _dsa(bqT, biqT, biwT, r3(bg), (pkT, pvT, pkiT), (r3(bk), r3(bv), r3(bik)), **kw)
    yc = _diff(cqT, r3(cg), wl["lams"], wl["subrow"], (pckT, pcvT), (r3(ck), r3(cv)), lam_init=lam_init, **kw)
    x_new = _outproj(ya.reshape(T, -1), yb.reshape(T, hw), yc.reshape(T, hw), x, wl["w_out"], wl["norm_post"],
                     tm=tm)
    states = (nbuf, hlast[:, 0, :],
              bk.reshape(B, L, B_HEADS, B_HEAD_DIM), bv.reshape(B, L, B_HEADS, B_HEAD_DIM),
              bikT.reshape(IDX_DIM, B, L).transpose(1, 2, 0),
              ck.reshape(B, L, C_HEADS, C_HEAD_DIM), cv.reshape(B, L, C_HEADS, C_HEAD_DIM))
    return x_new, states


def kernel(x_prompt, x_sample, cache_a_conv, state_a_h, cache_b_k, cache_b_v, cache_b_kidx, cache_c_k, cache_c_v, norm_pre, norm_post, w_in, w_out, a_conv_w, a_conv_b, a_rg_w, a_rg_b, a_in_w, a_in_b, a_lambda, c_lam_q1, c_lam_k1, c_lam_q2, c_lam_k2, c_subln):
    Bp, Lp, D = x_prompt.shape
    Bs, Ls, _ = x_sample.shape
    depth = w_in.shape[0]
    P = cache_b_k.shape[2]
    aw = D // 2

    wr_p, wf_p, *plan_p = _pack_w_in(w_in, D, _ROW_PROMPT, _FEAT_PROMPT, {"b_k", "c_k"})
    wr_s, wf_s, *plan_s = _pack_w_in(w_in, D, _ROW_SAMPLE, _FEAT_SAMPLE, set())
    rw = _block_diag(a_rg_w).astype(BF16)
    iw = _block_diag(a_in_w).astype(BF16)
    wo = w_out.astype(BF16)

    fm = lambda c: jnp.transpose(c, (0, 1, 3, 4, 2)).reshape(depth, Bs, -1, P)
    past = (fm(cache_b_k), fm(cache_b_v), jnp.transpose(cache_b_kidx, (0, 1, 3, 2)), fm(cache_c_k), fm(cache_c_v))

    xp = x_prompt.reshape(Bp * Lp, D)
    xs = x_sample.reshape(Bs * Ls, D)
    zero_buf = jnp.zeros((Bp, CONV_W - 1, aw), F32)
    zero_h = jnp.zeros((Bp, aw), F32)
    p_st, s_st = [], []
    for l in range(depth):
        row = lambda a: a[l][None, :]
        wl = {"norm_pre": row(norm_pre), "norm_post": row(norm_post), "w_out": wo[l],
              "wr_p": wr_p[l], "wf_p": wf_p[l], "plan_p": plan_p, "wr_s": wr_s[l], "wf_s": wf_s[l], "plan_s": plan_s,
              "a_conv_w": a_conv_w[l], "a_conv_b": row(a_conv_b), "rw": rw[l], "a_rg_b": row(a_rg_b),
              "iw": iw[l], "a_in_b": row(a_in_b), "a_lambda": row(a_lambda),
              "lams": (row(c_lam_q1), row(c_lam_k1), row(c_lam_q2), row(c_lam_k2)),
              "subrow": jnp.tile(c_subln[l], C_HEADS)[None, :]}
        lam_init = 0.8 - 0.6 * math.exp(-0.3 * l)
        xp, st_p = _prompt_layer(xp, Bp, Lp, wl, zero_buf, zero_h, lam_init)
        xs, st_s = _sample_layer(xs, Bs, Ls, P, l, wl, cache_a_conv[l], state_a_h[l], past, lam_init)
        p_st.append(st_p)
        s_st.append(st_s)

    stk = lambda states, i: jnp.stack([s[i] for s in states], axis=0)
    return (xp.reshape(Bp, Lp, D), xs.reshape(Bs, Ls, D),
            *[stk(p_st, i) for i in range(7)], *[stk(s_st, i) for i in range(7)])
```

```python
import functools
import math

import jax
import jax.numpy as jnp
from jax import lax
from jax.experimental import pallas as pl
from jax.experimental.pallas import tpu as pltpu

F32 = jnp.float32
BF16 = jnp.bfloat16
I32 = jnp.int32

CHUNK = 64
CONV_W = 4
LRU_C = 8.0
B_HEADS = 4
B_HEAD_DIM = 64
IDX_HEADS = 8
IDX_DIM = 32
TOPK_MAX = 256
C_HEADS = 4
C_HALF = 32
C_HEAD_DIM = 2 * C_HALF
EPS = 1e-6
NEG = -1e30
INT_MIN = -2 ** 31
VMEM_LIMIT_BYTES = 56 * 1024 * 1024
KEY_TILE = 256
TILE_ELEMS = 512 * 256
TIE_ROWS = 256
PROJ_ROWS = 512
ONES_ROWS = 16
I16 = jnp.int16
MIN16 = -2 ** 15
LOG2E = math.log2(math.e)


def _past_tile(P, tq):
    rows = max(KEY_TILE, TILE_ELEMS // tq)
    while P % rows:
        rows //= 2
    return rows


def _cparams(sem):
    return pltpu.CompilerParams(dimension_semantics=sem, vmem_limit_bytes=VMEM_LIMIT_BYTES)


def _silu(g):
    return g * jax.nn.sigmoid(g)


def _inproj_kernel(x_ref, g_ref, wr_ref, wf_ref, *out_refs, row_plan, feat_plan):
    x = x_ref[...]
    ms = jnp.mean(x * x, axis=-1, keepdims=True)
    xn = (x * lax.rsqrt(ms + EPS) * g_ref[...]).astype(BF16)
    proj = jnp.dot(xn, wr_ref[...], preferred_element_type=F32)
    projT = lax.dot_general(wf_ref[...], xn, (((1,), (1,)), ((), ())),
                            preferred_element_type=F32)
    refs = iter(out_refs)
    o = 0
    for n, dt in row_plan:
        ref = next(refs)
        ref[...] = proj[:, o:o + n].astype(dt)
        o += n
    o = 0
    for n, dt in feat_plan:
        ref = next(refs)
        ref[...] = projT[o:o + n, :].astype(dt)
        o += n


def _inproj(x2d, g, wr, wf, *, tm, row_plan, feat_plan, feat_batch):
    T, D = x2d.shape
    row = lambda n: pl.BlockSpec((tm, n), lambda i: (i, 0))
    const = lambda s: pl.BlockSpec(s, lambda i: (0, 0))
    out_shape = [jax.ShapeDtypeStruct((T, n), dt) for n, dt in row_plan]
    out_specs = [row(n) for n, _ in row_plan]
    if feat_batch is None:
        out_shape += [jax.ShapeDtypeStruct((n, T), dt) for n, dt in feat_plan]
        out_specs += [pl.BlockSpec((n, tm), lambda i: (0, i)) for n, _ in feat_plan]
    else:
        B, L = feat_batch
        npb = L // tm
        out_shape += [jax.ShapeDtypeStruct((B, n, L), dt) for n, dt in feat_plan]
        out_specs += [pl.BlockSpec((None, n, tm), lambda i: (i // npb, 0, i % npb)) for n, _ in feat_plan]
    return pl.pallas_call(
        functools.partial(_inproj_kernel, row_plan=row_plan, feat_plan=feat_plan),
        grid=(T // tm,),
        in_specs=[row(D), const((1, D)), const(wr.shape), const(wf.shape)],
        out_specs=out_specs, out_shape=out_shape,
        compiler_params=_cparams(("parallel",)),
    )(x2d, g, wr, wf)


def _amix_kernel(ax_ref, ag_ref, buf_ref, h0_ref, cw_ref, cb_ref, rw_ref, rb_ref, iw_ref, ib_ref, lam_ref,
                 ya_ref, nbuf_ref, hlast_ref, xbuf_ref, hc_ref, *, tl, nl, aw):
    li = pl.program_id(1)

    @pl.when(li == 0)
    def _():
        xbuf_ref[0:8, :] = jnp.zeros((8, aw), F32)
        xbuf_ref[8 - (CONV_W - 1):8, :] = buf_ref[...]
        hc_ref[...] = h0_ref[...]

    x = ax_ref[...]
    xbuf_ref[8:8 + tl, :] = x
    cw = cw_ref[...]
    conv = cb_ref[...] + x * cw[CONV_W - 1:CONV_W, :]
    for s in range(1, CONV_W):
        conv = conv + xbuf_ref[8 - s:8 - s + tl, :] * cw[CONV_W - 1 - s:CONV_W - s, :]

    cbf = conv.astype(BF16)
    r = jax.nn.sigmoid(jnp.dot(cbf, rw_ref[...], preferred_element_type=F32) + rb_ref[...])
    ig = jax.nn.sigmoid(jnp.dot(cbf, iw_ref[...], preferred_element_type=F32) + ib_ref[...])
    nl_lam = -lam_ref[...]
    sp = jnp.maximum(nl_lam, 0.0) + jnp.log1p(jnp.exp(-jnp.abs(nl_lam)))
    log_a = (-LRU_C) * r * sp
    a = jnp.exp(log_a)
    u = jnp.sqrt(1.0 - a * a) * (ig * conv)

    row = lax.broadcasted_iota(I32, (tl, aw), 0)
    s = 1
    while s < tl:
        if s % 8:
            keep = row >= s
            u = jnp.where(keep, a * pltpu.roll(u, s, 0) + u, u)
            a = jnp.where(keep, a * pltpu.roll(a, s, 0), a)
        else:
            u = jnp.concatenate([u[:s], a[s:] * u[:tl - s] + u[s:]], axis=0)
            a = jnp.concatenate([a[:s], a[s:] * a[:tl - s]], axis=0)
        s *= 2
    h = a * hc_ref[...] + u
    ya_ref[...] = h * _silu(ag_ref[...])
    hc_ref[...] = h[tl - 1:tl, :]
    xbuf_ref[0:8, :] = x[tl - 8:tl, :]

    @pl.when(li == nl - 1)
    def _():
        nbuf_ref[...] = xbuf_ref[8 + tl - (CONV_W - 1):8 + tl, :]
        hlast_ref[...] = h[tl - 1:tl, :]


def _amix(ax, ag, buf, h0, cw, cb, rw, rb, iw, ib, lam, *, tl):
    B, L, aw = ax.shape
    nl = L // tl
    seq = pl.BlockSpec((None, tl, aw), lambda b, l: (b, l, 0))
    perb = lambda r: pl.BlockSpec((None, r, aw), lambda b, l: (b, 0, 0))
    const = lambda s: pl.BlockSpec(s, lambda b, l: (0, 0))
    return pl.pallas_call(
        functools.partial(_amix_kernel, tl=tl, nl=nl, aw=aw),
        grid=(B, nl),
        in_specs=[seq, seq, perb(CONV_W - 1), perb(1), const((CONV_W, aw)), const((1, aw)),
                  const((aw, aw)), const((1, aw)), const((aw, aw)), const((1, aw)), const((1, aw))],
        out_specs=[seq, perb(CONV_W - 1), perb(1)],
        out_shape=[jax.ShapeDtypeStruct((B, L, aw), F32), jax.ShapeDtypeStruct((B, CONV_W - 1, aw), F32),
                   jax.ShapeDtypeStruct((B, 1, aw), F32)],
        scratch_shapes=[pltpu.VMEM((tl + 8, aw), F32), pltpu.VMEM((1, aw), F32)],
        compiler_params=_cparams(("parallel", "arbitrary")),
    )(ax, ag, buf, h0, cw, cb, rw, rb, iw, ib, lam)


def _col_reduce(x, op):
    rows, n = x.shape
    return op(op(x.reshape(rows // 8, 8, n), axis=0), axis=0, keepdims=True)


def _diag_valid(rows, tq):
    kc = lax.broadcasted_iota(I32, (rows, tq), 0) // CHUNK
    qc = lax.broadcasted_iota(I32, (rows, tq), 1) // CHUNK
    return kc <= qc


def _ones_rows(rows):
    return jnp.where(lax.broadcasted_iota(I32, (ONES_ROWS, rows), 0) == 0, 1.0, 0.0).astype(BF16)


def _softmax_step(s, pen, vt_h, m_ref, acc_ref, idx):
    m_old = m_ref[idx]
    sm = s if pen is None else s + pen
    m_new = jnp.maximum(m_old, _col_reduce(sm, jnp.max))
    alpha = jnp.exp2(m_old - m_new)
    p = jnp.exp2(sm - m_new).astype(BF16)
    pv = jnp.dot(vt_h, p, preferred_element_type=F32)
    acc_ref[idx, :, :] = alpha * acc_ref[idx, :, :] + pv
    m_ref[idx] = m_new


def _key_dot(keys, rhs, keys_fm):
    keys = keys.astype(BF16)
    if keys_fm:
        return lax.dot_general(keys, rhs, (((0,), (0,)), ((), ())), preferred_element_type=F32)
    return jnp.dot(keys, rhs, preferred_element_type=F32)


def _fill_block_diag(dst_ref, qT_ref, nb, tqr, nblk, blk, shared_rows):
    tq = nb * tqr
    dst_ref[...] = jnp.zeros(dst_ref.shape, dst_ref.dtype)
    for a in range(nb):
        for m in range(nblk):
            r = a * blk if shared_rows else (a * nblk + m) * blk
            dst_ref[r:r + blk, m * tq + a * tqr:m * tq + (a + 1) * tqr] = \
                qT_ref[blk * m:blk * (m + 1), a * tqr:(a + 1) * tqr]


def _own_lanes(full, nb, dh, tqr):
    if nb == 1:
        return full
    lane_seq = lax.broadcasted_iota(I32, (dh, nb * tqr), 1) // tqr
    out = full[0:dh, :]
    for a in range(1, nb):
        out = jnp.where(lane_seq == a, full[a * dh:(a + 1) * dh, :], out)
    return out


class _Keys:
    def __init__(self, nb, tkp, tkn, new_fm, past_refs, new_refs):
        self.nb, self.tkp, self.tkn, self.new_fm = nb, tkp, tkn, new_fm
        self.past_refs, self.new_refs = past_refs, new_refs

    def past_cols(self, which, r0):
        t = self.past_refs[which][:, :, pl.ds(r0, self.tkp)]
        return t.reshape(t.shape[0] * t.shape[1], t.shape[2])

    def past_vt(self, r0):
        return [self.past_refs[1][a, :, pl.ds(r0, self.tkp)] for a in range(self.nb)]

    def new_rows(self, which, r0):
        ref = self.new_refs[which]
        if self.new_fm:
            return ref[pl.ds(r0, self.tkn), :]
        return jnp.concatenate([ref[a] for a in range(self.nb)], axis=1)

    def new_vt(self, r0):
        ref = self.new_refs[1]
        if self.new_fm:
            return [ref[:, pl.ds(r0, self.tkn)]]
        return [ref[a].T for a in range(self.nb)]


def _dsa_kernel(*refs, P, nb, tqr, tkp, tkn, topk, new_fm):
    tq = nb * tqr
    it = iter(refs)
    qT_ref, qiT_ref, wT_ref, g_ref = next(it), next(it), next(it), next(it)
    past_refs = (next(it), next(it), next(it)) if P > 0 else None
    new_refs = (next(it), next(it), next(it))
    y_ref = next(it)
    hi_ref, lo_ref, qip_ref, qbd_ref, m_ref, acc_ref = (next(it) for _ in range(6))
    kv = _Keys(nb, tkp, tkn, new_fm, past_refs, new_refs)
    j = pl.program_id(1)
    nnew = j + 1

    _fill_block_diag(qip_ref, qiT_ref, nb, tqr, IDX_HEADS, IDX_DIM, True)
    w = wT_ref[...]

    def put_keys(r0, rows, key):
        hi_ref[pl.ds(r0, rows), :] = (key >> 16).astype(I16)
        lo_ref[pl.ds(r0, rows), :] = ((key & 0xFFFF) + MIN16).astype(I16)

    def idx_keys(ki, valid, fm=False):
        s_all = _key_dot(ki, qip_ref[...], fm)
        acc = None
        for h in range(IDX_HEADS):
            t = jnp.maximum(s_all[:, h * tq:(h + 1) * tq], 0.0) * w[h:h + 1, :]
            acc = t if acc is None else acc + t
        bits = lax.bitcast_convert_type(acc, I32)
        sign = bits >> 31
        key = (bits ^ (sign & 0x7FFFFFFF)) - sign
        return key if valid is None else jnp.where(valid, key, INT_MIN)

    def for_tiles(fn, carry, n_new=None):
        if P > 0:
            carry = lax.fori_loop(0, P // tkp, lambda t, c: fn(pl.multiple_of(t * tkp, tkp), tkp, c), carry,
                                  unroll=True)
        return lax.fori_loop(0, nnew if n_new is None else n_new,
                             lambda i, c: fn(P + pl.multiple_of(i * tkn, tkn), tkn, c), carry)

    if P > 0:
        def past_keys(t, c):
            r0 = pl.multiple_of(t * tkp, tkp)
            put_keys(r0, tkp, idx_keys(kv.past_cols(2, r0), None, True))
            return c
        lax.fori_loop(0, P // tkp, past_keys, 0, unroll=True)

    def new_keys(i, c):
        r0 = pl.multiple_of(i * tkn, tkn)
        put_keys(P + r0, tkn, idx_keys(kv.new_rows(2, r0), None))
        return c
    lax.fori_loop(0, j, new_keys, 0)
    rd = pl.multiple_of(j * tkn, tkn)
    put_keys(P + rd, tkn, idx_keys(kv.new_rows(2, rd), _diag_valid(tkn, tq) if tkn > CHUNK else None))

    one, zero = jnp.int16(1), jnp.int16(0)

    def fold16(ind):
        parts = [ind[r:r + 16] for r in range(0, ind.shape[0], 16)]
        ways = max(1, min(len(parts), 512 // tq))
        accs = parts[:ways]
        for i in range(ways, len(parts)):
            accs[i % ways] = accs[i % ways] + parts[i]
        return functools.reduce(lambda x, y: x + y, accs)

    def total(parts):
        return jnp.sum(parts.astype(I32), axis=0, keepdims=True)

    def count(pred):
        def fn(r0, rows, c):
            ind = pred(lambda: hi_ref[pl.ds(r0, rows), :], lambda: lo_ref[pl.ds(r0, rows), :], r0, rows)
            return c + fold16(jnp.where(ind, one, zero))
        return total(for_tiles(fn, jnp.zeros((16, tq), I16)))

    def bisect16(pick, kth):
        def step(it, ans):
            cand = ans + jnp.left_shift(jnp.int32(1), 15 - it)
            c16 = cand.astype(I16)
            cnt = count(lambda hi, lo, r0, rows: pick(hi, lo) >= c16)
            return jnp.where(cnt >= kth, cand, ans)
        return lax.fori_loop(0, 16, step, jnp.full((1, tq), MIN16, I32))

    b = bisect16(lambda hi, lo: hi(), topk)
    b16 = b.astype(I16)

    def mask_lo(r0, rows, c):
        hi = hi_ref[pl.ds(r0, rows), :]
        lo_ref[pl.ds(r0, rows), :] = jnp.where(hi == b16, lo_ref[pl.ds(r0, rows), :], jnp.int16(MIN16))
        return c + fold16(jnp.where(hi > b16, one, zero))
    n_above = total(for_tiles(mask_lo, jnp.zeros((16, tq), I16)))
    kth_lo = topk - n_above
    cst = bisect16(lambda hi, lo: lo(), kth_lo)
    c16 = cst.astype(I16)

    def is_tie(hi, lo):
        return (hi == b16) & (lo == c16)
    n_gt = n_above + count(lambda hi, lo, r0, rows: lo() > c16)
    n_tie = count(lambda hi, lo, r0, rows: is_tie(hi(), lo()))
    need = (n_gt + n_tie > topk) & (b > MIN16)
    take = topk - n_gt

    @pl.when(jnp.max(jnp.where(need, 1, 0)) > 0)
    def _():
        take_f = jnp.where(need, take, 2 ** 30).astype(F32)
        tri = jnp.where(lax.broadcasted_iota(I32, (TIE_ROWS, TIE_ROWS), 0)
                        >= lax.broadcasted_iota(I32, (TIE_ROWS, TIE_ROWS), 1), 1.0, 0.0).astype(BF16)

        def demote(r0, rows, seen):
            blocks = []
            for o in range(0, rows, TIE_ROWS):
                n = min(TIE_ROWS, rows - o)
                sl = pl.ds(r0 + o, n)
                hi, lo = hi_ref[sl, :], lo_ref[sl, :]
                tie = is_tie(hi, lo)
                ind = jnp.where(tie, jnp.bfloat16(1), jnp.bfloat16(0))
                blocks.append((sl, n, hi, lo, tie, jnp.dot(tri[0:n, 0:n], ind, preferred_element_type=F32)))
            for sl, n, hi, lo, tie, within in blocks:
                rank = within + seen
                drop = tie & (jnp.where(rank > take_f, 1, 0).astype(I16) != 0)
                hi_ref[sl, :] = jnp.where(drop, jnp.int16(MIN16), hi)
                lo_ref[sl, :] = jnp.where(drop, jnp.int16(MIN16), lo)
                seen = rank[n - 1:n, :]
            return seen
        for_tiles(demote, jnp.zeros((1, tq), F32))

    whole = cst == MIN16
    bs16 = jnp.where(whole, jnp.maximum(b - 1, MIN16), b).astype(I16)
    cs16 = jnp.where(whole, 2 ** 15 - 1, cst - 1).astype(I16)
    open_, shut = jnp.bfloat16(0), jnp.bfloat16(NEG)

    _fill_block_diag(qbd_ref, qT_ref, nb, tqr, B_HEADS, B_HEAD_DIM, False)
    m_ref[...] = jnp.full(m_ref.shape, NEG, F32)
    acc_ref[...] = jnp.zeros(acc_ref.shape, F32)

    def attend(k, vt_tiles, r0, rows, fm=False):
        hi, lo = hi_ref[pl.ds(r0, rows), :], lo_ref[pl.ds(r0, rows), :]
        pen = jnp.where(hi > bs16, open_, jnp.where(lo > cs16, open_, shut)).astype(F32)
        s_all = _key_dot(k, qbd_ref[...], fm)
        vts = [t.astype(BF16) for t in vt_tiles]
        ones = _ones_rows(rows)
        for h in range(B_HEADS):
            vt_h = jnp.concatenate([v[B_HEAD_DIM * h:B_HEAD_DIM * (h + 1), :] for v in vts] + [ones], axis=0)
            _softmax_step(s_all[:, h * tq:(h + 1) * tq], pen, vt_h, m_ref, acc_ref, h)

    if P > 0:
        def pa(t, z):
            r0 = pl.multiple_of(t * tkp, tkp)
            attend(kv.past_cols(0, r0), kv.past_vt(r0), r0, tkp, True)
            return z
        lax.fori_loop(0, P // tkp, pa, 0, unroll=True)

    def na(i, z):
        r0 = pl.multiple_of(i * tkn, tkn)
        attend(kv.new_rows(0, r0), kv.new_vt(r0), P + r0, tkn)
        return z
    lax.fori_loop(0, nnew, na, 0)

    nv = nb * B_HEAD_DIM
    oT = jnp.concatenate([_own_lanes(acc_ref[h, 0:nv, :] / acc_ref[h, nv:nv + 1, :], nb, B_HEAD_DIM, tqr)
                          for h in range(B_HEADS)], axis=0)
    y = oT.T * _silu(g_ref[...].reshape(tq, -1))
    y_ref[...] = y.reshape(y_ref.shape)


def _attn_specs(B, L, P, nb, tqr, hw, layer, new_fm, feat_rows):
    tq = nb * tqr
    if new_fm:
        feat = lambda n: pl.BlockSpec((None, n, tq), lambda b, j: (b, 0, j))
        seqblk = pl.BlockSpec((None, tqr, hw), lambda b, j: (b, j, 0))
        new_row = lambda c: pl.BlockSpec((None, L, c), lambda b, j: (b, 0, 0))
        new_col = lambda r: pl.BlockSpec((None, r, L), lambda b, j: (b, 0, 0))
    else:
        feat = lambda n: pl.BlockSpec((n, tq), lambda b, j: (0, b))
        seqblk = pl.BlockSpec((nb, tqr, hw), lambda b, j: (b, 0, 0))
        new_row = lambda c: pl.BlockSpec((nb, L, c), lambda b, j: (b, 0, 0))
        new_col = None
    past = lambda r: pl.BlockSpec((None, nb, r, P), lambda b, j: (layer, b, 0, 0))
    return feat, seqblk, new_row, new_col, past


def _dsa(qT, qiT, wT, gate, past, new, *, B, L, P, nb, tqr, layer, new_fm):
    hw = gate.shape[-1]
    tq = nb * tqr
    Lk = P + L
    topk = min(TOPK_MAX, Lk // 4)
    feat, seqblk, new_row, new_col, pastspec = _attn_specs(B, L, P, nb, tqr, hw, layer, new_fm, None)
    in_specs = [feat(hw), feat(hw), feat(IDX_HEADS), seqblk]
    args = [qT, qiT, wT, gate]
    if P > 0:
        in_specs += [pastspec(hw), pastspec(hw), pastspec(IDX_DIM)]
        args += list(past)
    in_specs += [new_row(hw), new_col(hw) if new_fm else new_row(hw), new_row(IDX_DIM)]
    args += list(new)
    return pl.pallas_call(
        functools.partial(_dsa_kernel, P=P, nb=nb, tqr=tqr, tkp=_past_tile(P, tq), tkn=tqr, topk=topk,
                          new_fm=new_fm),
        grid=(B // nb, L // tqr),
        in_specs=in_specs,
        out_specs=seqblk,
        out_shape=jax.ShapeDtypeStruct((B, L, hw), F32),
        scratch_shapes=[pltpu.VMEM((Lk, tq), I16), pltpu.VMEM((Lk, tq), I16),
                        pltpu.VMEM((nb * IDX_DIM, IDX_HEADS * tq), BF16),
                        pltpu.VMEM((nb * hw, B_HEADS * tq), BF16),
                        pltpu.VMEM((B_HEADS, 1, tq), F32),
                        pltpu.VMEM((B_HEADS, nb * B_HEAD_DIM + ONES_ROWS, tq), F32)],
        compiler_params=_cparams(("parallel", "arbitrary")),
    )(*args)


def _diff_kernel(*refs, P, nb, tqr, tkp, tkn, lam_init, new_fm):
    tq = nb * tqr
    it = iter(refs)
    qT_ref, g_ref, lq1_ref, lk1_ref, lq2_ref, lk2_ref, sub_ref = (next(it) for _ in range(7))
    past_refs = (next(it), next(it)) if P > 0 else None
    new_refs = (next(it), next(it))
    y_ref = next(it)
    qbd_ref, m_ref, acc_ref = (next(it) for _ in range(3))
    kv = _Keys(nb, tkp, tkn, new_fm, past_refs, new_refs)
    j = pl.program_id(1)
    nmaps = 2 * C_HEADS

    _fill_block_diag(qbd_ref, qT_ref, nb, tqr, nmaps, C_HALF, False)
    m_ref[...] = jnp.full(m_ref.shape, NEG, F32)
    acc_ref[...] = jnp.zeros(acc_ref.shape, F32)

    def attend(k, vt_tiles, pen, fm=False):
        s_all = _key_dot(k, qbd_ref[...], fm)
        vts = [t.astype(BF16) for t in vt_tiles]
        ones = _ones_rows(s_all.shape[0])
        for h in range(C_HEADS):
            vt_h = jnp.concatenate([v[C_HEAD_DIM * h:C_HEAD_DIM * (h + 1), :] for v in vts] + [ones], axis=0)
            for mi in (2 * h, 2 * h + 1):
                _softmax_step(s_all[:, mi * tq:(mi + 1) * tq], pen, vt_h, m_ref, acc_ref, mi)

    if P > 0:
        def pa(t, z):
            r0 = pl.multiple_of(t * tkp, tkp)
            attend(kv.past_cols(0, r0), kv.past_vt(r0), None, True)
            return z
        lax.fori_loop(0, P // tkp, pa, 0, unroll=True)

    def na(i, z):
        r0 = pl.multiple_of(i * tkn, tkn)
        attend(kv.new_rows(0, r0), kv.new_vt(r0), None)
        return z
    lax.fori_loop(0, j, na, 0)
    rd = pl.multiple_of(j * tkn, tkn)
    attend(kv.new_rows(0, rd), kv.new_vt(rd),
           jnp.where(_diag_valid(tkn, tq), 0.0, NEG) if tkn > CHUNK else None)

    lam = (jnp.exp(jnp.sum(lq1_ref[...] * lk1_ref[...], axis=-1, keepdims=True))
           - jnp.exp(jnp.sum(lq2_ref[...] * lk2_ref[...], axis=-1, keepdims=True)) + lam_init)
    outs = []
    nv = nb * C_HEAD_DIM
    for h in range(C_HEADS):
        o = (acc_ref[2 * h, 0:nv, :] / acc_ref[2 * h, nv:nv + 1, :]
             - lam * (acc_ref[2 * h + 1, 0:nv, :] / acc_ref[2 * h + 1, nv:nv + 1, :]))
        o = _own_lanes(o, nb, C_HEAD_DIM, tqr)
        ms = jnp.mean(o * o, axis=0, keepdims=True)
        outs.append(o * lax.rsqrt(ms + EPS))
    o = jnp.concatenate(outs, axis=0).T
    y = (o * sub_ref[...]) * (1.0 - lam_init) * _silu(g_ref[...].reshape(tq, -1))
    y_ref[...] = y.reshape(y_ref.shape)


def _diff(qT, gate, lams, subrow, past, new, *, B, L, P, nb, tqr, layer, new_fm, lam_init):
    hw = gate.shape[-1]
    tq = nb * tqr
    nmaps = 2 * C_HEADS
    feat, seqblk, new_row, new_col, pastspec = _attn_specs(B, L, P, nb, tqr, hw, layer, new_fm, None)
    const = lambda s: pl.BlockSpec(s, lambda b, j: (0, 0))
    in_specs = [feat(hw), seqblk] + [const((1, C_HALF))] * 4 + [const((1, hw))]
    args = [qT, gate, *lams, subrow]
    if P > 0:
        in_specs += [pastspec(hw), pastspec(hw)]
        args += list(past)
    in_specs += [new_row(hw), new_col(hw) if new_fm else new_row(hw)]
    args += list(new)
    return pl.pallas_call(
        functools.partial(_diff_kernel, P=P, nb=nb, tqr=tqr, tkp=_past_tile(P, tq), tkn=tqr,
                          lam_init=lam_init, new_fm=new_fm),
        grid=(B // nb, L // tqr),
        in_specs=in_specs,
        out_specs=seqblk,
        out_shape=jax.ShapeDtypeStruct((B, L, hw), F32),
        scratch_shapes=[pltpu.VMEM((nb * hw, nmaps * tq), BF16),
                        pltpu.VMEM((nmaps, 1, tq), F32),
                        pltpu.VMEM((nmaps, nb * C_HEAD_DIM + ONES_ROWS, tq), F32)],
        compiler_params=_cparams(("parallel", "arbitrary")),
    )(*args)


def _outproj_kernel(ya_ref, yb_ref, yc_ref, x_ref, w_ref, g_ref, o_ref, *, aw, hw):
    mix = jnp.dot(ya_ref[...].astype(BF16), w_ref[0:aw, :], preferred_element_type=F32)
    mix = mix + jnp.dot(yb_ref[...].astype(BF16), w_ref[aw:aw + hw, :], preferred_element_type=F32)
    mix = mix + jnp.dot(yc_ref[...].astype(BF16), w_ref[aw + hw:aw + 2 * hw, :], preferred_element_type=F32)
    ms = jnp.mean(mix * mix, axis=-1, keepdims=True)
    o_ref[...] = x_ref[...] + mix * lax.rsqrt(ms + EPS) * g_ref[...]


def _outproj(ya, yb, yc, x2d, w, g, *, tm):
    T, D = x2d.shape
    aw, hw = D // 2, D // 4
    row = lambda n: pl.BlockSpec((tm, n), lambda i: (i, 0))
    const = lambda s: pl.BlockSpec(s, lambda i: (0, 0))
    return pl.pallas_call(
        functools.partial(_outproj_kernel, aw=aw, hw=hw),
        grid=(T // tm,),
        in_specs=[row(aw), row(hw), row(hw), row(D), const((D, D)), const((1, D))],
        out_specs=row(D),
        out_shape=jax.ShapeDtypeStruct((T, D), F32),
        compiler_params=_cparams(("parallel",)),
    )(ya, yb, yc, x2d, w, g)


_ROW_PROMPT = ("a_x", "a_g", "b_g", "c_g", "b_k", "c_k", "b_ik")
_FEAT_PROMPT = ("b_q", "b_iq", "c_q", "b_k", "b_v", "c_k", "c_v", "b_ik", "b_iw")
_ROW_SAMPLE = ("a_x", "a_g", "b_g", "c_g", "b_k", "b_v", "c_k", "c_v", "b_ik")
_FEAT_SAMPLE = ("b_q", "b_iq", "c_q", "b_ik", "b_iw")
_BF16_OUT = {"b_q", "b_iq", "c_q"}


def _pack_w_in(w_in, D, rows, feats, row_bf16):
    aw, hw = D // 2, D // 4
    sizes = (aw, aw, hw, hw, hw, hw, IDX_HEADS * IDX_DIM, IDX_DIM, IDX_HEADS, hw, hw, hw, hw)
    names = ("a_x", "a_g", "b_q", "b_k", "b_v", "b_g", "b_iq", "b_ik", "b_iw", "c_q", "c_k", "c_v", "c_g")
    scale = {"b_q": B_HEAD_DIM ** -0.5 * LOG2E, "c_q": C_HALF ** -0.5 * LOG2E,
             "b_iw": (IDX_DIM ** -0.5) * (IDX_HEADS ** -0.5)}
    seg, o = {}, 0
    for n, s in zip(names, sizes):
        seg[n] = w_in[..., o:o + s] * scale[n] if n in scale else w_in[..., o:o + s]
        o += s
    depth = w_in.shape[0]

    def cat(parts, mult):
        w = jnp.concatenate([seg[n] for n in parts], axis=-1)
        pad = (-w.shape[-1]) % mult
        return jnp.concatenate([w, jnp.zeros((depth, D, pad), w.dtype)], axis=-1) if pad else w
    wr = cat(rows, 128).astype(BF16)
    wf = jnp.swapaxes(cat(feats, 16), 1, 2).astype(BF16)
    row_plan = tuple((seg[n].shape[-1], BF16 if n in row_bf16 else F32) for n in rows)
    feat_plan = tuple((seg[n].shape[-1], BF16 if n in _BF16_OUT else F32) for n in feats)
    return wr, wf, row_plan, feat_plan


def _block_diag(w):
    depth, nb, n, _ = w.shape
    eye = jnp.eye(nb, dtype=w.dtype)
    return jnp.einsum("lnde,nm->lndme", w, eye).reshape(depth, nb * n, nb * n)


def _prompt_layer(x, B, L, wl, zero_buf, zero_h, lam_init):
    T, D = x.shape
    hw = D // 4
    tm = min(PROJ_ROWS, L)
    out = _inproj(x, wl["norm_pre"], wl["wr_p"], wl["wf_p"], tm=tm, row_plan=wl["plan_p"][0],
                  feat_plan=wl["plan_p"][1], feat_batch=(B, L))
    ax, ag, bg, cg, bk_bf, ck_bf, bik = out[:7]
    bqT, biqT, cqT, bkT, bvT, ckT, cvT, bikT, biwT = out[7:]
    r3 = lambda a: a.reshape(B, L, a.shape[-1])
    ya, nbuf, hlast = _amix(r3(ax), r3(ag), zero_buf, zero_h[:, None, :], wl["a_conv_w"], wl["a_conv_b"],
                            wl["rw"], wl["a_rg_b"], wl["iw"], wl["a_in_b"], wl["a_lambda"], tl=min(L, 512))
    tqr = min(512, L)
    kw = dict(B=B, L=L, P=0, nb=1, tqr=tqr, layer=0, new_fm=True)
    yb = _dsa(bqT, biqT, biwT, r3(bg), None, (r3(bk_bf), bvT, r3(bik)), **kw)
    yc = _diff(cqT, r3(cg), wl["lams"], wl["subrow"], None, (r3(ck_bf), cvT), lam_init=lam_init, **kw)
    x_new = _outproj(ya.reshape(T, -1), yb.reshape(T, hw), yc.reshape(T, hw), x, wl["w_out"], wl["norm_post"],
                     tm=tm)
    heads = lambda t, nh: t.reshape(B, nh, t.shape[1] // nh, L).transpose(0, 3, 1, 2)
    states = (nbuf, hlast[:, 0, :], heads(bkT, B_HEADS), heads(bvT, B_HEADS), bikT.transpose(0, 2, 1),
              heads(ckT, C_HEADS), heads(cvT, C_HEADS))
    return x_new, states


def _sample_layer(x, B, L, P, layer, wl, conv_buf, h0, past, lam_init):
    T, D = x.shape
    hw = D // 4
    tm = min(PROJ_ROWS, T)
    out = _inproj(x, wl["norm_pre"], wl["wr_s"], wl["wf_s"], tm=tm, row_plan=wl["plan_s"][0],
                  feat_plan=wl["plan_s"][1], feat_batch=None)
    ax, ag, bg, cg, bk, bv, ck, cv, bik = out[:9]
    bqT, biqT, cqT, bikT, biwT = out[9:]
    r3 = lambda a: a.reshape(B, L, a.shape[-1])
    ya, nbuf, hlast = _amix(r3(ax), r3(ag), conv_buf, h0[:, None, :], wl["a_conv_w"], wl["a_conv_b"],
                            wl["rw"], wl["a_rg_b"], wl["iw"], wl["a_in_b"], wl["a_lambda"], tl=L)
    nb = 2 if B % 2 == 0 else 1
    kw = dict(B=B, L=L, P=P, nb=nb, tqr=L, layer=layer, new_fm=False)
    pkT, pvT, pkiT, pckT, pcvT = past
    yb = _dsa(bqT, biqT, biwT, r3(bg), (pkT, pvT, pkiT), (r3(bk), r3(bv), r3(bik)), **kw)
    yc = _diff(cqT, r3(cg), wl["lams"], wl["subrow"], (pckT, pcvT), (r3(ck), r3(cv)), lam_init=lam_init, **kw)
    x_new = _outproj(ya.reshape(T, -1), yb.reshape(T, hw), yc.reshape(T, hw), x, wl["w_out"], wl["norm_post"],
                     tm=tm)
    states = (nbuf, hlast[:, 0, :],
              bk.reshape(B, L, B_HEADS, B_HEAD_DIM), bv.reshape(B, L, B_HEADS, B_HEAD_DIM),
              bikT.reshape(IDX_DIM, B, L).transpose(1, 2, 0),
              ck.reshape(B, L, C_HEADS, C_HEAD_DIM), cv.reshape(B, L, C_HEADS, C_HEAD_DIM))
    return x_new, states


def kernel(x_prompt, x_sample, cache_a_conv, state_a_h, cache_b_k, cache_b_v, cache_b_kidx, cache_c_k, cache_c_v, norm_pre, norm_post, w_in, w_out, a_conv_w, a_conv_b, a_rg_w, a_rg_b, a_in_w, a_in_b, a_lambda, c_lam_q1, c_lam_k1, c_lam_q2, c_lam_k2, c_subln):
    Bp, Lp, D = x_prompt.shape
    Bs, Ls, _ = x_sample.shape
    depth = w_in.shape[0]
    P = cache_b_k.shape[2]
    aw = D // 2

    wr_p, wf_p, *plan_p = _pack_w_in(w_in, D, _ROW_PROMPT, _FEAT_PROMPT, {"b_k", "c_k"})
    wr_s, wf_s, *plan_s = _pack_w_in(w_in, D, _ROW_SAMPLE, _FEAT_SAMPLE, set())
    rw = _block_diag(a_rg_w).astype(BF16)
    iw = _block_diag(a_in_w).astype(BF16)
    wo = w_out.astype(BF16)

    fm = lambda c: jnp.transpose(c, (0, 1, 3, 4, 2)).reshape(depth, Bs, -1, P)
    past = (fm(cache_b_k), fm(cache_b_v), jnp.transpose(cache_b_kidx, (0, 1, 3, 2)), fm(cache_c_k), fm(cache_c_v))

    xp = x_prompt.reshape(Bp * Lp, D)
    xs = x_sample.reshape(Bs * Ls, D)
    zero_buf = jnp.zeros((Bp, CONV_W - 1, aw), F32)
    zero_h = jnp.zeros((Bp, aw), F32)
    p_st, s_st = [], []
    for l in range(depth):
        row = lambda a: a[l][None, :]
        wl = {"norm_pre": row(norm_pre), "norm_post": row(norm_post), "w_out": wo[l],
              "wr_p": wr_p[l], "wf_p": wf_p[l], "plan_p": plan_p, "wr_s": wr_s[l], "wf_s": wf_s[l], "plan_s": plan_s,
              "a_conv_w": a_conv_w[l], "a_conv_b": row(a_conv_b), "rw": rw[l], "a_rg_b": row(a_rg_b),
              "iw": iw[l], "a_in_b": row(a_in_b), "a_lambda": row(a_lambda),
              "lams": (row(c_lam_q1), row(c_lam_k1), row(c_lam_q2), row(c_lam_k2)),
              "subrow": jnp.tile(c_subln[l], C_HEADS)[None, :]}
        lam_init = 0.8 - 0.6 * math.exp(-0.3 * l)
        xp, st_p = _prompt_layer(xp, Bp, Lp, wl, zero_buf, zero_h, lam_init)
        xs, st_s = _sample_layer(xs, Bs, Ls, P, l, wl, cache_a_conv[l], state_a_h[l], past, lam_init)
        p_st.append(st_p)
        s_st.append(st_s)

    stk = lambda states, i: jnp.stack([s[i] for s in states], axis=0)
    return (xp.reshape(Bp, Lp, D), xs.reshape(Bs, Ls, D),
            *[stk(p_st, i) for i in range(7)], *[stk(s_st, i) for i in range(7)])
```

```python
import functools
import math

import jax
import jax.numpy as jnp
from jax import lax
from jax.experimental import pallas as pl
from jax.experimental.pallas import tpu as pltpu

F32 = jnp.float32
BF16 = jnp.bfloat16
I32 = jnp.int32

CHUNK = 64
CONV_W = 4
LRU_C = 8.0
B_HEADS = 4
B_HEAD_DIM = 64
IDX_HEADS = 8
IDX_DIM = 32
TOPK_MAX = 256
C_HEADS = 4
C_HALF = 32
C_HEAD_DIM = 2 * C_HALF
EPS = 1e-6
NEG = -1e30
INT_MIN = -2 ** 31
VMEM_LIMIT_BYTES = 56 * 1024 * 1024
KEY_TILE = 256
TILE_ELEMS = 512 * 256
TIE_ROWS = 256
PROJ_ROWS = 512
ONES_ROWS = 16
I16 = jnp.int16
MIN16 = -2 ** 15
LOG2E = math.log2(math.e)


def _past_tile(P, tq):
    rows = max(KEY_TILE, TILE_ELEMS // tq)
    while P % rows:
        rows //= 2
    return rows


def _cparams(sem):
    return pltpu.CompilerParams(dimension_semantics=sem, vmem_limit_bytes=VMEM_LIMIT_BYTES)


def _silu(g):
    return g * jax.nn.sigmoid(g)


def _inproj_kernel(x_ref, g_ref, wr_ref, wf_ref, *out_refs, row_plan, feat_plan):
    x = x_ref[...]
    ms = jnp.mean(x * x, axis=-1, keepdims=True)
    xn = (x * lax.rsqrt(ms + EPS) * g_ref[...]).astype(BF16)
    proj = jnp.dot(xn, wr_ref[...], preferred_element_type=F32)
    projT = lax.dot_general(wf_ref[...], xn, (((1,), (1,)), ((), ())),
                            preferred_element_type=F32)
    refs = iter(out_refs)
    o = 0
    for n, dt in row_plan:
        ref = next(refs)
        ref[...] = proj[:, o:o + n].astype(dt)
        o += n
    o = 0
    for n, dt in feat_plan:
        ref = next(refs)
        ref[...] = projT[o:o + n, :].astype(dt)
        o += n


def _inproj(x2d, g, wr, wf, *, tm, row_plan, feat_plan, feat_batch):
    T, D = x2d.shape
    row = lambda n: pl.BlockSpec((tm, n), lambda i: (i, 0))
    const = lambda s: pl.BlockSpec(s, lambda i: (0, 0))
    out_shape = [jax.ShapeDtypeStruct((T, n), dt) for n, dt in row_plan]
    out_specs = [row(n) for n, _ in row_plan]
    if feat_batch is None:
        out_shape += [jax.ShapeDtypeStruct((n, T), dt) for n, dt in feat_plan]
        out_specs += [pl.BlockSpec((n, tm), lambda i: (0, i)) for n, _ in feat_plan]
    else:
        B, L = feat_batch
        npb = L // tm
        out_shape += [jax.ShapeDtypeStruct((B, n, L), dt) for n, dt in feat_plan]
        out_specs += [pl.BlockSpec((None, n, tm), lambda i: (i // npb, 0, i % npb)) for n, _ in feat_plan]
    return pl.pallas_call(
        functools.partial(_inproj_kernel, row_plan=row_plan, feat_plan=feat_plan),
        grid=(T // tm,),
        in_specs=[row(D), const((1, D)), const(wr.shape), const(wf.shape)],
        out_specs=out_specs, out_shape=out_shape,
        compiler_params=_cparams(("parallel",)),
    )(x2d, g, wr, wf)


def _amix_kernel(ax_ref, ag_ref, buf_ref, h0_ref, cw_ref, cb_ref, rw_ref, rb_ref, iw_ref, ib_ref, lam_ref,
                 ya_ref, nbuf_ref, hlast_ref, xbuf_ref, hc_ref, *, tl, nl, aw):
    li = pl.program_id(1)

    @pl.when(li == 0)
    def _():
        xbuf_ref[0:8, :] = jnp.zeros((8, aw), F32)
        xbuf_ref[8 - (CONV_W - 1):8, :] = buf_ref[...]
        hc_ref[...] = h0_ref[...]

    x = ax_ref[...]
    xbuf_ref[8:8 + tl, :] = x
    cw = cw_ref[...]
    conv = cb_ref[...] + x * cw[CONV_W - 1:CONV_W, :]
    for s in range(1, CONV_W):
        conv = conv + xbuf_ref[8 - s:8 - s + tl, :] * cw[CONV_W - 1 - s:CONV_W - s, :]

    cbf = conv.astype(BF16)
    r = jax.nn.sigmoid(jnp.dot(cbf, rw_ref[...], preferred_element_type=F32) + rb_ref[...])
    ig = jax.nn.sigmoid(jnp.dot(cbf, iw_ref[...], preferred_element_type=F32) + ib_ref[...])
    nl_lam = -lam_ref[...]
    sp = jnp.maximum(nl_lam, 0.0) + jnp.log1p(jnp.exp(-jnp.abs(nl_lam)))
    log_a = (-LRU_C) * r * sp
    a = jnp.exp(log_a)
    u = jnp.sqrt(1.0 - a * a) * (ig * conv)

    row = lax.broadcasted_iota(I32, (tl, aw), 0)
    s = 1
    while s < tl:
        if s % 8:
            keep = row >= s
            u = jnp.where(keep, a * pltpu.roll(u, s, 0) + u, u)
            a = jnp.where(keep, a * pltpu.roll(a, s, 0), a)
        else:
            u = jnp.concatenate([u[:s], a[s:] * u[:tl - s] + u[s:]], axis=0)
            a = jnp.concatenate([a[:s], a[s:] * a[:tl - s]], axis=0)
        s *= 2
    h = a * hc_ref[...] + u
    ya_ref[...] = h * _silu(ag_ref[...])
    hc_ref[...] = h[tl - 1:tl, :]
    xbuf_ref[0:8, :] = x[tl - 8:tl, :]

    @pl.when(li == nl - 1)
    def _():
        nbuf_ref[...] = xbuf_ref[8 + tl - (CONV_W - 1):8 + tl, :]
        hlast_ref[...] = h[tl - 1:tl, :]


def _amix(ax, ag, buf, h0, cw, cb, rw, rb, iw, ib, lam, *, tl):
    B, L, aw = ax.shape
    nl = L // tl
    seq = pl.BlockSpec((None, tl, aw), lambda b, l: (b, l, 0))
    perb = lambda r: pl.BlockSpec((None, r, aw), lambda b, l: (b, 0, 0))
    const = lambda s: pl.BlockSpec(s, lambda b, l: (0, 0))
    return pl.pallas_call(
        functools.partial(_amix_kernel, tl=tl, nl=nl, aw=aw),
        grid=(B, nl),
        in_specs=[seq, seq, perb(CONV_W - 1), perb(1), const((CONV_W, aw)), const((1, aw)),
                  const((aw, aw)), const((1, aw)), const((aw, aw)), const((1, aw)), const((1, aw))],
        out_specs=[seq, perb(CONV_W - 1), perb(1)],
        out_shape=[jax.ShapeDtypeStruct((B, L, aw), F32), jax.ShapeDtypeStruct((B, CONV_W - 1, aw), F32),
                   jax.ShapeDtypeStruct((B, 1, aw), F32)],
        scratch_shapes=[pltpu.VMEM((tl + 8, aw), F32), pltpu.VMEM((1, aw), F32)],
        compiler_params=_cparams(("parallel", "arbitrary")),
    )(ax, ag, buf, h0, cw, cb, rw, rb, iw, ib, lam)


def _col_reduce(x, op):
    rows, n = x.shape
    return op(op(x.reshape(rows // 8, 8, n), axis=0), axis=0, keepdims=True)


def _diag_valid(rows, tq):
    kc = lax.broadcasted_iota(I32, (rows, tq), 0) // CHUNK
    qc = lax.broadcasted_iota(I32, (rows, tq), 1) // CHUNK
    return kc <= qc


def _ones_rows(rows):
    return jnp.where(lax.broadcasted_iota(I32, (ONES_ROWS, rows), 0) == 0, 1.0, 0.0).astype(BF16)


def _softmax_step(s, pen, vt_h, m_ref, acc_ref, idx):
    m_old = m_ref[idx]
    sm = s if pen is None else s + pen
    m_new = jnp.maximum(m_old, _col_reduce(sm, jnp.max))
    alpha = jnp.exp2(m_old - m_new)
    p = jnp.exp2(sm - m_new).astype(BF16)
    pv = jnp.dot(vt_h, p, preferred_element_type=F32)
    acc_ref[idx, :, :] = alpha * acc_ref[idx, :, :] + pv
    m_ref[idx] = m_new


def _for_pairs(n, fn):
    def two(t, z):
        fn(2 * t)
        fn(2 * t + 1)
        return z
    lax.fori_loop(0, lax.shift_right_logical(n, 1), two, 0)

    @pl.when((n & 1) == 1)
    def _():
        fn(n - 1)


def _key_dot(keys, rhs, keys_fm):
    keys = keys.astype(BF16)
    if keys_fm:
        return lax.dot_general(keys, rhs, (((0,), (0,)), ((), ())), preferred_element_type=F32)
    return jnp.dot(keys, rhs, preferred_element_type=F32)


def _fill_block_diag(dst_ref, qT_ref, nb, tqr, nblk, blk, shared_rows):
    tq = nb * tqr
    dst_ref[...] = jnp.zeros(dst_ref.shape, dst_ref.dtype)
    for a in range(nb):
        for m in range(nblk):
            r = a * blk if shared_rows else (a * nblk + m) * blk
            dst_ref[r:r + blk, m * tq + a * tqr:m * tq + (a + 1) * tqr] = \
                qT_ref[blk * m:blk * (m + 1), a * tqr:(a + 1) * tqr]


def _own_lanes(full, nb, dh, tqr):
    if nb == 1:
        return full
    lane_seq = lax.broadcasted_iota(I32, (dh, nb * tqr), 1) // tqr
    out = full[0:dh, :]
    for a in range(1, nb):
        out = jnp.where(lane_seq == a, full[a * dh:(a + 1) * dh, :], out)
    return out


class _Keys:
    def __init__(self, nb, tkp, tkn, new_fm, past_refs, new_refs):
        self.nb, self.tkp, self.tkn, self.new_fm = nb, tkp, tkn, new_fm
        self.past_refs, self.new_refs = past_refs, new_refs

    def past_cols(self, which, r0):
        t = self.past_refs[which][:, :, pl.ds(r0, self.tkp)]
        return t.reshape(t.shape[0] * t.shape[1], t.shape[2])

    def past_vt(self, r0):
        return [self.past_refs[1][a, :, pl.ds(r0, self.tkp)] for a in range(self.nb)]

    def new_rows(self, which, r0):
        ref = self.new_refs[which]
        if self.new_fm:
            return ref[pl.ds(r0, self.tkn), :]
        return jnp.concatenate([ref[a] for a in range(self.nb)], axis=1)

    def new_vt(self, r0):
        ref = self.new_refs[1]
        if self.new_fm:
            return [ref[:, pl.ds(r0, self.tkn)]]
        return [ref[a].T for a in range(self.nb)]


def _dsa_kernel(*refs, P, nb, tqr, tkp, tkn, topk, new_fm):
    tq = nb * tqr
    it = iter(refs)
    qT_ref, qiT_ref, wT_ref, g_ref = next(it), next(it), next(it), next(it)
    past_refs = (next(it), next(it), next(it)) if P > 0 else None
    new_refs = (next(it), next(it), next(it))
    y_ref = next(it)
    hi_ref, lo_ref, qip_ref, qbd_ref, m_ref, acc_ref = (next(it) for _ in range(6))
    kv = _Keys(nb, tkp, tkn, new_fm, past_refs, new_refs)
    j = pl.program_id(1)
    nnew = j + 1

    _fill_block_diag(qip_ref, qiT_ref, nb, tqr, IDX_HEADS, IDX_DIM, True)
    w = wT_ref[...]

    def put_keys(r0, rows, key):
        hi_ref[pl.ds(r0, rows), :] = (key >> 16).astype(I16)
        lo_ref[pl.ds(r0, rows), :] = ((key & 0xFFFF) + MIN16).astype(I16)

    def idx_keys(ki, valid, fm=False):
        s_all = _key_dot(ki, qip_ref[...], fm)
        acc = None
        for h in range(IDX_HEADS):
            t = jnp.maximum(s_all[:, h * tq:(h + 1) * tq], 0.0) * w[h:h + 1, :]
            acc = t if acc is None else acc + t
        bits = lax.bitcast_convert_type(acc, I32)
        sign = bits >> 31
        key = (bits ^ (sign & 0x7FFFFFFF)) - sign
        return key if valid is None else jnp.where(valid, key, INT_MIN)

    def for_tiles(fn, carry, n_new=None):
        if P > 0:
            carry = lax.fori_loop(0, P // tkp, lambda t, c: fn(pl.multiple_of(t * tkp, tkp), tkp, c), carry,
                                  unroll=True)
        return lax.fori_loop(0, nnew if n_new is None else n_new,
                             lambda i, c: fn(P + pl.multiple_of(i * tkn, tkn), tkn, c), carry)

    if P > 0:
        def past_keys(t, c):
            r0 = pl.multiple_of(t * tkp, tkp)
            put_keys(r0, tkp, idx_keys(kv.past_cols(2, r0), None, True))
            return c
        lax.fori_loop(0, P // tkp, past_keys, 0, unroll=True)

    def new_keys(i):
        r0 = pl.multiple_of(i * tkn, tkn)
        put_keys(P + r0, tkn, idx_keys(kv.new_rows(2, r0), None))
    _for_pairs(j, new_keys)
    rd = pl.multiple_of(j * tkn, tkn)
    put_keys(P + rd, tkn, idx_keys(kv.new_rows(2, rd), _diag_valid(tkn, tq) if tkn > CHUNK else None))

    one, zero = jnp.int16(1), jnp.int16(0)

    def fold16(ind):
        parts = [ind[r:r + 16] for r in range(0, ind.shape[0], 16)]
        ways = max(1, min(len(parts), 512 // tq))
        accs = parts[:ways]
        for i in range(ways, len(parts)):
            accs[i % ways] = accs[i % ways] + parts[i]
        return functools.reduce(lambda x, y: x + y, accs)

    def total(parts):
        return jnp.sum(parts.astype(I32), axis=0, keepdims=True)

    def count(pred):
        def fn(r0, rows, c):
            ind = pred(lambda: hi_ref[pl.ds(r0, rows), :], lambda: lo_ref[pl.ds(r0, rows), :], r0, rows)
            return c + fold16(jnp.where(ind, one, zero))
        return total(for_tiles(fn, jnp.zeros((16, tq), I16)))

    def bisect16(pick, kth):
        def step(it, ans):
            cand = ans + jnp.left_shift(jnp.int32(1), 15 - it)
            c16 = cand.astype(I16)
            cnt = count(lambda hi, lo, r0, rows: pick(hi, lo) >= c16)
            return jnp.where(cnt >= kth, cand, ans)
        return lax.fori_loop(0, 16, step, jnp.full((1, tq), MIN16, I32))

    b = bisect16(lambda hi, lo: hi(), topk)
    b16 = b.astype(I16)

    def mask_lo(r0, rows, c):
        hi = hi_ref[pl.ds(r0, rows), :]
        lo_ref[pl.ds(r0, rows), :] = jnp.where(hi == b16, lo_ref[pl.ds(r0, rows), :], jnp.int16(MIN16))
        return c + fold16(jnp.where(hi > b16, one, zero))
    n_above = total(for_tiles(mask_lo, jnp.zeros((16, tq), I16)))
    kth_lo = topk - n_above
    cst = bisect16(lambda hi, lo: lo(), kth_lo)
    c16 = cst.astype(I16)

    def is_tie(hi, lo):
        return (hi == b16) & (lo == c16)
    n_gt = n_above + count(lambda hi, lo, r0, rows: lo() > c16)
    n_tie = count(lambda hi, lo, r0, rows: is_tie(hi(), lo()))
    need = (n_gt + n_tie > topk) & (b > MIN16)
    take = topk - n_gt

    @pl.when(jnp.max(jnp.where(need, 1, 0)) > 0)
    def _():
        take_f = jnp.where(need, take, 2 ** 30).astype(F32)
        tri = jnp.where(lax.broadcasted_iota(I32, (TIE_ROWS, TIE_ROWS), 0)
                        >= lax.broadcasted_iota(I32, (TIE_ROWS, TIE_ROWS), 1), 1.0, 0.0).astype(BF16)

        def demote(r0, rows, seen):
            blocks = []
            for o in range(0, rows, TIE_ROWS):
                n = min(TIE_ROWS, rows - o)
                sl = pl.ds(r0 + o, n)
                hi, lo = hi_ref[sl, :], lo_ref[sl, :]
                tie = is_tie(hi, lo)
                ind = jnp.where(tie, jnp.bfloat16(1), jnp.bfloat16(0))
                blocks.append((sl, n, hi, lo, tie, jnp.dot(tri[0:n, 0:n], ind, preferred_element_type=F32)))
            for sl, n, hi, lo, tie, within in blocks:
                rank = within + seen
                drop = tie & (jnp.where(rank > take_f, 1, 0).astype(I16) != 0)
                hi_ref[sl, :] = jnp.where(drop, jnp.int16(MIN16), hi)
                lo_ref[sl, :] = jnp.where(drop, jnp.int16(MIN16), lo)
                seen = rank[n - 1:n, :]
            return seen
        for_tiles(demote, jnp.zeros((1, tq), F32))

    whole = cst == MIN16
    bs16 = jnp.where(whole, jnp.maximum(b - 1, MIN16), b).astype(I16)
    cs16 = jnp.where(whole, 2 ** 15 - 1, cst - 1).astype(I16)
    open_, shut = jnp.bfloat16(0), jnp.bfloat16(NEG)

    _fill_block_diag(qbd_ref, qT_ref, nb, tqr, B_HEADS, B_HEAD_DIM, False)
    m_ref[...] = jnp.full(m_ref.shape, NEG, F32)
    acc_ref[...] = jnp.zeros(acc_ref.shape, F32)

    def attend(k, vt_tiles, r0, rows, fm=False):
        hi, lo = hi_ref[pl.ds(r0, rows), :], lo_ref[pl.ds(r0, rows), :]
        pen = jnp.where(hi > bs16, open_, jnp.where(lo > cs16, open_, shut)).astype(F32)
        s_all = _key_dot(k, qbd_ref[...], fm)
        vts = [t.astype(BF16) for t in vt_tiles]
        ones = _ones_rows(rows)
        for h in range(B_HEADS):
            vt_h = jnp.concatenate([v[B_HEAD_DIM * h:B_HEAD_DIM * (h + 1), :] for v in vts] + [ones], axis=0)
            _softmax_step(s_all[:, h * tq:(h + 1) * tq], pen, vt_h, m_ref, acc_ref, h)

    if P > 0:
        def pa(t, z):
            r0 = pl.multiple_of(t * tkp, tkp)
            attend(kv.past_cols(0, r0), kv.past_vt(r0), r0, tkp, True)
            return z
        lax.fori_loop(0, P // tkp, pa, 0, unroll=True)

    def na(i):
        r0 = pl.multiple_of(i * tkn, tkn)
        attend(kv.new_rows(0, r0), kv.new_vt(r0), P + r0, tkn)
    _for_pairs(nnew, na)

    nv = nb * B_HEAD_DIM
    oT = jnp.concatenate([_own_lanes(acc_ref[h, 0:nv, :] / acc_ref[h, nv:nv + 1, :], nb, B_HEAD_DIM, tqr)
                          for h in range(B_HEADS)], axis=0)
    y = oT.T * _silu(g_ref[...].reshape(tq, -1))
    y_ref[...] = y.reshape(y_ref.shape)


def _attn_specs(B, L, P, nb, tqr, hw, layer, new_fm, feat_rows):
    tq = nb * tqr
    if new_fm:
        feat = lambda n: pl.BlockSpec((None, n, tq), lambda b, j: (b, 0, j))
        seqblk = pl.BlockSpec((None, tqr, hw), lambda b, j: (b, j, 0))
        new_row = lambda c: pl.BlockSpec((None, L, c), lambda b, j: (b, 0, 0))
        new_col = lambda r: pl.BlockSpec((None, r, L), lambda b, j: (b, 0, 0))
    else:
        feat = lambda n: pl.BlockSpec((n, tq), lambda b, j: (0, b))
        seqblk = pl.BlockSpec((nb, tqr, hw), lambda b, j: (b, 0, 0))
        new_row = lambda c: pl.BlockSpec((nb, L, c), lambda b, j: (b, 0, 0))
        new_col = None
    past = lambda r: pl.BlockSpec((None, nb, r, P), lambda b, j: (layer, b, 0, 0))
    return feat, seqblk, new_row, new_col, past


def _dsa(qT, qiT, wT, gate, past, new, *, B, L, P, nb, tqr, layer, new_fm):
    hw = gate.shape[-1]
    tq = nb * tqr
    Lk = P + L
    topk = min(TOPK_MAX, Lk // 4)
    feat, seqblk, new_row, new_col, pastspec = _attn_specs(B, L, P, nb, tqr, hw, layer, new_fm, None)
    in_specs = [feat(hw), feat(hw), feat(IDX_HEADS), seqblk]
    args = [qT, qiT, wT, gate]
    if P > 0:
        in_specs += [pastspec(hw), pastspec(hw), pastspec(IDX_DIM)]
        args += list(past)
    in_specs += [new_row(hw), new_col(hw) if new_fm else new_row(hw), new_row(IDX_DIM)]
    args += list(new)
    return pl.pallas_call(
        functools.partial(_dsa_kernel, P=P, nb=nb, tqr=tqr, tkp=_past_tile(P, tq), tkn=tqr, topk=topk,
                          new_fm=new_fm),
        grid=(B // nb, L // tqr),
        in_specs=in_specs,
        out_specs=seqblk,
        out_shape=jax.ShapeDtypeStruct((B, L, hw), F32),
        scratch_shapes=[pltpu.VMEM((Lk, tq), I16), pltpu.VMEM((Lk, tq), I16),
                        pltpu.VMEM((nb * IDX_DIM, IDX_HEADS * tq), BF16),
                        pltpu.VMEM((nb * hw, B_HEADS * tq), BF16),
                        pltpu.VMEM((B_HEADS, 1, tq), F32),
                        pltpu.VMEM((B_HEADS, nb * B_HEAD_DIM + ONES_ROWS, tq), F32)],
        compiler_params=_cparams(("parallel", "arbitrary")),
    )(*args)


def _diff_kernel(*refs, P, nb, tqr, tkp, tkn, lam_init, new_fm):
    tq = nb * tqr
    it = iter(refs)
    qT_ref, g_ref, lq1_ref, lk1_ref, lq2_ref, lk2_ref, sub_ref = (next(it) for _ in range(7))
    past_refs = (next(it), next(it)) if P > 0 else None
    new_refs = (next(it), next(it))
    y_ref = next(it)
    qbd_ref, m_ref, acc_ref = (next(it) for _ in range(3))
    kv = _Keys(nb, tkp, tkn, new_fm, past_refs, new_refs)
    j = pl.program_id(1)
    nmaps = 2 * C_HEADS

    _fill_block_diag(qbd_ref, qT_ref, nb, tqr, nmaps, C_HALF, False)
    m_ref[...] = jnp.full(m_ref.shape, NEG, F32)
    acc_ref[...] = jnp.zeros(acc_ref.shape, F32)

    def attend(k, vt_tiles, pen, fm=False):
        s_all = _key_dot(k, qbd_ref[...], fm)
        vts = [t.astype(BF16) for t in vt_tiles]
        ones = _ones_rows(s_all.shape[0])
        for h in range(C_HEADS):
            vt_h = jnp.concatenate([v[C_HEAD_DIM * h:C_HEAD_DIM * (h + 1), :] for v in vts] + [ones], axis=0)
            for mi in (2 * h, 2 * h + 1):
                _softmax_step(s_all[:, mi * tq:(mi + 1) * tq], pen, vt_h, m_ref, acc_ref, mi)

    if P > 0:
        def pa(t, z):
            r0 = pl.multiple_of(t * tkp, tkp)
            attend(kv.past_cols(0, r0), kv.past_vt(r0), None, True)
            return z
        lax.fori_loop(0, P // tkp, pa, 0, unroll=True)

    def na(i):
        r0 = pl.multiple_of(i * tkn, tkn)
        attend(kv.new_rows(0, r0), kv.new_vt(r0), None)
    _for_pairs(j, na)
    rd = pl.multiple_of(j * tkn, tkn)
    attend(kv.new_rows(0, rd), kv.new_vt(rd),
           jnp.where(_diag_valid(tkn, tq), 0.0, NEG) if tkn > CHUNK else None)

    lam = (jnp.exp(jnp.sum(lq1_ref[...] * lk1_ref[...], axis=-1, keepdims=True))
           - jnp.exp(jnp.sum(lq2_ref[...] * lk2_ref[...], axis=-1, keepdims=True)) + lam_init)
    outs = []
    nv = nb * C_HEAD_DIM
    for h in range(C_HEADS):
        o = (acc_ref[2 * h, 0:nv, :] / acc_ref[2 * h, nv:nv + 1, :]
             - lam * (acc_ref[2 * h + 1, 0:nv, :] / acc_ref[2 * h + 1, nv:nv + 1, :]))
        o = _own_lanes(o, nb, C_HEAD_DIM, tqr)
        ms = jnp.mean(o * o, axis=0, keepdims=True)
        outs.append(o * lax.rsqrt(ms + EPS))
    o = jnp.concatenate(outs, axis=0).T
    y = (o * sub_ref[...]) * (1.0 - lam_init) * _silu(g_ref[...].reshape(tq, -1))
    y_ref[...] = y.reshape(y_ref.shape)


def _diff(qT, gate, lams, subrow, past, new, *, B, L, P, nb, tqr, layer, new_fm, lam_init):
    hw = gate.shape[-1]
    tq = nb * tqr
    nmaps = 2 * C_HEADS
    feat, seqblk, new_row, new_col, pastspec = _attn_specs(B, L, P, nb, tqr, hw, layer, new_fm, None)
    const = lambda s: pl.BlockSpec(s, lambda b, j: (0, 0))
    in_specs = [feat(hw), seqblk] + [const((1, C_HALF))] * 4 + [const((1, hw))]
    args = [qT, gate, *lams, subrow]
    if P > 0:
        in_specs += [pastspec(hw), pastspec(hw)]
        args += list(past)
    in_specs += [new_row(hw), new_col(hw) if new_fm else new_row(hw)]
    args += list(new)
    return pl.pallas_call(
        functools.partial(_diff_kernel, P=P, nb=nb, tqr=tqr, tkp=_past_tile(P, tq), tkn=tqr,
                          lam_init=lam_init, new_fm=new_fm),
        grid=(B // nb, L // tqr),
        in_specs=in_specs,
        out_specs=seqblk,
        out_shape=jax.ShapeDtypeStruct((B, L, hw), F32),
        scratch_shapes=[pltpu.VMEM((nb * hw, nmaps * tq), BF16),
                        pltpu.VMEM((nmaps, 1, tq), F32),
                        pltpu.VMEM((nmaps, nb * C_HEAD_DIM + ONES_ROWS, tq), F32)],
        compiler_params=_cparams(("parallel", "arbitrary")),
    )(*args)


def _outproj_kernel(ya_ref, yb_ref, yc_ref, x_ref, w_ref, g_ref, o_ref, *, aw, hw):
    mix = jnp.dot(ya_ref[...].astype(BF16), w_ref[0:aw, :], preferred_element_type=F32)
    mix = mix + jnp.dot(yb_ref[...].astype(BF16), w_ref[aw:aw + hw, :], preferred_element_type=F32)
    mix = mix + jnp.dot(yc_ref[...].astype(BF16), w_ref[aw + hw:aw + 2 * hw, :], preferred_element_type=F32)
    ms = jnp.mean(mix * mix, axis=-1, keepdims=True)
    o_ref[...] = x_ref[...] + mix * lax.rsqrt(ms + EPS) * g_ref[...]


def _outproj(ya, yb, yc, x2d, w, g, *, tm):
    T, D = x2d.shape
    aw, hw = D // 2, D // 4
    row = lambda n: pl.BlockSpec((tm, n), lambda i: (i, 0))
    const = lambda s: pl.BlockSpec(s, lambda i: (0, 0))
    return pl.pallas_call(
        functools.partial(_outproj_kernel, aw=aw, hw=hw),
        grid=(T // tm,),
        in_specs=[row(aw), row(hw), row(hw), row(D), const((D, D)), const((1, D))],
        out_specs=row(D),
        out_shape=jax.ShapeDtypeStruct((T, D), F32),
        compiler_params=_cparams(("parallel",)),
    )(ya, yb, yc, x2d, w, g)


_ROW_PROMPT = ("a_x", "a_g", "b_g", "c_g", "b_k", "c_k", "b_ik")
_FEAT_PROMPT = ("b_q", "b_iq", "c_q", "b_k", "b_v", "c_k", "c_v", "b_ik", "b_iw")
_ROW_SAMPLE = ("a_x", "a_g", "b_g", "c_g", "b_k", "b_v", "c_k", "c_v", "b_ik")
_FEAT_SAMPLE = ("b_q", "b_iq", "c_q", "b_ik", "b_iw")
_BF16_OUT = {"b_q", "b_iq", "c_q"}


def _pack_w_in(w_in, D, rows, feats, row_bf16):
    aw, hw = D // 2, D // 4
    sizes = (aw, aw, hw, hw, hw, hw, IDX_HEADS * IDX_DIM, IDX_DIM, IDX_HEADS, hw, hw, hw, hw)
    names = ("a_x", "a_g", "b_q", "b_k", "b_v", "b_g", "b_iq", "b_ik", "b_iw", "c_q", "c_k", "c_v", "c_g")
    scale = {"b_q": B_HEAD_DIM ** -0.5 * LOG2E, "c_q": C_HALF ** -0.5 * LOG2E,
             "b_iw": (IDX_DIM ** -0.5) * (IDX_HEADS ** -0.5)}
    seg, o = {}, 0
    for n, s in zip(names, sizes):
        seg[n] = w_in[..., o:o + s] * scale[n] if n in scale else w_in[..., o:o + s]
        o += s
    depth = w_in.shape[0]

    def cat(parts, mult):
        w = jnp.concatenate([seg[n] for n in parts], axis=-1)
        pad = (-w.shape[-1]) % mult
        return jnp.concatenate([w, jnp.zeros((depth, D, pad), w.dtype)], axis=-1) if pad else w
    wr = cat(rows, 128).astype(BF16)
    wf = jnp.swapaxes(cat(feats, 16), 1, 2).astype(BF16)
    row_plan = tuple((seg[n].shape[-1], BF16 if n in row_bf16 else F32) for n in rows)
    feat_plan = tuple((seg[n].shape[-1], BF16 if n in _BF16_OUT else F32) for n in feats)
    return wr, wf, row_plan, feat_plan


def _block_diag(w):
    depth, nb, n, _ = w.shape
    eye = jnp.eye(nb, dtype=w.dtype)
    return jnp.einsum("lnde,nm->lndme", w, eye).reshape(depth, nb * n, nb * n)


def _prompt_layer(x, B, L, wl, zero_buf, zero_h, lam_init):
    T, D = x.shape
    hw = D // 4
    tm = min(PROJ_ROWS, L)
    out = _inproj(x, wl["norm_pre"], wl["wr_p"], wl["wf_p"], tm=tm, row_plan=wl["plan_p"][0],
                  feat_plan=wl["plan_p"][1], feat_batch=(B, L))
    ax, ag, bg, cg, bk_bf, ck_bf, bik = out[:7]
    bqT, biqT, cqT, bkT, bvT, ckT, cvT, bikT, biwT = out[7:]
    r3 = lambda a: a.reshape(B, L, a.shape[-1])
    ya, nbuf, hlast = _amix(r3(ax), r3(ag), zero_buf, zero_h[:, None, :], wl["a_conv_w"], wl["a_conv_b"],
                            wl["rw"], wl["a_rg_b"], wl["iw"], wl["a_in_b"], wl["a_lambda"], tl=min(L, 512))
    tqr = min(512, L)
    kw = dict(B=B, L=L, P=0, nb=1, tqr=tqr, layer=0, new_fm=True)
    yb = _dsa(bqT, biqT, biwT, r3(bg), None, (r3(bk_bf), bvT, r3(bik)), **kw)
    yc = _diff(cqT, r3(cg), wl["lams"], wl["subrow"], None, (r3(ck_bf), cvT), lam_init=lam_init, **kw)
    x_new = _outproj(ya.reshape(T, -1), yb.reshape(T, hw), yc.reshape(T, hw), x, wl["w_out"], wl["norm_post"],
                     tm=tm)
    heads = lambda t, nh: t.reshape(B, nh, t.shape[1] // nh, L).transpose(0, 3, 1, 2)
    states = (nbuf, hlast[:, 0, :], heads(bkT, B_HEADS), heads(bvT, B_HEADS), bikT.transpose(0, 2, 1),
              heads(ckT, C_HEADS), heads(cvT, C_HEADS))
    return x_new, states


def _sample_layer(x, B, L, P, layer, wl, conv_buf, h0, past, lam_init):
    T, D = x.shape
    hw = D // 4
    tm = min(PROJ_ROWS, T)
    out = _inproj(x, wl["norm_pre"], wl["wr_s"], wl["wf_s"], tm=tm, row_plan=wl["plan_s"][0],
                  feat_plan=wl["plan_s"][1], feat_batch=None)
    ax, ag, bg, cg, bk, bv, ck, cv, bik = out[:9]
    bqT, biqT, cqT, bikT, biwT = out[9:]
    r3 = lambda a: a.reshape(B, L, a.shape[-1])
    ya, nbuf, hlast = _amix(r3(ax), r3(ag), conv_buf, h0[:, None, :], wl["a_conv_w"], wl["a_conv_b"],
                            wl["rw"], wl["a_rg_b"], wl["iw"], wl["a_in_b"], wl["a_lambda"], tl=L)
    nb = 2 if B % 2 == 0 else 1
    kw = dict(B=B, L=L, P=P, nb=nb, tqr=L, layer=layer, new_fm=False)
    pkT, pvT, pkiT, pckT, pcvT = past
    yb = _dsa(bqT, biqT, biwT, r3(bg), (pkT, pvT, pkiT), (r3(bk), r3(bv), r3(bik)), **kw)
    yc = _diff(cqT, r3(cg), wl["lams"], wl["subrow"], (pckT, pcvT), (r3(ck), r3(cv)), lam_init=lam_init, **kw)
    x_new = _outproj(ya.reshape(T, -1), yb.reshape(T, hw), yc.reshape(T, hw), x, wl["w_out"], wl["norm_post"],
                     tm=tm)
    states = (nbuf, hlast[:, 0, :],
              bk.reshape(B, L, B_HEADS, B_HEAD_DIM), bv.reshape(B, L, B_HEADS, B_HEAD_DIM),
              bikT.reshape(IDX_DIM, B, L).transpose(1, 2, 0),
              ck.reshape(B, L, C_HEADS, C_HEAD_DIM), cv.reshape(B, L, C_HEADS, C_HEAD_DIM))
    return x_new, states


def kernel(x_prompt, x_sample, cache_a_conv, state_a_h, cache_b_k, cache_b_v, cache_b_kidx, cache_c_k, cache_c_v, norm_pre, norm_post, w_in, w_out, a_conv_w, a_conv_b, a_rg_w, a_rg_b, a_in_w, a_in_b, a_lambda, c_lam_q1, c_lam_k1, c_lam_q2, c_lam_k2, c_subln):
    Bp, Lp, D = x_prompt.shape
    Bs, Ls, _ = x_sample.shape
    depth = w_in.shape[0]
    P = cache_b_k.shape[2]
    aw = D // 2

    wr_p, wf_p, *plan_p = _pack_w_in(w_in, D, _ROW_PROMPT, _FEAT_PROMPT, {"b_k", "c_k"})
    wr_s, wf_s, *plan_s = _pack_w_in(w_in, D, _ROW_SAMPLE, _FEAT_SAMPLE, set())
    rw = _block_diag(a_rg_w).astype(BF16)
    iw = _block_diag(a_in_w).astype(BF16)
    wo = w_out.astype(BF16)

    fm = lambda c: jnp.transpose(c, (0, 1, 3, 4, 2)).reshape(depth, Bs, -1, P)
    past = (fm(cache_b_k), fm(cache_b_v), jnp.transpose(cache_b_kidx, (0, 1, 3, 2)), fm(cache_c_k), fm(cache_c_v))

    xp = x_prompt.reshape(Bp * Lp, D)
    xs = x_sample.reshape(Bs * Ls, D)
    zero_buf = jnp.zeros((Bp, CONV_W - 1, aw), F32)
    zero_h = jnp.zeros((Bp, aw), F32)
    p_st, s_st = [], []
    for l in range(depth):
        row = lambda a: a[l][None, :]
        wl = {"norm_pre": row(norm_pre), "norm_post": row(norm_post), "w_out": wo[l],
              "wr_p": wr_p[l], "wf_p": wf_p[l], "plan_p": plan_p, "wr_s": wr_s[l], "wf_s": wf_s[l], "plan_s": plan_s,
              "a_conv_w": a_conv_w[l], "a_conv_b": row(a_conv_b), "rw": rw[l], "a_rg_b": row(a_rg_b),
              "iw": iw[l], "a_in_b": row(a_in_b), "a_lambda": row(a_lambda),
              "lams": (row(c_lam_q1), row(c_lam_k1), row(c_lam_q2), row(c_lam_k2)),
              "subrow": jnp.tile(c_subln[l], C_HEADS)[None, :]}
        lam_init = 0.8 - 0.6 * math.exp(-0.3 * l)
        xp, st_p = _prompt_layer(xp, Bp, Lp, wl, zero_buf, zero_h, lam_init)
        xs, st_s = _sample_layer(xs, Bs, Ls, P, l, wl, cache_a_conv[l], state_a_h[l], past, lam_init)
        p_st.append(st_p)
        s_st.append(st_s)

    stk = lambda states, i: jnp.stack([s[i] for s in states], axis=0)
    return (xp.reshape(Bp, Lp, D), xs.reshape(Bs, Ls, D),
            *[stk(p_st, i) for i in range(7)], *[stk(s_st, i) for i in range(7)])
```

```python
import functools
import math

import jax
import jax.numpy as jnp
from jax import lax
from jax.experimental import pallas as pl
from jax.experimental.pallas import tpu as pltpu

F32 = jnp.float32
BF16 = jnp.bfloat16
I32 = jnp.int32

CHUNK = 64
CONV_W = 4
LRU_C = 8.0
B_HEADS = 4
B_HEAD_DIM = 64
IDX_HEADS = 8
IDX_DIM = 32
TOPK_MAX = 256
C_HEADS = 4
C_HALF = 32
C_HEAD_DIM = 2 * C_HALF
EPS = 1e-6
NEG = -1e30
INT_MIN = -2 ** 31
VMEM_LIMIT_BYTES = 56 * 1024 * 1024
KEY_TILE = 256
TILE_ELEMS = 512 * 256
TIE_ROWS = 256
PROJ_ROWS = 512
ONES_ROWS = 16
I16 = jnp.int16
MIN16 = -2 ** 15
LOG2E = math.log2(math.e)


def _past_tile(P, tq):
    rows = max(KEY_TILE, TILE_ELEMS // tq)
    while P % rows:
        rows //= 2
    return rows


def _cparams(sem):
    return pltpu.CompilerParams(dimension_semantics=sem, vmem_limit_bytes=VMEM_LIMIT_BYTES)


def _silu(g):
    return g * jax.nn.sigmoid(g)


def _inproj_kernel(x_ref, g_ref, wr_ref, wf_ref, *rest, row_plan, feat_plan, n_alias):
    out_refs = rest[n_alias:]
    x = x_ref[...]
    ms = jnp.mean(x * x, axis=-1, keepdims=True)
    xn = (x * lax.rsqrt(ms + EPS) * g_ref[...]).astype(BF16)
    proj = jnp.dot(xn, wr_ref[...], preferred_element_type=F32)
    projT = lax.dot_general(wf_ref[...], xn, (((1,), (1,)), ((), ())),
                            preferred_element_type=F32)
    refs = iter(out_refs)
    o = 0
    for n, dt in row_plan:
        ref = next(refs)
        ref[...] = proj[:, o:o + n].astype(dt)
        o += n
    o = 0
    for n, dt, _ in feat_plan:
        ref = next(refs)
        ref[...] = projT[o:o + n, :].astype(dt)
        o += n


def _inproj(x2d, g, wr, wf, *, tm, row_plan, feat_plan, feat_batch, layer=0, stacks=()):
    T, D = x2d.shape
    row = lambda n: pl.BlockSpec((tm, n), lambda i: (i, 0))
    const = lambda s: pl.BlockSpec(s, lambda i: (0, 0))
    out_shape = [jax.ShapeDtypeStruct((T, n), dt) for n, dt in row_plan]
    out_specs = [row(n) for n, _ in row_plan]
    aliases, stack_it = {}, iter(stacks)
    for n, dt, stacked in feat_plan:
        if feat_batch is None:
            out_shape.append(jax.ShapeDtypeStruct((n, T), dt))
            out_specs.append(pl.BlockSpec((n, tm), lambda i: (0, i)))
            continue
        B, L = feat_batch
        npb = L // tm
        if stacked:
            buf = next(stack_it)
            aliases[4 + len(aliases)] = len(out_shape)
            out_shape.append(jax.ShapeDtypeStruct(buf.shape, dt))
            out_specs.append(pl.BlockSpec((None, None, n, tm), lambda i: (layer, i // npb, 0, i % npb)))
        else:
            out_shape.append(jax.ShapeDtypeStruct((B, n, L), dt))
            out_specs.append(pl.BlockSpec((None, n, tm), lambda i: (i // npb, 0, i % npb)))
    return pl.pallas_call(
        functools.partial(_inproj_kernel, row_plan=row_plan, feat_plan=feat_plan, n_alias=len(stacks)),
        grid=(T // tm,),
        in_specs=[row(D), const((1, D)), const(wr.shape), const(wf.shape)]
        + [pl.BlockSpec(memory_space=pl.ANY)] * len(stacks),
        out_specs=out_specs, out_shape=out_shape,
        input_output_aliases=aliases,
        compiler_params=_cparams(("parallel",)),
    )(x2d, g, wr, wf, *stacks)


def _amix_kernel(ax_ref, ag_ref, buf_ref, h0_ref, cw_ref, cb_ref, rw_ref, rb_ref, iw_ref, ib_ref, lam_ref,
                 ya_ref, nbuf_ref, hlast_ref, xbuf_ref, hc_ref, *, tl, nl, aw):
    li = pl.program_id(1)

    @pl.when(li == 0)
    def _():
        xbuf_ref[0:8, :] = jnp.zeros((8, aw), F32)
        xbuf_ref[8 - (CONV_W - 1):8, :] = buf_ref[...]
        hc_ref[...] = h0_ref[...]

    x = ax_ref[...]
    xbuf_ref[8:8 + tl, :] = x
    cw = cw_ref[...]
    conv = cb_ref[...] + x * cw[CONV_W - 1:CONV_W, :]
    for s in range(1, CONV_W):
        conv = conv + xbuf_ref[8 - s:8 - s + tl, :] * cw[CONV_W - 1 - s:CONV_W - s, :]

    cbf = conv.astype(BF16)
    r = jax.nn.sigmoid(jnp.dot(cbf, rw_ref[...], preferred_element_type=F32) + rb_ref[...])
    ig = jax.nn.sigmoid(jnp.dot(cbf, iw_ref[...], preferred_element_type=F32) + ib_ref[...])
    nl_lam = -lam_ref[...]
    sp = jnp.maximum(nl_lam, 0.0) + jnp.log1p(jnp.exp(-jnp.abs(nl_lam)))
    log_a = (-LRU_C) * r * sp
    a = jnp.exp(log_a)
    u = jnp.sqrt(1.0 - a * a) * (ig * conv)

    row = lax.broadcasted_iota(I32, (tl, aw), 0)
    s = 1
    while s < tl:
        if s % 8:
            keep = row >= s
            u = jnp.where(keep, a * pltpu.roll(u, s, 0) + u, u)
            a = jnp.where(keep, a * pltpu.roll(a, s, 0), a)
        else:
            u = jnp.concatenate([u[:s], a[s:] * u[:tl - s] + u[s:]], axis=0)
            a = jnp.concatenate([a[:s], a[s:] * a[:tl - s]], axis=0)
        s *= 2
    h = a * hc_ref[...] + u
    ya_ref[...] = h * _silu(ag_ref[...])
    hc_ref[...] = h[tl - 1:tl, :]
    xbuf_ref[0:8, :] = x[tl - 8:tl, :]

    @pl.when(li == nl - 1)
    def _():
        nbuf_ref[...] = xbuf_ref[8 + tl - (CONV_W - 1):8 + tl, :]
        hlast_ref[...] = h[tl - 1:tl, :]


def _amix(ax, ag, buf, h0, cw, cb, rw, rb, iw, ib, lam, *, tl):
    B, L, aw = ax.shape
    nl = L // tl
    seq = pl.BlockSpec((None, tl, aw), lambda b, l: (b, l, 0))
    perb = lambda r: pl.BlockSpec((None, r, aw), lambda b, l: (b, 0, 0))
    const = lambda s: pl.BlockSpec(s, lambda b, l: (0, 0))
    return pl.pallas_call(
        functools.partial(_amix_kernel, tl=tl, nl=nl, aw=aw),
        grid=(B, nl),
        in_specs=[seq, seq, perb(CONV_W - 1), perb(1), const((CONV_W, aw)), const((1, aw)),
                  const((aw, aw)), const((1, aw)), const((aw, aw)), const((1, aw)), const((1, aw))],
        out_specs=[seq, perb(CONV_W - 1), perb(1)],
        out_shape=[jax.ShapeDtypeStruct((B, L, aw), F32), jax.ShapeDtypeStruct((B, CONV_W - 1, aw), F32),
                   jax.ShapeDtypeStruct((B, 1, aw), F32)],
        scratch_shapes=[pltpu.VMEM((tl + 8, aw), F32), pltpu.VMEM((1, aw), F32)],
        compiler_params=_cparams(("parallel", "arbitrary")),
    )(ax, ag, buf, h0, cw, cb, rw, rb, iw, ib, lam)


def _col_reduce(x, op):
    rows, n = x.shape
    return op(op(x.reshape(rows // 8, 8, n), axis=0), axis=0, keepdims=True)


def _diag_valid(rows, tq):
    kc = lax.broadcasted_iota(I32, (rows, tq), 0) // CHUNK
    qc = lax.broadcasted_iota(I32, (rows, tq), 1) // CHUNK
    return kc <= qc


def _ones_rows(rows):
    return jnp.where(lax.broadcasted_iota(I32, (ONES_ROWS, rows), 0) == 0, 1.0, 0.0).astype(BF16)


def _softmax_step(s, pen, vt_h, m_ref, acc_ref, idx):
    m_old = m_ref[idx]
    sm = s if pen is None else s + pen
    m_new = jnp.maximum(m_old, _col_reduce(sm, jnp.max))
    alpha = jnp.exp2(m_old - m_new)
    p = jnp.exp2(sm - m_new).astype(BF16)
    pv = jnp.dot(vt_h, p, preferred_element_type=F32)
    acc_ref[idx, :, :] = alpha * acc_ref[idx, :, :] + pv
    m_ref[idx] = m_new


def _for_pairs(n, fn):
    def two(t, z):
        fn(2 * t)
        fn(2 * t + 1)
        return z
    lax.fori_loop(0, lax.shift_right_logical(n, 1), two, 0)

    @pl.when((n & 1) == 1)
    def _():
        fn(n - 1)


def _key_dot(keys, rhs, keys_fm):
    keys = keys.astype(BF16)
    if keys_fm:
        return lax.dot_general(keys, rhs, (((0,), (0,)), ((), ())), preferred_element_type=F32)
    return jnp.dot(keys, rhs, preferred_element_type=F32)


def _fill_block_diag(dst_ref, qT_ref, nb, tqr, nblk, blk, shared_rows):
    tq = nb * tqr
    dst_ref[...] = jnp.zeros(dst_ref.shape, dst_ref.dtype)
    for a in range(nb):
        for m in range(nblk):
            r = a * blk if shared_rows else (a * nblk + m) * blk
            dst_ref[r:r + blk, m * tq + a * tqr:m * tq + (a + 1) * tqr] = \
                qT_ref[blk * m:blk * (m + 1), a * tqr:(a + 1) * tqr]


def _own_lanes(full, nb, dh, tqr):
    if nb == 1:
        return full
    lane_seq = lax.broadcasted_iota(I32, (dh, nb * tqr), 1) // tqr
    out = full[0:dh, :]
    for a in range(1, nb):
        out = jnp.where(lane_seq == a, full[a * dh:(a + 1) * dh, :], out)
    return out


class _Keys:
    def __init__(self, nb, tkp, tkn, new_fm, past_refs, new_refs):
        self.nb, self.tkp, self.tkn, self.new_fm = nb, tkp, tkn, new_fm
        self.past_refs, self.new_refs = past_refs, new_refs

    def past_cols(self, which, r0):
        t = self.past_refs[which][:, :, pl.ds(r0, self.tkp)]
        return t.reshape(t.shape[0] * t.shape[1], t.shape[2])

    def past_vt(self, r0):
        return [self.past_refs[1][a, :, pl.ds(r0, self.tkp)] for a in range(self.nb)]

    def new_rows(self, which, r0):
        ref = self.new_refs[which]
        if self.new_fm:
            return ref[pl.ds(r0, self.tkn), :]
        return jnp.concatenate([ref[a] for a in range(self.nb)], axis=1)

    def new_vt(self, r0):
        ref = self.new_refs[1]
        if self.new_fm:
            return [ref[:, pl.ds(r0, self.tkn)]]
        return [ref[a].T for a in range(self.nb)]


def _dsa_kernel(*refs, P, nb, tqr, tkp, tkn, topk, new_fm):
    tq = nb * tqr
    it = iter(refs)
    qT_ref, qiT_ref, wT_ref, g_ref = next(it), next(it), next(it), next(it)
    past_refs = (next(it), next(it), next(it)) if P > 0 else None
    new_refs = (next(it), next(it), next(it))
    y_ref = next(it)
    hi_ref, lo_ref, qip_ref, qbd_ref, m_ref, acc_ref = (next(it) for _ in range(6))
    kv = _Keys(nb, tkp, tkn, new_fm, past_refs, new_refs)
    j = pl.program_id(1)
    nnew = j + 1

    _fill_block_diag(qip_ref, qiT_ref, nb, tqr, IDX_HEADS, IDX_DIM, True)
    w = wT_ref[...]

    def put_keys(r0, rows, key):
        hi_ref[pl.ds(r0, rows), :] = (key >> 16).astype(I16)
        lo_ref[pl.ds(r0, rows), :] = ((key & 0xFFFF) + MIN16).astype(I16)

    def idx_keys(ki, valid, fm=False):
        s_all = _key_dot(ki, qip_ref[...], fm)
        acc = None
        for h in range(IDX_HEADS):
            t = jnp.maximum(s_all[:, h * tq:(h + 1) * tq], 0.0) * w[h:h + 1, :]
            acc = t if acc is None else acc + t
        bits = lax.bitcast_convert_type(acc, I32)
        sign = bits >> 31
        key = (bits ^ (sign & 0x7FFFFFFF)) - sign
        return key if valid is None else jnp.where(valid, key, INT_MIN)

    def for_tiles(fn, carry, n_new=None):
        if P > 0:
            carry = lax.fori_loop(0, P // tkp, lambda t, c: fn(pl.multiple_of(t * tkp, tkp), tkp, c), carry,
                                  unroll=True)
        return lax.fori_loop(0, nnew if n_new is None else n_new,
                             lambda i, c: fn(P + pl.multiple_of(i * tkn, tkn), tkn, c), carry)

    if P > 0:
        def past_keys(t, c):
            r0 = pl.multiple_of(t * tkp, tkp)
            put_keys(r0, tkp, idx_keys(kv.past_cols(2, r0), None, True))
            return c
        lax.fori_loop(0, P // tkp, past_keys, 0, unroll=True)

    def new_keys(i):
        r0 = pl.multiple_of(i * tkn, tkn)
        put_keys(P + r0, tkn, idx_keys(kv.new_rows(2, r0), None))
    _for_pairs(j, new_keys)
    rd = pl.multiple_of(j * tkn, tkn)
    put_keys(P + rd, tkn, idx_keys(kv.new_rows(2, rd), _diag_valid(tkn, tq) if tkn > CHUNK else None))

    one, zero = jnp.int16(1), jnp.int16(0)

    def fold16(ind):
        parts = [ind[r:r + 16] for r in range(0, ind.shape[0], 16)]
        ways = max(1, min(len(parts), 512 // tq))
        accs = parts[:ways]
        for i in range(ways, len(parts)):
            accs[i % ways] = accs[i % ways] + parts[i]
        return functools.reduce(lambda x, y: x + y, accs)

    def total(parts):
        return jnp.sum(parts.astype(I32), axis=0, keepdims=True)

    def count(pred):
        def fn(r0, rows, c):
            ind = pred(lambda: hi_ref[pl.ds(r0, rows), :], lambda: lo_ref[pl.ds(r0, rows), :], r0, rows)
            return c + fold16(jnp.where(ind, one, zero))
        return total(for_tiles(fn, jnp.zeros((16, tq), I16)))

    def bisect16(pick, kth):
        def step(it, ans):
            cand = ans + jnp.left_shift(jnp.int32(1), 15 - it)
            c16 = cand.astype(I16)
            cnt = count(lambda hi, lo, r0, rows: pick(hi, lo) >= c16)
            return jnp.where(cnt >= kth, cand, ans)
        return lax.fori_loop(0, 16, step, jnp.full((1, tq), MIN16, I32))

    b = bisect16(lambda hi, lo: hi(), topk)
    b16 = b.astype(I16)

    def mask_lo(r0, rows, c):
        hi = hi_ref[pl.ds(r0, rows), :]
        lo_ref[pl.ds(r0, rows), :] = jnp.where(hi == b16, lo_ref[pl.ds(r0, rows), :], jnp.int16(MIN16))
        return c + fold16(jnp.where(hi > b16, one, zero))
    n_above = total(for_tiles(mask_lo, jnp.zeros((16, tq), I16)))
    kth_lo = topk - n_above
    cst = bisect16(lambda hi, lo: lo(), kth_lo)
    c16 = cst.astype(I16)

    def is_tie(hi, lo):
        return (hi == b16) & (lo == c16)
    def gt_and_tie(r0, rows, c):
        hi, lo = hi_ref[pl.ds(r0, rows), :], lo_ref[pl.ds(r0, rows), :]
        return (c[0] + fold16(jnp.where(lo > c16, one, zero)), c[1] + fold16(jnp.where(is_tie(hi, lo), one, zero)))
    gt_parts, tie_parts = for_tiles(gt_and_tie, (jnp.zeros((16, tq), I16), jnp.zeros((16, tq), I16)))
    n_gt = n_above + total(gt_parts)
    n_tie = total(tie_parts)
    need = (n_gt + n_tie > topk) & (b > MIN16)
    take = topk - n_gt

    @pl.when(jnp.max(jnp.where(need, 1, 0)) > 0)
    def _():
        take_f = jnp.where(need, take, 2 ** 30).astype(F32)
        tri = jnp.where(lax.broadcasted_iota(I32, (TIE_ROWS, TIE_ROWS), 0)
                        >= lax.broadcasted_iota(I32, (TIE_ROWS, TIE_ROWS), 1), 1.0, 0.0).astype(BF16)

        def demote(r0, rows, seen):
            blocks = []
            for o in range(0, rows, TIE_ROWS):
                n = min(TIE_ROWS, rows - o)
                sl = pl.ds(r0 + o, n)
                hi, lo = hi_ref[sl, :], lo_ref[sl, :]
                tie = is_tie(hi, lo)
                ind = jnp.where(tie, jnp.bfloat16(1), jnp.bfloat16(0))
                blocks.append((sl, n, hi, lo, tie, jnp.dot(tri[0:n, 0:n], ind, preferred_element_type=F32)))
            for sl, n, hi, lo, tie, within in blocks:
                rank = within + seen
                drop = tie & (jnp.where(rank > take_f, 1, 0).astype(I16) != 0)
                hi_ref[sl, :] = jnp.where(drop, jnp.int16(MIN16), hi)
                lo_ref[sl, :] = jnp.where(drop, jnp.int16(MIN16), lo)
                seen = rank[n - 1:n, :]
            return seen
        for_tiles(demote, jnp.zeros((1, tq), F32))

    whole = cst == MIN16
    bs16 = jnp.where(whole, jnp.maximum(b - 1, MIN16), b).astype(I16)
    cs16 = jnp.where(whole, 2 ** 15 - 1, cst - 1).astype(I16)
    open_, shut = jnp.bfloat16(0), jnp.bfloat16(NEG)

    _fill_block_diag(qbd_ref, qT_ref, nb, tqr, B_HEADS, B_HEAD_DIM, False)
    m_ref[...] = jnp.full(m_ref.shape, NEG, F32)
    acc_ref[...] = jnp.zeros(acc_ref.shape, F32)

    def attend(k, vt_tiles, r0, rows, fm=False):
        hi, lo = hi_ref[pl.ds(r0, rows), :], lo_ref[pl.ds(r0, rows), :]
        pen = jnp.where(hi > bs16, open_, jnp.where(lo > cs16, open_, shut)).astype(F32)
        s_all = _key_dot(k, qbd_ref[...], fm)
        vts = [t.astype(BF16) for t in vt_tiles]
        ones = _ones_rows(rows)
        for h in range(B_HEADS):
            vt_h = jnp.concatenate([v[B_HEAD_DIM * h:B_HEAD_DIM * (h + 1), :] for v in vts] + [ones], axis=0)
            _softmax_step(s_all[:, h * tq:(h + 1) * tq], pen, vt_h, m_ref, acc_ref, h)

    if P > 0:
        def pa(t, z):
            r0 = pl.multiple_of(t * tkp, tkp)
            attend(kv.past_cols(0, r0), kv.past_vt(r0), r0, tkp, True)
            return z
        lax.fori_loop(0, P // tkp, pa, 0, unroll=True)

    def na(i):
        r0 = pl.multiple_of(i * tkn, tkn)
        attend(kv.new_rows(0, r0), kv.new_vt(r0), P + r0, tkn)
    _for_pairs(nnew, na)

    nv = nb * B_HEAD_DIM
    oT = jnp.concatenate([_own_lanes(acc_ref[h, 0:nv, :] / acc_ref[h, nv:nv + 1, :], nb, B_HEAD_DIM, tqr)
                          for h in range(B_HEADS)], axis=0)
    y = oT.T * _silu(g_ref[...].reshape(tq, -1))
    y_ref[...] = y.reshape(y_ref.shape)


def _attn_specs(B, L, P, nb, tqr, hw, layer, new_fm):
    tq = nb * tqr
    if new_fm:
        feat = lambda n: pl.BlockSpec((None, n, tq), lambda b, j: (b, 0, j))
        seqblk = pl.BlockSpec((None, tqr, hw), lambda b, j: (b, j, 0))
        new_row = lambda c: pl.BlockSpec((None, L, c), lambda b, j: (b, 0, 0))
        new_col = lambda r: pl.BlockSpec((None, None, r, L), lambda b, j: (layer, b, 0, 0))
    else:
        feat = lambda n: pl.BlockSpec((n, tq), lambda b, j: (0, b))
        seqblk = pl.BlockSpec((nb, tqr, hw), lambda b, j: (b, 0, 0))
        new_row = lambda c: pl.BlockSpec((nb, L, c), lambda b, j: (b, 0, 0))
        new_col = None
    past = lambda r: pl.BlockSpec((None, nb, r, P), lambda b, j: (layer, b, 0, 0))
    return feat, seqblk, new_row, new_col, past


def _dsa(qT, qiT, wT, gate, past, new, *, B, L, P, nb, tqr, layer, new_fm):
    hw = gate.shape[-1]
    tq = nb * tqr
    Lk = P + L
    topk = min(TOPK_MAX, Lk // 4)
    feat, seqblk, new_row, new_col, pastspec = _attn_specs(B, L, P, nb, tqr, hw, layer, new_fm)
    in_specs = [feat(hw), feat(hw), feat(IDX_HEADS), seqblk]
    args = [qT, qiT, wT, gate]
    if P > 0:
        in_specs += [pastspec(hw), pastspec(hw), pastspec(IDX_DIM)]
        args += list(past)
    in_specs += [new_row(hw), new_col(hw) if new_fm else new_row(hw), new_row(IDX_DIM)]
    args += list(new)
    return pl.pallas_call(
        functools.partial(_dsa_kernel, P=P, nb=nb, tqr=tqr, tkp=_past_tile(P, tq), tkn=tqr, topk=topk,
                          new_fm=new_fm),
        grid=(B // nb, L // tqr),
        in_specs=in_specs,
        out_specs=seqblk,
        out_shape=jax.ShapeDtypeStruct((B, L, hw), F32),
        scratch_shapes=[pltpu.VMEM((Lk, tq), I16), pltpu.VMEM((Lk, tq), I16),
                        pltpu.VMEM((nb * IDX_DIM, IDX_HEADS * tq), BF16),
                        pltpu.VMEM((nb * hw, B_HEADS * tq), BF16),
                        pltpu.VMEM((B_HEADS, 1, tq), F32),
                        pltpu.VMEM((B_HEADS, nb * B_HEAD_DIM + ONES_ROWS, tq), F32)],
        compiler_params=_cparams(("parallel", "arbitrary")),
    )(*args)


def _diff_kernel(*refs, P, nb, tqr, tkp, tkn, lam_init, new_fm):
    tq = nb * tqr
    it = iter(refs)
    qT_ref, g_ref, lq1_ref, lk1_ref, lq2_ref, lk2_ref, sub_ref = (next(it) for _ in range(7))
    past_refs = (next(it), next(it)) if P > 0 else None
    new_refs = (next(it), next(it))
    y_ref = next(it)
    qbd_ref, m_ref, acc_ref = (next(it) for _ in range(3))
    kv = _Keys(nb, tkp, tkn, new_fm, past_refs, new_refs)
    j = pl.program_id(1)
    nmaps = 2 * C_HEADS

    _fill_block_diag(qbd_ref, qT_ref, nb, tqr, nmaps, C_HALF, False)
    m_ref[...] = jnp.full(m_ref.shape, NEG, F32)
    acc_ref[...] = jnp.zeros(acc_ref.shape, F32)

    def attend(k, vt_tiles, pen, fm=False):
        s_all = _key_dot(k, qbd_ref[...], fm)
        vts = [t.astype(BF16) for t in vt_tiles]
        ones = _ones_rows(s_all.shape[0])
        for h in range(C_HEADS):
            vt_h = jnp.concatenate([v[C_HEAD_DIM * h:C_HEAD_DIM * (h + 1), :] for v in vts] + [ones], axis=0)
            for mi in (2 * h, 2 * h + 1):
                _softmax_step(s_all[:, mi * tq:(mi + 1) * tq], pen, vt_h, m_ref, acc_ref, mi)

    if P > 0:
        def pa(t, z):
            r0 = pl.multiple_of(t * tkp, tkp)
            attend(kv.past_cols(0, r0), kv.past_vt(r0), None, True)
            return z
        lax.fori_loop(0, P // tkp, pa, 0, unroll=True)

    def na(i):
        r0 = pl.multiple_of(i * tkn, tkn)
        attend(kv.new_rows(0, r0), kv.new_vt(r0), None)
    _for_pairs(j, na)
    rd = pl.multiple_of(j * tkn, tkn)
    attend(kv.new_rows(0, rd), kv.new_vt(rd),
           jnp.where(_diag_valid(tkn, tq), 0.0, NEG) if tkn > CHUNK else None)

    lam = (jnp.exp(jnp.sum(lq1_ref[...] * lk1_ref[...], axis=-1, keepdims=True))
           - jnp.exp(jnp.sum(lq2_ref[...] * lk2_ref[...], axis=-1, keepdims=True)) + lam_init)
    outs = []
    nv = nb * C_HEAD_DIM
    for h in range(C_HEADS):
        o = (acc_ref[2 * h, 0:nv, :] / acc_ref[2 * h, nv:nv + 1, :]
             - lam * (acc_ref[2 * h + 1, 0:nv, :] / acc_ref[2 * h + 1, nv:nv + 1, :]))
        o = _own_lanes(o, nb, C_HEAD_DIM, tqr)
        ms = jnp.mean(o * o, axis=0, keepdims=True)
        outs.append(o * lax.rsqrt(ms + EPS))
    o = jnp.concatenate(outs, axis=0).T
    y = (o * sub_ref[...]) * (1.0 - lam_init) * _silu(g_ref[...].reshape(tq, -1))
    y_ref[...] = y.reshape(y_ref.shape)


def _diff(qT, gate, lams, subrow, past, new, *, B, L, P, nb, tqr, layer, new_fm, lam_init):
    hw = gate.shape[-1]
    tq = nb * tqr
    nmaps = 2 * C_HEADS
    feat, seqblk, new_row, new_col, pastspec = _attn_specs(B, L, P, nb, tqr, hw, layer, new_fm)
    const = lambda s: pl.BlockSpec(s, lambda b, j: (0, 0))
    in_specs = [feat(hw), seqblk] + [const((1, C_HALF))] * 4 + [const((1, hw))]
    args = [qT, gate, *lams, subrow]
    if P > 0:
        in_specs += [pastspec(hw), pastspec(hw)]
        args += list(past)
    in_specs += [new_row(hw), new_col(hw) if new_fm else new_row(hw)]
    args += list(new)
    return pl.pallas_call(
        functools.partial(_diff_kernel, P=P, nb=nb, tqr=tqr, tkp=_past_tile(P, tq), tkn=tqr,
                          lam_init=lam_init, new_fm=new_fm),
        grid=(B // nb, L // tqr),
        in_specs=in_specs,
        out_specs=seqblk,
        out_shape=jax.ShapeDtypeStruct((B, L, hw), F32),
        scratch_shapes=[pltpu.VMEM((nb * hw, nmaps * tq), BF16),
                        pltpu.VMEM((nmaps, 1, tq), F32),
                        pltpu.VMEM((nmaps, nb * C_HEAD_DIM + ONES_ROWS, tq), F32)],
        compiler_params=_cparams(("parallel", "arbitrary")),
    )(*args)


def _outproj_kernel(ya_ref, yb_ref, yc_ref, x_ref, w_ref, g_ref, o_ref, *, aw, hw):
    mix = jnp.dot(ya_ref[...].astype(BF16), w_ref[0:aw, :], preferred_element_type=F32)
    mix = mix + jnp.dot(yb_ref[...].astype(BF16), w_ref[aw:aw + hw, :], preferred_element_type=F32)
    mix = mix + jnp.dot(yc_ref[...].astype(BF16), w_ref[aw + hw:aw + 2 * hw, :], preferred_element_type=F32)
    ms = jnp.mean(mix * mix, axis=-1, keepdims=True)
    o_ref[...] = x_ref[...] + mix * lax.rsqrt(ms + EPS) * g_ref[...]


def _outproj(ya, yb, yc, x2d, w, g, *, tm):
    T, D = x2d.shape
    aw, hw = D // 2, D // 4
    row = lambda n: pl.BlockSpec((tm, n), lambda i: (i, 0))
    const = lambda s: pl.BlockSpec(s, lambda i: (0, 0))
    return pl.pallas_call(
        functools.partial(_outproj_kernel, aw=aw, hw=hw),
        grid=(T // tm,),
        in_specs=[row(aw), row(hw), row(hw), row(D), const((D, D)), const((1, D))],
        out_specs=row(D),
        out_shape=jax.ShapeDtypeStruct((T, D), F32),
        compiler_params=_cparams(("parallel",)),
    )(ya, yb, yc, x2d, w, g)


_ROW_PROMPT = ("a_x", "a_g", "b_g", "c_g", "b_k", "c_k", "b_ik")
_FEAT_PROMPT = ("b_q", "b_iq", "c_q", "b_k", "b_v", "c_k", "c_v", "b_ik", "b_iw")
_ROW_SAMPLE = ("a_x", "a_g", "b_g", "c_g", "b_k", "b_v", "c_k", "c_v", "b_ik")
_FEAT_SAMPLE = ("b_q", "b_iq", "c_q", "b_ik", "b_iw")
_BF16_OUT = {"b_q", "b_iq", "c_q"}
_PROMPT_STATES = ("b_k", "b_v", "c_k", "c_v", "b_ik")


def _pack_w_in(w_in, D, rows, feats, row_bf16, stacked):
    aw, hw = D // 2, D // 4
    sizes = (aw, aw, hw, hw, hw, hw, IDX_HEADS * IDX_DIM, IDX_DIM, IDX_HEADS, hw, hw, hw, hw)
    names = ("a_x", "a_g", "b_q", "b_k", "b_v", "b_g", "b_iq", "b_ik", "b_iw", "c_q", "c_k", "c_v", "c_g")
    scale = {"b_q": B_HEAD_DIM ** -0.5 * LOG2E, "c_q": C_HALF ** -0.5 * LOG2E,
             "b_iw": (IDX_DIM ** -0.5) * (IDX_HEADS ** -0.5)}
    seg, o = {}, 0
    for n, s in zip(names, sizes):
        seg[n] = w_in[..., o:o + s] * scale[n] if n in scale else w_in[..., o:o + s]
        o += s
    depth = w_in.shape[0]

    def cat(parts, mult):
        w = jnp.concatenate([seg[n].astype(BF16) for n in parts], axis=-1)
        pad = (-w.shape[-1]) % mult
        return jnp.concatenate([w, jnp.zeros((depth, D, pad), BF16)], axis=-1) if pad else w
    wr = cat(rows, 128)
    wf = jnp.swapaxes(cat(feats, 16), 1, 2)
    row_plan = tuple((seg[n].shape[-1], BF16 if n in row_bf16 else F32) for n in rows)
    feat_plan = tuple((seg[n].shape[-1], BF16 if n in _BF16_OUT else F32, n in stacked) for n in feats)
    return wr, wf, row_plan, feat_plan


def _block_diag(w):
    depth, nb, n, _ = w.shape
    eye = jnp.eye(nb, dtype=w.dtype)
    return jnp.einsum("lnde,nm->lndme", w, eye).reshape(depth, nb * n, nb * n)


def _prompt_layer(x, B, L, layer, wl, zero_buf, zero_h, lam_init, stacks):
    T, D = x.shape
    hw = D // 4
    tm = min(PROJ_ROWS, L)
    out = _inproj(x, wl["norm_pre"], wl["wr_p"], wl["wf_p"], tm=tm, row_plan=wl["plan_p"][0],
                  feat_plan=wl["plan_p"][1], feat_batch=(B, L), layer=layer, stacks=stacks)
    ax, ag, bg, cg, bk_bf, ck_bf, bik = out[:7]
    bqT, biqT, cqT, bkT, bvT, ckT, cvT, bikT, biwT = out[7:]
    r3 = lambda a: a.reshape(B, L, a.shape[-1])
    ya, nbuf, hlast = _amix(r3(ax), r3(ag), zero_buf, zero_h[:, None, :], wl["a_conv_w"], wl["a_conv_b"],
                            wl["rw"], wl["a_rg_b"], wl["iw"], wl["a_in_b"], wl["a_lambda"], tl=min(L, 512))
    tqr = min(512, L)
    kw = dict(B=B, L=L, P=0, nb=1, tqr=tqr, layer=layer, new_fm=True)
    yb = _dsa(bqT, biqT, biwT, r3(bg), None, (r3(bk_bf), bvT, r3(bik)), **kw)
    yc = _diff(cqT, r3(cg), wl["lams"], wl["subrow"], None, (r3(ck_bf), cvT), lam_init=lam_init, **kw)
    x_new = _outproj(ya.reshape(T, -1), yb.reshape(T, hw), yc.reshape(T, hw), x, wl["w_out"], wl["norm_post"],
                     tm=tm)
    return x_new, (nbuf, hlast[:, 0, :]), (bkT, bvT, ckT, cvT, bikT)


def _sample_layer(x, B, L, P, layer, wl, conv_buf, h0, past, lam_init):
    T, D = x.shape
    hw = D // 4
    tm = min(PROJ_ROWS, T)
    out = _inproj(x, wl["norm_pre"], wl["wr_s"], wl["wf_s"], tm=tm, row_plan=wl["plan_s"][0],
                  feat_plan=wl["plan_s"][1], feat_batch=None)
    ax, ag, bg, cg, bk, bv, ck, cv, bik = out[:9]
    bqT, biqT, cqT, bikT, biwT = out[9:]
    r3 = lambda a: a.reshape(B, L, a.shape[-1])
    ya, nbuf, hlast = _amix(r3(ax), r3(ag), conv_buf, h0[:, None, :], wl["a_conv_w"], wl["a_conv_b"],
                            wl["rw"], wl["a_rg_b"], wl["iw"], wl["a_in_b"], wl["a_lambda"], tl=L)
    nb = 2 if B % 2 == 0 else 1
    kw = dict(B=B, L=L, P=P, nb=nb, tqr=L, layer=layer, new_fm=False)
    pkT, pvT, pkiT, pckT, pcvT = past
    yb = _dsa(bqT, biqT, biwT, r3(bg), (pkT, pvT, pkiT), (r3(bk), r3(bv), r3(bik)), **kw)
    yc = _diff(cqT, r3(cg), wl["lams"], wl["subrow"], (pckT, pcvT), (r3(ck), r3(cv)), lam_init=lam_init, **kw)
    x_new = _outproj(ya.reshape(T, -1), yb.reshape(T, hw), yc.reshape(T, hw), x, wl["w_out"], wl["norm_post"],
                     tm=tm)
    states = (nbuf, hlast[:, 0, :],
              bk.reshape(B, L, B_HEADS, B_HEAD_DIM), bv.reshape(B, L, B_HEADS, B_HEAD_DIM),
              bikT.reshape(IDX_DIM, B, L).transpose(1, 2, 0),
              ck.reshape(B, L, C_HEADS, C_HEAD_DIM), cv.reshape(B, L, C_HEADS, C_HEAD_DIM))
    return x_new, states


def kernel(x_prompt, x_sample, cache_a_conv, state_a_h, cache_b_k, cache_b_v, cache_b_kidx, cache_c_k, cache_c_v, norm_pre, norm_post, w_in, w_out, a_conv_w, a_conv_b, a_rg_w, a_rg_b, a_in_w, a_in_b, a_lambda, c_lam_q1, c_lam_k1, c_lam_q2, c_lam_k2, c_subln):
    Bp, Lp, D = x_prompt.shape
    Bs, Ls, _ = x_sample.shape
    depth = w_in.shape[0]
    P = cache_b_k.shape[2]
    aw = D // 2

    wr_p, wf_p, *plan_p = _pack_w_in(w_in, D, _ROW_PROMPT, _FEAT_PROMPT, {"b_k", "c_k"}, _PROMPT_STATES)
    wr_s, wf_s, *plan_s = _pack_w_in(w_in, D, _ROW_SAMPLE, _FEAT_SAMPLE, set(), ())
    rw = _block_diag(a_rg_w).astype(BF16)
    iw = _block_diag(a_in_w).astype(BF16)
    wo = w_out.astype(BF16)

    fm = lambda c: jnp.transpose(c, (0, 1, 3, 4, 2)).reshape(depth, Bs, -1, P)
    past = (fm(cache_b_k), fm(cache_b_v), jnp.transpose(cache_b_kidx, (0, 1, 3, 2)), fm(cache_c_k), fm(cache_c_v))

    xp = x_prompt.reshape(Bp * Lp, D)
    xs = x_sample.reshape(Bs * Ls, D)
    zero_buf = jnp.zeros((Bp, CONV_W - 1, aw), F32)
    zero_h = jnp.zeros((Bp, aw), F32)
    stacks = tuple(jnp.zeros((depth, Bp, n, Lp), F32) for n, _, stacked in plan_p[1] if stacked)
    p_st, s_st = [], []
    for l in range(depth):
        row = lambda a: a[l][None, :]
        wl = {"norm_pre": row(norm_pre), "norm_post": row(norm_post), "w_out": wo[l],
              "wr_p": wr_p[l], "wf_p": wf_p[l], "plan_p": plan_p, "wr_s": wr_s[l], "wf_s": wf_s[l], "plan_s": plan_s,
              "a_conv_w": a_conv_w[l], "a_conv_b": row(a_conv_b), "rw": rw[l], "a_rg_b": row(a_rg_b),
              "iw": iw[l], "a_in_b": row(a_in_b), "a_lambda": row(a_lambda),
              "lams": (row(c_lam_q1), row(c_lam_k1), row(c_lam_q2), row(c_lam_k2)),
              "subrow": jnp.tile(c_subln[l], C_HEADS)[None, :]}
        lam_init = 0.8 - 0.6 * math.exp(-0.3 * l)
        xp, st_p, stacks = _prompt_layer(xp, Bp, Lp, l, wl, zero_buf, zero_h, lam_init, stacks)
        xs, st_s = _sample_layer(xs, Bs, Ls, P, l, wl, cache_a_conv[l], state_a_h[l], past, lam_init)
        p_st.append(st_p)
        s_st.append(st_s)

    stk = lambda states, i: jnp.stack([s[i] for s in states], axis=0)
    heads = lambda t, nh: t.reshape(depth, Bp, nh, -1, Lp).transpose(0, 1, 4, 2, 3)
    bkT, bvT, ckT, cvT, bikT = stacks
    return (xp.reshape(Bp, Lp, D), xs.reshape(Bs, Ls, D),
            stk(p_st, 0), stk(p_st, 1), heads(bkT, B_HEADS), heads(bvT, B_HEADS), bikT.transpose(0, 1, 3, 2),
            heads(ckT, C_HEADS), heads(cvT, C_HEADS), *[stk(s_st, i) for i in range(7)])
```

```python
import functools
import math

import jax
import jax.numpy as jnp
from jax import lax
from jax.experimental import pallas as pl
from jax.experimental.pallas import tpu as pltpu

F32 = jnp.float32
BF16 = jnp.bfloat16
I32 = jnp.int32

CHUNK = 64
CONV_W = 4
LRU_C = 8.0
B_HEADS = 4
B_HEAD_DIM = 64
IDX_HEADS = 8
IDX_DIM = 32
TOPK_MAX = 256
C_HEADS = 4
C_HALF = 32
C_HEAD_DIM = 2 * C_HALF
EPS = 1e-6
NEG = -1e30
INT_MIN = -2 ** 31
VMEM_LIMIT_BYTES = 56 * 1024 * 1024
KEY_TILE = 256
TILE_ELEMS = 512 * 256
TIE_ROWS = 256
PROJ_ROWS = 512
ONES_ROWS = 16
I16 = jnp.int16
MIN16 = -2 ** 15
LOG2E = math.log2(math.e)


def _past_tile(P, tq):
    rows = max(KEY_TILE, TILE_ELEMS // tq)
    while P % rows:
        rows //= 2
    return rows


def _cparams(sem):
    return pltpu.CompilerParams(dimension_semantics=sem, vmem_limit_bytes=VMEM_LIMIT_BYTES)


def _silu(g):
    return g * jax.nn.sigmoid(g)


def _inproj_kernel(x_ref, g_ref, wr_ref, wf_ref, *rest, row_plan, feat_plan, n_alias):
    out_refs = rest[n_alias:]
    x = x_ref[...]
    ms = jnp.mean(x * x, axis=-1, keepdims=True)
    xn = (x * lax.rsqrt(ms + EPS) * g_ref[...]).astype(BF16)
    proj = jnp.dot(xn, wr_ref[...], preferred_element_type=F32)
    projT = lax.dot_general(wf_ref[...], xn, (((1,), (1,)), ((), ())),
                            preferred_element_type=F32)
    refs = iter(out_refs)
    o = 0
    for n, dt in row_plan:
        ref = next(refs)
        ref[...] = proj[:, o:o + n].astype(dt)
        o += n
    o = 0
    for n, dt, _ in feat_plan:
        ref = next(refs)
        ref[...] = projT[o:o + n, :].astype(dt)
        o += n


def _inproj(x2d, g, wr, wf, *, tm, row_plan, feat_plan, feat_batch, layer=0, stacks=()):
    T, D = x2d.shape
    row = lambda n: pl.BlockSpec((tm, n), lambda i: (i, 0))
    const = lambda s: pl.BlockSpec(s, lambda i: (0, 0))
    out_shape = [jax.ShapeDtypeStruct((T, n), dt) for n, dt in row_plan]
    out_specs = [row(n) for n, _ in row_plan]
    aliases, stack_it = {}, iter(stacks)
    for n, dt, stacked in feat_plan:
        if feat_batch is None:
            out_shape.append(jax.ShapeDtypeStruct((n, T), dt))
            out_specs.append(pl.BlockSpec((n, tm), lambda i: (0, i)))
            continue
        B, L = feat_batch
        npb = L // tm
        if stacked:
            buf = next(stack_it)
            aliases[4 + len(aliases)] = len(out_shape)
            out_shape.append(jax.ShapeDtypeStruct(buf.shape, dt))
            out_specs.append(pl.BlockSpec((None, None, n, tm), lambda i: (layer, i // npb, 0, i % npb)))
        else:
            out_shape.append(jax.ShapeDtypeStruct((B, n, L), dt))
            out_specs.append(pl.BlockSpec((None, n, tm), lambda i: (i // npb, 0, i % npb)))
    return pl.pallas_call(
        functools.partial(_inproj_kernel, row_plan=row_plan, feat_plan=feat_plan, n_alias=len(stacks)),
        grid=(T // tm,),
        in_specs=[row(D), const((1, D)), const(wr.shape), const(wf.shape)]
        + [pl.BlockSpec(memory_space=pl.ANY)] * len(stacks),
        out_specs=out_specs, out_shape=out_shape,
        input_output_aliases=aliases,
        compiler_params=_cparams(("parallel",)),
    )(x2d, g, wr, wf, *stacks)


def _amix_kernel(ax_ref, ag_ref, buf_ref, h0_ref, cw_ref, cb_ref, rw_ref, rb_ref, iw_ref, ib_ref, lam_ref,
                 ya_ref, nbuf_ref, hlast_ref, xbuf_ref, hc_ref, *, tl, nl, aw):
    li = pl.program_id(1)

    @pl.when(li == 0)
    def _():
        xbuf_ref[0:8, :] = jnp.zeros((8, aw), F32)
        xbuf_ref[8 - (CONV_W - 1):8, :] = buf_ref[...]
        hc_ref[...] = h0_ref[...]

    x = ax_ref[...]
    xbuf_ref[8:8 + tl, :] = x
    cw = cw_ref[...]
    conv = cb_ref[...] + x * cw[CONV_W - 1:CONV_W, :]
    for s in range(1, CONV_W):
        conv = conv + xbuf_ref[8 - s:8 - s + tl, :] * cw[CONV_W - 1 - s:CONV_W - s, :]

    cbf = conv.astype(BF16)
    r = jax.nn.sigmoid(jnp.dot(cbf, rw_ref[...], preferred_element_type=F32) + rb_ref[...])
    ig = jax.nn.sigmoid(jnp.dot(cbf, iw_ref[...], preferred_element_type=F32) + ib_ref[...])
    nl_lam = -lam_ref[...]
    sp = jnp.maximum(nl_lam, 0.0) + jnp.log1p(jnp.exp(-jnp.abs(nl_lam)))
    log_a = (-LRU_C) * r * sp
    a = jnp.exp(log_a)
    u = jnp.sqrt(1.0 - a * a) * (ig * conv)

    row = lax.broadcasted_iota(I32, (tl, aw), 0)
    s = 1
    while s < tl:
        if s % 8:
            keep = row >= s
            u = jnp.where(keep, a * pltpu.roll(u, s, 0) + u, u)
            a = jnp.where(keep, a * pltpu.roll(a, s, 0), a)
        else:
            u = jnp.concatenate([u[:s], a[s:] * u[:tl - s] + u[s:]], axis=0)
            a = jnp.concatenate([a[:s], a[s:] * a[:tl - s]], axis=0)
        s *= 2
    h = a * hc_ref[...] + u
    ya_ref[...] = h * _silu(ag_ref[...])
    hc_ref[...] = h[tl - 1:tl, :]
    xbuf_ref[0:8, :] = x[tl - 8:tl, :]

    @pl.when(li == nl - 1)
    def _():
        nbuf_ref[...] = xbuf_ref[8 + tl - (CONV_W - 1):8 + tl, :]
        hlast_ref[...] = h[tl - 1:tl, :]


def _amix(ax, ag, buf, h0, cw, cb, rw, rb, iw, ib, lam, *, tl):
    B, L, aw = ax.shape
    nl = L // tl
    seq = pl.BlockSpec((None, tl, aw), lambda b, l: (b, l, 0))
    perb = lambda r: pl.BlockSpec((None, r, aw), lambda b, l: (b, 0, 0))
    const = lambda s: pl.BlockSpec(s, lambda b, l: (0, 0))
    return pl.pallas_call(
        functools.partial(_amix_kernel, tl=tl, nl=nl, aw=aw),
        grid=(B, nl),
        in_specs=[seq, seq, perb(CONV_W - 1), perb(1), const((CONV_W, aw)), const((1, aw)),
                  const((aw, aw)), const((1, aw)), const((aw, aw)), const((1, aw)), const((1, aw))],
        out_specs=[seq, perb(CONV_W - 1), perb(1)],
        out_shape=[jax.ShapeDtypeStruct((B, L, aw), F32), jax.ShapeDtypeStruct((B, CONV_W - 1, aw), F32),
                   jax.ShapeDtypeStruct((B, 1, aw), F32)],
        scratch_shapes=[pltpu.VMEM((tl + 8, aw), F32), pltpu.VMEM((1, aw), F32)],
        compiler_params=_cparams(("parallel", "arbitrary")),
    )(ax, ag, buf, h0, cw, cb, rw, rb, iw, ib, lam)


def _col_reduce(x, op):
    rows, n = x.shape
    return op(op(x.reshape(rows // 8, 8, n), axis=0), axis=0, keepdims=True)


def _diag_valid(rows, tq):
    kc = lax.broadcasted_iota(I32, (rows, tq), 0) // CHUNK
    qc = lax.broadcasted_iota(I32, (rows, tq), 1) // CHUNK
    return kc <= qc


def _ones_rows(rows):
    return jnp.where(lax.broadcasted_iota(I32, (ONES_ROWS, rows), 0) == 0, 1.0, 0.0).astype(BF16)


def _softmax_step(s, pen, vt_h, m_ref, acc_ref, idx):
    m_old = m_ref[idx]
    sm = s if pen is None else s + pen
    m_new = jnp.maximum(m_old, _col_reduce(sm, jnp.max))
    alpha = jnp.exp2(m_old - m_new)
    p = jnp.exp2(sm - m_new).astype(BF16)
    pv = jnp.dot(vt_h, p, preferred_element_type=F32)
    acc_ref[idx, :, :] = alpha * acc_ref[idx, :, :] + pv
    m_ref[idx] = m_new


def _for_pairs(n, fn):
    def two(t, z):
        fn(2 * t)
        fn(2 * t + 1)
        return z
    lax.fori_loop(0, lax.shift_right_logical(n, 1), two, 0)

    @pl.when((n & 1) == 1)
    def _():
        fn(n - 1)


def _key_dot(keys, rhs, keys_fm):
    keys = keys.astype(BF16)
    if keys_fm:
        return lax.dot_general(keys, rhs, (((0,), (0,)), ((), ())), preferred_element_type=F32)
    return jnp.dot(keys, rhs, preferred_element_type=F32)


def _fill_block_diag(dst_ref, qT_ref, nb, tqr, nblk, blk, shared_rows):
    tq = nb * tqr
    dst_ref[...] = jnp.zeros(dst_ref.shape, dst_ref.dtype)
    for a in range(nb):
        for m in range(nblk):
            r = a * blk if shared_rows else (a * nblk + m) * blk
            dst_ref[r:r + blk, m * tq + a * tqr:m * tq + (a + 1) * tqr] = \
                qT_ref[blk * m:blk * (m + 1), a * tqr:(a + 1) * tqr]


def _own_lanes(full, nb, dh, tqr):
    if nb == 1:
        return full
    lane_seq = lax.broadcasted_iota(I32, (dh, nb * tqr), 1) // tqr
    out = full[0:dh, :]
    for a in range(1, nb):
        out = jnp.where(lane_seq == a, full[a * dh:(a + 1) * dh, :], out)
    return out


class _Keys:
    def __init__(self, nb, tkp, tkn, new_fm, past_refs, new_refs):
        self.nb, self.tkp, self.tkn, self.new_fm = nb, tkp, tkn, new_fm
        self.past_refs, self.new_refs = past_refs, new_refs

    def past_cols(self, which, r0):
        t = self.past_refs[which][:, :, pl.ds(r0, self.tkp)]
        return t.reshape(t.shape[0] * t.shape[1], t.shape[2])

    def past_vt(self, r0):
        return [self.past_refs[1][a, :, pl.ds(r0, self.tkp)] for a in range(self.nb)]

    def new_rows(self, which, r0):
        ref = self.new_refs[which]
        if self.new_fm:
            return ref[pl.ds(r0, self.tkn), :]
        return jnp.concatenate([ref[a] for a in range(self.nb)], axis=1)

    def new_vt(self, r0):
        ref = self.new_refs[1]
        if self.new_fm:
            return [ref[:, pl.ds(r0, self.tkn)]]
        return [ref[a].T for a in range(self.nb)]


def _dsa_kernel(*refs, P, nb, tqr, tkp, tkn, topk, new_fm):
    tq = nb * tqr
    it = iter(refs)
    qT_ref, qiT_ref, wT_ref, g_ref = next(it), next(it), next(it), next(it)
    past_refs = (next(it), next(it), next(it)) if P > 0 else None
    new_refs = (next(it), next(it), next(it))
    y_ref = next(it)
    hi_ref, lo_ref, qip_ref, qbd_ref, m_ref, acc_ref = (next(it) for _ in range(6))
    kv = _Keys(nb, tkp, tkn, new_fm, past_refs, new_refs)
    j = pl.program_id(1)
    nnew = j + 1

    _fill_block_diag(qip_ref, qiT_ref, nb, tqr, IDX_HEADS, IDX_DIM, True)
    w = wT_ref[...]

    def put_keys(r0, rows, key):
        hi_ref[pl.ds(r0, rows), :] = (key >> 16).astype(I16)
        lo_ref[pl.ds(r0, rows), :] = ((key & 0xFFFF) + MIN16).astype(I16)

    def idx_keys(ki, valid, fm=False):
        s_all = _key_dot(ki, qip_ref[...], fm)
        acc = None
        for h in range(IDX_HEADS):
            t = jnp.maximum(s_all[:, h * tq:(h + 1) * tq], 0.0) * w[h:h + 1, :]
            acc = t if acc is None else acc + t
        bits = lax.bitcast_convert_type(acc, I32)
        sign = bits >> 31
        key = (bits ^ (sign & 0x7FFFFFFF)) - sign
        return key if valid is None else jnp.where(valid, key, INT_MIN)

    def for_tiles(fn, carry, n_new=None):
        if P > 0:
            carry = lax.fori_loop(0, P // tkp, lambda t, c: fn(pl.multiple_of(t * tkp, tkp), tkp, c), carry,
                                  unroll=True)
        return lax.fori_loop(0, nnew if n_new is None else n_new,
                             lambda i, c: fn(P + pl.multiple_of(i * tkn, tkn), tkn, c), carry)

    if P > 0:
        def past_keys(t, c):
            r0 = pl.multiple_of(t * tkp, tkp)
            put_keys(r0, tkp, idx_keys(kv.past_cols(2, r0), None, True))
            return c
        lax.fori_loop(0, P // tkp, past_keys, 0, unroll=True)

    def new_keys(i):
        r0 = pl.multiple_of(i * tkn, tkn)
        put_keys(P + r0, tkn, idx_keys(kv.new_rows(2, r0), None))
    _for_pairs(j, new_keys)
    rd = pl.multiple_of(j * tkn, tkn)
    put_keys(P + rd, tkn, idx_keys(kv.new_rows(2, rd), _diag_valid(tkn, tq) if tkn > CHUNK else None))

    one, zero = jnp.int16(1), jnp.int16(0)

    def fold16(ind):
        parts = [ind[r:r + 16] for r in range(0, ind.shape[0], 16)]
        ways = max(1, min(len(parts), 512 // tq))
        accs = parts[:ways]
        for i in range(ways, len(parts)):
            accs[i % ways] = accs[i % ways] + parts[i]
        return functools.reduce(lambda x, y: x + y, accs)

    def total(parts):
        return jnp.sum(parts.astype(I32), axis=0, keepdims=True)

    def count(pred):
        def fn(r0, rows, c):
            ind = pred(lambda: hi_ref[pl.ds(r0, rows), :], lambda: lo_ref[pl.ds(r0, rows), :], r0, rows)
            return c + fold16(jnp.where(ind, one, zero))
        return total(for_tiles(fn, jnp.zeros((16, tq), I16)))

    def bisect16(pick, kth):
        def step(it, ans):
            cand = ans + jnp.left_shift(jnp.int32(1), 15 - it)
            c16 = cand.astype(I16)
            cnt = count(lambda hi, lo, r0, rows: pick(hi, lo) >= c16)
            return jnp.where(cnt >= kth, cand, ans)
        return lax.fori_loop(0, 16, step, jnp.full((1, tq), MIN16, I32))

    b = bisect16(lambda hi, lo: hi(), topk)
    b16 = b.astype(I16)

    def mask_lo(r0, rows, c):
        hi = hi_ref[pl.ds(r0, rows), :]
        lo_ref[pl.ds(r0, rows), :] = jnp.where(hi == b16, lo_ref[pl.ds(r0, rows), :], jnp.int16(MIN16))
        return c + fold16(jnp.where(hi > b16, one, zero))
    n_above = total(for_tiles(mask_lo, jnp.zeros((16, tq), I16)))
    kth_lo = topk - n_above
    cst = bisect16(lambda hi, lo: lo(), kth_lo)
    c16 = cst.astype(I16)

    def is_tie(hi, lo):
        return (hi == b16) & (lo == c16)
    n_gt = n_above + count(lambda hi, lo, r0, rows: lo() > c16)
    n_tie = count(lambda hi, lo, r0, rows: is_tie(hi(), lo()))
    need = (n_gt + n_tie > topk) & (b > MIN16)
    take = topk - n_gt

    @pl.when(jnp.max(jnp.where(need, 1, 0)) > 0)
    def _():
        take_f = jnp.where(need, take, 2 ** 30).astype(F32)
        tri = jnp.where(lax.broadcasted_iota(I32, (TIE_ROWS, TIE_ROWS), 0)
                        >= lax.broadcasted_iota(I32, (TIE_ROWS, TIE_ROWS), 1), 1.0, 0.0).astype(BF16)

        def demote(r0, rows, seen):
            blocks = []
            for o in range(0, rows, TIE_ROWS):
                n = min(TIE_ROWS, rows - o)
                sl = pl.ds(r0 + o, n)
                hi, lo = hi_ref[sl, :], lo_ref[sl, :]
                tie = is_tie(hi, lo)
                ind = jnp.where(tie, jnp.bfloat16(1), jnp.bfloat16(0))
                blocks.append((sl, n, hi, lo, tie, jnp.dot(tri[0:n, 0:n], ind, preferred_element_type=F32)))
            for sl, n, hi, lo, tie, within in blocks:
                rank = within + seen
                drop = tie & (jnp.where(rank > take_f, 1, 0).astype(I16) != 0)
                hi_ref[sl, :] = jnp.where(drop, jnp.int16(MIN16), hi)
                lo_ref[sl, :] = jnp.where(drop, jnp.int16(MIN16), lo)
                seen = rank[n - 1:n, :]
            return seen
        for_tiles(demote, jnp.zeros((1, tq), F32))

    whole = cst == MIN16
    bs16 = jnp.where(whole, jnp.maximum(b - 1, MIN16), b).astype(I16)
    cs16 = jnp.where(whole, 2 ** 15 - 1, cst - 1).astype(I16)
    open_, shut = jnp.bfloat16(0), jnp.bfloat16(NEG)

    _fill_block_diag(qbd_ref, qT_ref, nb, tqr, B_HEADS, B_HEAD_DIM, False)
    m_ref[...] = jnp.full(m_ref.shape, NEG, F32)
    acc_ref[...] = jnp.zeros(acc_ref.shape, F32)

    def attend(k, vt_tiles, r0, rows, fm=False):
        hi, lo = hi_ref[pl.ds(r0, rows), :], lo_ref[pl.ds(r0, rows), :]
        pen = jnp.where(hi > bs16, open_, jnp.where(lo > cs16, open_, shut)).astype(F32)
        s_all = _key_dot(k, qbd_ref[...], fm)
        vts = [t.astype(BF16) for t in vt_tiles]
        ones = _ones_rows(rows)
        for h in range(B_HEADS):
            vt_h = jnp.concatenate([v[B_HEAD_DIM * h:B_HEAD_DIM * (h + 1), :] for v in vts] + [ones], axis=0)
            _softmax_step(s_all[:, h * tq:(h + 1) * tq], pen, vt_h, m_ref, acc_ref, h)

    if P > 0:
        def pa(t, z):
            r0 = pl.multiple_of(t * tkp, tkp)
            attend(kv.past_cols(0, r0), kv.past_vt(r0), r0, tkp, True)
            return z
        lax.fori_loop(0, P // tkp, pa, 0, unroll=True)

    def na(i):
        r0 = pl.multiple_of(i * tkn, tkn)
        attend(kv.new_rows(0, r0), kv.new_vt(r0), P + r0, tkn)
    _for_pairs(nnew, na)

    nv = nb * B_HEAD_DIM
    oT = jnp.concatenate([_own_lanes(acc_ref[h, 0:nv, :] / acc_ref[h, nv:nv + 1, :], nb, B_HEAD_DIM, tqr)
                          for h in range(B_HEADS)], axis=0)
    y = oT.T * _silu(g_ref[...].reshape(tq, -1))
    y_ref[...] = y.reshape(y_ref.shape)


def _attn_specs(B, L, P, nb, tqr, hw, layer, new_fm):
    tq = nb * tqr
    if new_fm:
        feat = lambda n: pl.BlockSpec((None, n, tq), lambda b, j: (b, 0, j))
        seqblk = pl.BlockSpec((None, tqr, hw), lambda b, j: (b, j, 0))
        new_row = lambda c: pl.BlockSpec((None, L, c), lambda b, j: (b, 0, 0))
        new_col = lambda r: pl.BlockSpec((None, None, r, L), lambda b, j: (layer, b, 0, 0))
    else:
        feat = lambda n: pl.BlockSpec((n, tq), lambda b, j: (0, b))
        seqblk = pl.BlockSpec((nb, tqr, hw), lambda b, j: (b, 0, 0))
        new_row = lambda c: pl.BlockSpec((nb, L, c), lambda b, j: (b, 0, 0))
        new_col = None
    past = lambda r: pl.BlockSpec((None, nb, r, P), lambda b, j: (layer, b, 0, 0))
    return feat, seqblk, new_row, new_col, past


def _dsa(qT, qiT, wT, gate, past, new, *, B, L, P, nb, tqr, layer, new_fm):
    hw = gate.shape[-1]
    tq = nb * tqr
    Lk = P + L
    topk = min(TOPK_MAX, Lk // 4)
    feat, seqblk, new_row, new_col, pastspec = _attn_specs(B, L, P, nb, tqr, hw, layer, new_fm)
    in_specs = [feat(hw), feat(hw), feat(IDX_HEADS), seqblk]
    args = [qT, qiT, wT, gate]
    if P > 0:
        in_specs += [pastspec(hw), pastspec(hw), pastspec(IDX_DIM)]
        args += list(past)
    in_specs += [new_row(hw), new_col(hw) if new_fm else new_row(hw), new_row(IDX_DIM)]
    args += list(new)
    return pl.pallas_call(
        functools.partial(_dsa_kernel, P=P, nb=nb, tqr=tqr, tkp=_past_tile(P, tq), tkn=tqr, topk=topk,
                          new_fm=new_fm),
        grid=(B // nb, L // tqr),
        in_specs=in_specs,
        out_specs=seqblk,
        out_shape=jax.ShapeDtypeStruct((B, L, hw), F32),
        scratch_shapes=[pltpu.VMEM((Lk, tq), I16), pltpu.VMEM((Lk, tq), I16),
                        pltpu.VMEM((nb * IDX_DIM, IDX_HEADS * tq), BF16),
                        pltpu.VMEM((nb * hw, B_HEADS * tq), BF16),
                        pltpu.VMEM((B_HEADS, 1, tq), F32),
                        pltpu.VMEM((B_HEADS, nb * B_HEAD_DIM + ONES_ROWS, tq), F32)],
        compiler_params=_cparams(("parallel", "arbitrary")),
    )(*args)


def _diff_kernel(*refs, P, nb, tqr, tkp, tkn, lam_init, new_fm):
    tq = nb * tqr
    it = iter(refs)
    qT_ref, g_ref, lq1_ref, lk1_ref, lq2_ref, lk2_ref, sub_ref = (next(it) for _ in range(7))
    past_refs = (next(it), next(it)) if P > 0 else None
    new_refs = (next(it), next(it))
    y_ref = next(it)
    qbd_ref, m_ref, acc_ref = (next(it) for _ in range(3))
    kv = _Keys(nb, tkp, tkn, new_fm, past_refs, new_refs)
    j = pl.program_id(1)
    nmaps = 2 * C_HEADS
    pair = m_ref.shape[0] == C_HEADS

    _fill_block_diag(qbd_ref, qT_ref, nb, tqr, nmaps, C_HALF, False)
    m_ref[...] = jnp.full(m_ref.shape, NEG, F32)
    acc_ref[...] = jnp.zeros(acc_ref.shape, F32)

    def attend(k, vt_tiles, pen, fm=False):
        s_all = _key_dot(k, qbd_ref[...], fm)
        vts = [t.astype(BF16) for t in vt_tiles]
        ones = _ones_rows(s_all.shape[0])
        pen2 = None if (pen is None or not pair) else jnp.concatenate([pen, pen], axis=1)
        for h in range(C_HEADS):
            vt_h = jnp.concatenate([v[C_HEAD_DIM * h:C_HEAD_DIM * (h + 1), :] for v in vts] + [ones], axis=0)
            if pair:
                _softmax_step(s_all[:, 2 * h * tq:(2 * h + 2) * tq], pen2, vt_h, m_ref, acc_ref, h)
            else:
                for mi in (2 * h, 2 * h + 1):
                    _softmax_step(s_all[:, mi * tq:(mi + 1) * tq], pen, vt_h, m_ref, acc_ref, mi)

    if P > 0:
        def pa(t, z):
            r0 = pl.multiple_of(t * tkp, tkp)
            attend(kv.past_cols(0, r0), kv.past_vt(r0), None, True)
            return z
        lax.fori_loop(0, P // tkp, pa, 0, unroll=True)

    def na(i):
        r0 = pl.multiple_of(i * tkn, tkn)
        attend(kv.new_rows(0, r0), kv.new_vt(r0), None)
    _for_pairs(j, na)
    rd = pl.multiple_of(j * tkn, tkn)
    attend(kv.new_rows(0, rd), kv.new_vt(rd),
           jnp.where(_diag_valid(tkn, tq), 0.0, NEG) if tkn > CHUNK else None)

    lam = (jnp.exp(jnp.sum(lq1_ref[...] * lk1_ref[...], axis=-1, keepdims=True))
           - jnp.exp(jnp.sum(lq2_ref[...] * lk2_ref[...], axis=-1, keepdims=True)) + lam_init)
    outs = []
    nv = nb * C_HEAD_DIM
    for h in range(C_HEADS):
        if pair:
            o = (acc_ref[h, 0:nv, 0:tq] / acc_ref[h, nv:nv + 1, 0:tq]
                 - lam * (acc_ref[h, 0:nv, tq:2 * tq] / acc_ref[h, nv:nv + 1, tq:2 * tq]))
        else:
            o = (acc_ref[2 * h, 0:nv, :] / acc_ref[2 * h, nv:nv + 1, :]
                 - lam * (acc_ref[2 * h + 1, 0:nv, :] / acc_ref[2 * h + 1, nv:nv + 1, :]))
        o = _own_lanes(o, nb, C_HEAD_DIM, tqr)
        ms = jnp.mean(o * o, axis=0, keepdims=True)
        outs.append(o * lax.rsqrt(ms + EPS))
    o = jnp.concatenate(outs, axis=0).T
    y = (o * sub_ref[...]) * (1.0 - lam_init) * _silu(g_ref[...].reshape(tq, -1))
    y_ref[...] = y.reshape(y_ref.shape)


def _diff(qT, gate, lams, subrow, past, new, *, B, L, P, nb, tqr, layer, new_fm, lam_init):
    hw = gate.shape[-1]
    tq = nb * tqr
    nmaps = 2 * C_HEADS
    nstate, wstate = (C_HEADS, 2 * tq) if tq <= 128 else (nmaps, tq)
    feat, seqblk, new_row, new_col, pastspec = _attn_specs(B, L, P, nb, tqr, hw, layer, new_fm)
    const = lambda s: pl.BlockSpec(s, lambda b, j: (0, 0))
    in_specs = [feat(hw), seqblk] + [const((1, C_HALF))] * 4 + [const((1, hw))]
    args = [qT, gate, *lams, subrow]
    if P > 0:
        in_specs += [pastspec(hw), pastspec(hw)]
        args += list(past)
    in_specs += [new_row(hw), new_col(hw) if new_fm else new_row(hw)]
    args += list(new)
    return pl.pallas_call(
        functools.partial(_diff_kernel, P=P, nb=nb, tqr=tqr, tkp=_past_tile(P, tq), tkn=tqr,
                          lam_init=lam_init, new_fm=new_fm),
        grid=(B // nb, L // tqr),
        in_specs=in_specs,
        out_specs=seqblk,
        out_shape=jax.ShapeDtypeStruct((B, L, hw), F32),
        scratch_shapes=[pltpu.VMEM((nb * hw, nmaps * tq), BF16),
                        pltpu.VMEM((nstate, 1, wstate), F32),
                        pltpu.VMEM((nstate, nb * C_HEAD_DIM + ONES_ROWS, wstate), F32)],
        compiler_params=_cparams(("parallel", "arbitrary")),
    )(*args)


def _outproj_kernel(ya_ref, yb_ref, yc_ref, x_ref, w_ref, g_ref, o_ref, *, aw, hw):
    mix = jnp.dot(ya_ref[...].astype(BF16), w_ref[0:aw, :], preferred_element_type=F32)
    mix = mix + jnp.dot(yb_ref[...].astype(BF16), w_ref[aw:aw + hw, :], preferred_element_type=F32)
    mix = mix + jnp.dot(yc_ref[...].astype(BF16), w_ref[aw + hw:aw + 2 * hw, :], preferred_element_type=F32)
    ms = jnp.mean(mix * mix, axis=-1, keepdims=True)
    o_ref[...] = x_ref[...] + mix * lax.rsqrt(ms + EPS) * g_ref[...]


def _outproj(ya, yb, yc, x2d, w, g, *, tm):
    T, D = x2d.shape
    aw, hw = D // 2, D // 4
    row = lambda n: pl.BlockSpec((tm, n), lambda i: (i, 0))
    const = lambda s: pl.BlockSpec(s, lambda i: (0, 0))
    return pl.pallas_call(
        functools.partial(_outproj_kernel, aw=aw, hw=hw),
        grid=(T // tm,),
        in_specs=[row(aw), row(hw), row(hw), row(D), const((D, D)), const((1, D))],
        out_specs=row(D),
        out_shape=jax.ShapeDtypeStruct((T, D), F32),
        compiler_params=_cparams(("parallel",)),
    )(ya, yb, yc, x2d, w, g)


_ROW_PROMPT = ("a_x", "a_g", "b_g", "c_g", "b_k", "c_k", "b_ik")
_FEAT_PROMPT = ("b_q", "b_iq", "c_q", "b_k", "b_v", "c_k", "c_v", "b_ik", "b_iw")
_ROW_SAMPLE = ("a_x", "a_g", "b_g", "c_g", "b_k", "b_v", "c_k", "c_v", "b_ik")
_FEAT_SAMPLE = ("b_q", "b_iq", "c_q", "b_ik", "b_iw")
_BF16_OUT = {"b_q", "b_iq", "c_q"}
_PROMPT_STATES = ("b_k", "b_v", "c_k", "c_v", "b_ik")


def _pack_w_in(w_in, D, rows, feats, row_bf16, stacked):
    aw, hw = D // 2, D // 4
    sizes = (aw, aw, hw, hw, hw, hw, IDX_HEADS * IDX_DIM, IDX_DIM, IDX_HEADS, hw, hw, hw, hw)
    names = ("a_x", "a_g", "b_q", "b_k", "b_v", "b_g", "b_iq", "b_ik", "b_iw", "c_q", "c_k", "c_v", "c_g")
    scale = {"b_q": B_HEAD_DIM ** -0.5 * LOG2E, "c_q": C_HALF ** -0.5 * LOG2E,
             "b_iw": (IDX_DIM ** -0.5) * (IDX_HEADS ** -0.5)}
    seg, o = {}, 0
    for n, s in zip(names, sizes):
        seg[n] = w_in[..., o:o + s] * scale[n] if n in scale else w_in[..., o:o + s]
        o += s
    depth = w_in.shape[0]

    def cat(parts, mult):
        w = jnp.concatenate([seg[n] for n in parts], axis=-1)
        pad = (-w.shape[-1]) % mult
        return jnp.concatenate([w, jnp.zeros((depth, D, pad), w.dtype)], axis=-1) if pad else w
    wr = cat(rows, 128).astype(BF16)
    wf = jnp.swapaxes(cat(feats, 16), 1, 2).astype(BF16)
    row_plan = tuple((seg[n].shape[-1], BF16 if n in row_bf16 else F32) for n in rows)
    feat_plan = tuple((seg[n].shape[-1], BF16 if n in _BF16_OUT else F32, n in stacked) for n in feats)
    return wr, wf, row_plan, feat_plan


def _block_diag(w):
    depth, nb, n, _ = w.shape
    eye = jnp.eye(nb, dtype=w.dtype)
    return jnp.einsum("lnde,nm->lndme", w, eye).reshape(depth, nb * n, nb * n)


def _prompt_layer(x, B, L, layer, wl, zero_buf, zero_h, lam_init, stacks):
    T, D = x.shape
    hw = D // 4
    tm = min(PROJ_ROWS, L)
    out = _inproj(x, wl["norm_pre"], wl["wr_p"], wl["wf_p"], tm=tm, row_plan=wl["plan_p"][0],
                  feat_plan=wl["plan_p"][1], feat_batch=(B, L), layer=layer, stacks=stacks)
    ax, ag, bg, cg, bk_bf, ck_bf, bik = out[:7]
    bqT, biqT, cqT, bkT, bvT, ckT, cvT, bikT, biwT = out[7:]
    r3 = lambda a: a.reshape(B, L, a.shape[-1])
    ya, nbuf, hlast = _amix(r3(ax), r3(ag), zero_buf, zero_h[:, None, :], wl["a_conv_w"], wl["a_conv_b"],
                            wl["rw"], wl["a_rg_b"], wl["iw"], wl["a_in_b"], wl["a_lambda"], tl=min(L, 512))
    tqr = min(512, L)
    kw = dict(B=B, L=L, P=0, nb=1, tqr=tqr, layer=layer, new_fm=True)
    yb = _dsa(bqT, biqT, biwT, r3(bg), None, (r3(bk_bf), bvT, r3(bik)), **kw)
    yc = _diff(cqT, r3(cg), wl["lams"], wl["subrow"], None, (r3(ck_bf), cvT), lam_init=lam_init, **kw)
    x_new = _outproj(ya.reshape(T, -1), yb.reshape(T, hw), yc.reshape(T, hw), x, wl["w_out"], wl["norm_post"],
                     tm=tm)
    return x_new, (nbuf, hlast[:, 0, :]), (bkT, bvT, ckT, cvT, bikT)


def _sample_layer(x, B, L, P, layer, wl, conv_buf, h0, past, lam_init):
    T, D = x.shape
    hw = D // 4
    tm = min(PROJ_ROWS, T)
    out = _inproj(x, wl["norm_pre"], wl["wr_s"], wl["wf_s"], tm=tm, row_plan=wl["plan_s"][0],
                  feat_plan=wl["plan_s"][1], feat_batch=None)
    ax, ag, bg, cg, bk, bv, ck, cv, bik = out[:9]
    bqT, biqT, cqT, bikT, biwT = out[9:]
    r3 = lambda a: a.reshape(B, L, a.shape[-1])
    ya, nbuf, hlast = _amix(r3(ax), r3(ag), conv_buf, h0[:, None, :], wl["a_conv_w"], wl["a_conv_b"],
                            wl["rw"], wl["a_rg_b"], wl["iw"], wl["a_in_b"], wl["a_lambda"], tl=L)
    nb = 2 if B % 2 == 0 else 1
    kw = dict(B=B, L=L, P=P, nb=nb, tqr=L, layer=layer, new_fm=False)
    pkT, pvT, pkiT, pckT, pcvT = past
    yb = _dsa(bqT, biqT, biwT, r3(bg), (pkT, pvT, pkiT), (r3(bk), r3(bv), r3(bik)), **kw)
    yc = _diff(cqT, r3(cg), wl["lams"], wl["subrow"], (pckT, pcvT), (r3(ck), r3(cv)), lam_init=lam_init, **kw)
    x_new = _outproj(ya.reshape(T, -1), yb.reshape(T, hw), yc.reshape(T, hw), x, wl["w_out"], wl["norm_post"],
                     tm=tm)
    states = (nbuf, hlast[:, 0, :],
              bk.reshape(B, L, B_HEADS, B_HEAD_DIM), bv.reshape(B, L, B_HEADS, B_HEAD_DIM),
              bikT.reshape(IDX_DIM, B, L).transpose(1, 2, 0),
              ck.reshape(B, L, C_HEADS, C_HEAD_DIM), cv.reshape(B, L, C_HEADS, C_HEAD_DIM))
    return x_new, states


def kernel(x_prompt, x_sample, cache_a_conv, state_a_h, cache_b_k, cache_b_v, cache_b_kidx, cache_c_k, cache_c_v, norm_pre, norm_post, w_in, w_out, a_conv_w, a_conv_b, a_rg_w, a_rg_b, a_in_w, a_in_b, a_lambda, c_lam_q1, c_lam_k1, c_lam_q2, c_lam_k2, c_subln):
    Bp, Lp, D = x_prompt.shape
    Bs, Ls, _ = x_sample.shape
    depth = w_in.shape[0]
    P = cache_b_k.shape[2]
    aw = D // 2

    wr_p, wf_p, *plan_p = _pack_w_in(w_in, D, _ROW_PROMPT, _FEAT_PROMPT, {"b_k", "c_k"}, _PROMPT_STATES)
    wr_s, wf_s, *plan_s = _pack_w_in(w_in, D, _ROW_SAMPLE, _FEAT_SAMPLE, set(), ())
    rw = _block_diag(a_rg_w).astype(BF16)
    iw = _block_diag(a_in_w).astype(BF16)
    wo = w_out.astype(BF16)

    fm = lambda c: jnp.transpose(c, (0, 1, 3, 4, 2)).reshape(depth, Bs, -1, P)
    past = (fm(cache_b_k), fm(cache_b_v), jnp.transpose(cache_b_kidx, (0, 1, 3, 2)), fm(cache_c_k), fm(cache_c_v))

    xp = x_prompt.reshape(Bp * Lp, D)
    xs = x_sample.reshape(Bs * Ls, D)
    zero_buf = jnp.zeros((Bp, CONV_W - 1, aw), F32)
    zero_h = jnp.zeros((Bp, aw), F32)
    stacks = tuple(jnp.zeros((depth, Bp, n, Lp), F32) for n, _, stacked in plan_p[1] if stacked)
    p_st, s_st = [], []
    for l in range(depth):
        row = lambda a: a[l][None, :]
        wl = {"norm_pre": row(norm_pre), "norm_post": row(norm_post), "w_out": wo[l],
              "wr_p": wr_p[l], "wf_p": wf_p[l], "plan_p": plan_p, "wr_s": wr_s[l], "wf_s": wf_s[l], "plan_s": plan_s,
              "a_conv_w": a_conv_w[l], "a_conv_b": row(a_conv_b), "rw": rw[l], "a_rg_b": row(a_rg_b),
              "iw": iw[l], "a_in_b": row(a_in_b), "a_lambda": row(a_lambda),
              "lams": (row(c_lam_q1), row(c_lam_k1), row(c_lam_q2), row(c_lam_k2)),
              "subrow": jnp.tile(c_subln[l], C_HEADS)[None, :]}
        lam_init = 0.8 - 0.6 * math.exp(-0.3 * l)
        xp, st_p, stacks = _prompt_layer(xp, Bp, Lp, l, wl, zero_buf, zero_h, lam_init, stacks)
        xs, st_s = _sample_layer(xs, Bs, Ls, P, l, wl, cache_a_conv[l], state_a_h[l], past, lam_init)
        p_st.append(st_p)
        s_st.append(st_s)

    stk = lambda states, i: jnp.stack([s[i] for s in states], axis=0)
    heads = lambda t, nh: t.reshape(depth, Bp, nh, -1, Lp).transpose(0, 1, 4, 2, 3)
    bkT, bvT, ckT, cvT, bikT = stacks
    return (xp.reshape(Bp, Lp, D), xs.reshape(Bs, Ls, D),
            stk(p_st, 0), stk(p_st, 1), heads(bkT, B_HEADS), heads(bvT, B_HEADS), bikT.transpose(0, 1, 3, 2),
            heads(ckT, C_HEADS), heads(cvT, C_HEADS), *[stk(s_st, i) for i in range(7)])
```
